```python
import math
import jax, jax.numpy as jnp
from jax import lax
import numpy as np

D_MODEL = 1024
BATCH = 16
SEQ = 2048
DEPTH = 2
DEC_BATCH = 32
DEC_SEQ = 8
PAST_LEN = 16384
PAGE_SIZE = 128

GLA_HEADS = 4
GLA_DK = D_MODEL // 2
GLA_DV = D_MODEL
GLA_DK_HEAD = GLA_DK // GLA_HEADS
GLA_DV_HEAD = GLA_DV // GLA_HEADS
GATE_RANK = 16
GATE_TAU = 16.0
GLA_CHUNK = 64
WINDOWS = (128, 512, 2048)
DILATIONS = (1, 4, 16)
N_GROUPS = 3
HEAD_DIM = 64
HEADS_PER_GROUP = D_MODEL // HEAD_DIM
KV_HEADS = 4
Q_PER_KV = HEADS_PER_GROUP // KV_HEADS
Q_BLOCK = 128
N_BUCKETS = 32
MAX_EXACT = 16
MAX_DISTANCE = 2048
D_FF = -(-8 * D_MODEL // (3 * 256)) * 256
EPS = 1e-6

kernel_name = 'yoco_gla_dilated_swa_decode_step'


def rmsnorm(x, g):
    x32 = x.astype(jnp.float32)
    y = x32 * lax.rsqrt(jnp.mean(x32 * x32, axis=-1, keepdims=True) + EPS)
    return (y * g.astype(jnp.float32)).astype(x.dtype)


def swiglu(u, w_gate_up, w_down):
    gate, up = jnp.split(u @ w_gate_up, 2, axis=-1)
    return (jax.nn.silu(gate) * up) @ w_down


def t5_buckets(dist):
    d = np.asarray(dist)
    large = MAX_EXACT + (np.log(np.maximum(d, 1) / MAX_EXACT) / np.log(MAX_DISTANCE / MAX_EXACT)
                         * (N_BUCKETS - MAX_EXACT)).astype(np.int64)
    large = np.minimum(large, N_BUCKETS - 1)
    return np.where(d < MAX_EXACT, d, large).astype(np.int32)


def gla_recurrence(q, k, v, g, s0):
    B, T, H, DK = q.shape
    DV = v.shape[-1]
    c = math.gcd(T, GLA_CHUNK)
    n = T // c

    def blocks(a):
        return a.astype(jnp.float32).reshape(B, n, c, H, a.shape[-1]).transpose(1, 0, 3, 2, 4)

    causal = jnp.tril(jnp.ones((c, c), dtype=bool))

    def step(s, inp):
        qc, kc, vc, gc = inp
        b = jnp.cumsum(gc, axis=2)
        o_inter = jnp.einsum('bhid,bhde->bhie', qc * jnp.exp(b), s)
        diff = b[:, :, :, None, :] - b[:, :, None, :, :]
        decay = jnp.exp(jnp.where(causal[:, :, None], diff, -jnp.inf))
        attn = jnp.sum(qc[:, :, :, None, :] * kc[:, :, None, :, :] * decay, axis=-1)
        o_intra = jnp.einsum('bhij,bhje->bhie', attn, vc)
        b_last = b[:, :, -1, :]
        s_new = jnp.exp(b_last)[..., None] * s + jnp.einsum(
            'bhjd,bhje->bhde', kc * jnp.exp(b_last[:, :, None, :] - b), vc)
        return s_new, o_inter + o_intra

    s, o = lax.scan(step, s0.astype(jnp.float32), (blocks(q), blocks(k), blocks(v), blocks(g)))
    o = o.transpose(1, 0, 3, 2, 4).reshape(B, T, H, DV)
    return o, s


def gla_mixer(u, s0, w_in, w_a2, b_a, g_onorm, w_o):
    B, T, _ = u.shape
    p = u @ w_in
    q, k, v, r, a = jnp.split(p, [GLA_DK, 2 * GLA_DK, 2 * GLA_DK + GLA_DV, 2 * GLA_DK + 2 * GLA_DV], axis=-1)
    q = q.reshape(B, T, GLA_HEADS, GLA_DK_HEAD) * GLA_DK_HEAD ** -0.5
    k = k.reshape(B, T, GLA_HEADS, GLA_DK_HEAD)
    v = v.reshape(B, T, GLA_HEADS, GLA_DV_HEAD)
    g = jax.nn.log_sigmoid((a @ w_a2 + b_a).astype(jnp.float32)) / GATE_TAU
    g = g.reshape(B, T, GLA_HEADS, GLA_DK_HEAD)
    o, s = gla_recurrence(q, k, v, g, s0)
    o = rmsnorm(o, g_onorm).astype(u.dtype).reshape(B, T, GLA_DV)
    o = o * jax.nn.silu(r)
    return o @ w_o, s.astype(u.dtype)


def dilated_group(q, kv, start, dil, n_keys, bias):
    Tq = q.shape[1]
    q_pos = start + jnp.arange(Tq)
    idx = q_pos[:, None] - dil * jnp.arange(n_keys)[None, :]
    valid = idx >= 0
    kvg = kv[:, jnp.maximum(idx, 0)]
    s = jnp.einsum('bqhgd,bqkhd->bqhgk', q, kvg[:, :, :, 0]).astype(jnp.float32) * HEAD_DIM ** -0.5
    s = jnp.where(valid[None, :, None, None, :], s + bias, -jnp.inf)
    m = jnp.max(s, axis=-1, keepdims=True)
    p = jnp.exp(s - m)
    l = jnp.sum(p, axis=-1, keepdims=True)
    o = jnp.einsum('bqhgk,bqkhd->bqhgd', (p / l).astype(kv.dtype), kvg[:, :, :, 1])
    lse = (m + jnp.log(l))[..., 0]
    return o, lse


def mix_groups(q, starts, kvs, biases):
    outs, lses = [], []
    for gi in range(N_GROUPS):
        o, lse = dilated_group(q[:, :, gi], kvs[gi], starts[gi], DILATIONS[gi],
                               WINDOWS[gi] // DILATIONS[gi] + 1, biases[gi])
        outs.append(o.astype(jnp.float32))
        lses.append(lse)
    w = jax.nn.softmax(jnp.stack(lses, axis=0), axis=0)
    o = jnp.sum(w[..., None] * jnp.stack(outs, axis=0), axis=0)
    return o.astype(q.dtype)


def dilated_mixer(u, kvs, starts, w_q, w_o, biases):
    B, T, _ = u.shape
    q = (u @ w_q).reshape(B, T, N_GROUPS, KV_HEADS, Q_PER_KV, HEAD_DIM)
    if starts is None:
        nblk = T // Q_BLOCK
        qb = q.reshape(B, nblk, Q_BLOCK, N_GROUPS, KV_HEADS, Q_PER_KV, HEAD_DIM).swapaxes(0, 1)

        def blk(args):
            qi, i = args
            start = i * Q_BLOCK
            return mix_groups(qi, (start,) * N_GROUPS, kvs, biases)

        o = lax.map(blk, (qb, jnp.arange(nblk)))
        o = o.swapaxes(0, 1).reshape(B, T, HEADS_PER_GROUP * HEAD_DIM)
    else:
        o = mix_groups(q, starts, kvs, biases).reshape(B, T, HEADS_PER_GROUP * HEAD_DIM)
    return o @ w_o


def trunk(x, gla_s0, win_past, norm_g, w_in_a, w_a2, b_a, g_onorm, w_o_a, g_kv, w_kv,
          w_q_b, w_o_b, rel_bias, w_gate_up, w_down):
    n_a = DEPTH // 2
    B, T, _ = x.shape
    biases = []
    for gi in range(N_GROUPS):
        nk = WINDOWS[gi] // DILATIONS[gi] + 1
        bk = jnp.asarray(t5_buckets(DILATIONS[gi] * np.arange(nk)))
        tab = rel_bias[bk][:, gi * HEADS_PER_GROUP:(gi + 1) * HEADS_PER_GROUP]
        biases.append(tab.T.reshape(KV_HEADS, Q_PER_KV, nk).astype(jnp.float32))
    h = x
    gla_states = []
    kvs, starts, win_new = None, None, None
    for l in range(DEPTH):
        if l < n_a:
            y, s = gla_mixer(rmsnorm(h, norm_g[l, 0]), gla_s0[l], w_in_a[l], w_a2[l], b_a[l],
                             g_onorm[l], w_o_a[l])
            gla_states.append(s)
        else:
            if l == n_a:
                kv = (rmsnorm(h, g_kv) @ w_kv).reshape(B, T, N_GROUPS, 2, KV_HEADS, HEAD_DIM)
                kvs, starts, win_new = [], [], []
                for gi in range(N_GROUPS):
                    new = kv[:, :, gi]
                    if win_past is None:
                        full, start = new, 0
                    else:
                        full = jnp.concatenate([win_past[gi].astype(new.dtype), new], axis=1)
                        start = win_past[gi].shape[1]
                    kvs.append(full)
                    starts.append(start)
                    win_new.append(full[:, full.shape[1] - min(WINDOWS[gi], full.shape[1]):])
            y = dilated_mixer(rmsnorm(h, norm_g[l, 0]), kvs, None if win_past is None else tuple(starts),
                              w_q_b[l - n_a], w_o_b[l - n_a], biases)
        h = h + rmsnorm(y, norm_g[l, 1])
        h = h + rmsnorm(swiglu(rmsnorm(h, norm_g[l, 2]), w_gate_up[l], w_down[l]), norm_g[l, 3])
    return h, jnp.stack(gla_states, axis=0), win_new


def setup_inputs(seed: int = 0) -> dict:
    key = jax.random.key(seed)
    ks = jax.random.split(key, 20)
    n_a = DEPTH // 2
    n_b = DEPTH - n_a
    f32 = jnp.float32
    nrm = lambda k, shp, sc: jax.random.normal(k, shp, f32) * sc
    x_prompt = nrm(ks[0], (BATCH, SEQ, D_MODEL), 1.0)
    x_sample = nrm(ks[1], (DEC_BATCH, DEC_SEQ, D_MODEL), 1.0)
    state_gla = nrm(ks[2], (n_a, DEC_BATCH, GLA_HEADS, GLA_DK_HEAD, GLA_DV_HEAD), 0.5)
    cache_win1 = nrm(ks[3], (DEC_BATCH, min(WINDOWS[0], PAST_LEN), 2, KV_HEADS, HEAD_DIM), 1.0)
    cache_win2 = nrm(ks[4], (DEC_BATCH, min(WINDOWS[1], PAST_LEN), 2, KV_HEADS, HEAD_DIM), 1.0)
    cache_win3 = nrm(ks[5], (DEC_BATCH, min(WINDOWS[2], PAST_LEN), 2, KV_HEADS, HEAD_DIM), 1.0)
    norm_g = 1.0 + nrm(ks[6], (DEPTH, 4, D_MODEL), 0.02)
    w_in_a = nrm(ks[7], (n_a, D_MODEL, 2 * GLA_DK + 2 * GLA_DV + GATE_RANK), D_MODEL ** -0.5)
    w_a2 = nrm(ks[8], (n_a, GATE_RANK, GLA_DK), GATE_RANK ** -0.5)
    b_a = nrm(ks[9], (n_a, GLA_DK), 0.1)
    g_onorm = 1.0 + nrm(ks[10], (n_a, GLA_DV_HEAD), 0.02)
    w_o_a = nrm(ks[11], (n_a, GLA_DV, D_MODEL), GLA_DV ** -0.5)
    g_kv = 1.0 + nrm(ks[12], (D_MODEL,), 0.02)
    w_kv = nrm(ks[13], (D_MODEL, N_GROUPS * 2 * KV_HEADS * HEAD_DIM), D_MODEL ** -0.5)
    w_q_b = nrm(ks[14], (n_b, D_MODEL, N_GROUPS * HEADS_PER_GROUP * HEAD_DIM), D_MODEL ** -0.5)
    w_o_b = nrm(ks[15], (n_b, HEADS_PER_GROUP * HEAD_DIM, D_MODEL), (HEADS_PER_GROUP * HEAD_DIM) ** -0.5)
    rel_bias = nrm(ks[16], (N_BUCKETS, N_GROUPS * HEADS_PER_GROUP), 0.1)
    w_gate_up = nrm(ks[17], (DEPTH, D_MODEL, 2 * D_FF), D_MODEL ** -0.5)
    w_down = nrm(ks[18], (DEPTH, D_FF, D_MODEL), D_FF ** -0.5)
    return {'x_prompt': x_prompt, 'x_sample': x_sample, 'state_gla': state_gla,
            'cache_win1': cache_win1, 'cache_win2': cache_win2, 'cache_win3': cache_win3,
            'norm_g': norm_g, 'w_in_a': w_in_a, 'w_a2': w_a2, 'b_a': b_a, 'g_onorm': g_onorm,
            'w_o_a': w_o_a, 'g_kv': g_kv, 'w_kv': w_kv, 'w_q_b': w_q_b, 'w_o_b': w_o_b,
            'rel_bias': rel_bias, 'w_gate_up': w_gate_up, 'w_down': w_down}


def reference(x_prompt, x_sample, state_gla, cache_win1, cache_win2, cache_win3, norm_g, w_in_a,
              w_a2, b_a, g_onorm, w_o_a, g_kv, w_kv, w_q_b, w_o_b, rel_bias, w_gate_up, w_down):
    n_a = DEPTH // 2
    s0_prompt = jnp.zeros((n_a, x_prompt.shape[0], GLA_HEADS, GLA_DK_HEAD, GLA_DV_HEAD), x_prompt.dtype)
    y_prompt, gla_prompt, win_prompt = trunk(
        x_prompt, s0_prompt, None, norm_g, w_in_a, w_a2, b_a, g_onorm, w_o_a, g_kv, w_kv,
        w_q_b, w_o_b, rel_bias, w_gate_up, w_down)
    y_sample, gla_sample, win_sample = trunk(
        x_sample, state_gla, (cache_win1, cache_win2, cache_win3), norm_g, w_in_a, w_a2, b_a,
        g_onorm, w_o_a, g_kv, w_kv, w_q_b, w_o_b, rel_bias, w_gate_up, w_down)
    return (y_prompt, y_sample, gla_prompt, win_prompt[0], win_prompt[1], win_prompt[2],
            gla_sample, win_sample[0], win_sample[1], win_sample[2])
```

```python
import functools

import numpy as np
import jax
import jax.numpy as jnp
from jax import lax
from jax.experimental import pallas as pl
from jax.experimental.pallas import tpu as pltpu

BF = jnp.bfloat16
F32 = jnp.float32

D_MODEL = 1024
GLA_HEADS = 4
GLA_DK = 512
GLA_DV = 1024
DKH = GLA_DK // GLA_HEADS
DVH = GLA_DV // GLA_HEADS
GATE_RANK = 16
GATE_TAU = 16.0
GLA_CHUNK = 64
GLA_SUB = 16
WINDOWS = (128, 512, 2048)
DILATIONS = (1, 4, 16)
N_GROUPS = 3
HEAD_DIM = 64
HEADS_PER_GROUP = 16
KV_HEADS = 4
Q_PER_KV = 4
N_KEYS = 129
N_BUCKETS = 32
MAX_EXACT = 16
MAX_DISTANCE = 2048
D_FF = 2816
EPS = 1e-6
NEG = -1e30
QB = 128
LANES = 128
VMEM_LIMIT_BYTES = 56 * 1024 * 1024


def _dot(a, b):
    return jnp.dot(a, b, preferred_element_type=F32)


def _dot_nt(a, b):
    return lax.dot_general(a, b, (((1,), (1,)), ((), ())), preferred_element_type=F32)


def _rms(x, g):
    return x * lax.rsqrt(jnp.mean(x * x, axis=-1, keepdims=True) + EPS) * g


def _sigmoid(x):
    return 1.0 / (1.0 + jnp.exp(-x))


def _const_spec(shape):
    nd = len(shape)
    return pl.BlockSpec(shape, lambda *_: (0,) * nd, pipeline_mode=pl.Buffered(1))


def _params(*sem):
    return pltpu.CompilerParams(dimension_semantics=sem, vmem_limit_bytes=VMEM_LIMIT_BYTES)


def _row_tile(n, want):
    tm = min(n, want)
    assert n % tm == 0
    return tm


def _gla_in_body(x_ref, ng_ref, wm_ref, wa_ref, wa2_ref, ba_ref, q_ref, k_ref, v_ref, r_ref, g_ref):
    xn = _rms(x_ref[...], ng_ref[0:1, :]).astype(BF)
    q_ref[...] = _dot(xn, wm_ref[:, 0:GLA_DK]) * (DKH ** -0.5)
    k_ref[...] = _dot(xn, wm_ref[:, GLA_DK:2 * GLA_DK])
    v_ref[...] = _dot(xn, wm_ref[:, 2 * GLA_DK:2 * GLA_DK + GLA_DV])
    r_ref[...] = _dot(xn, wm_ref[:, 2 * GLA_DK + GLA_DV:])
    a = _dot(xn, wa_ref[...]).astype(BF)
    z = _dot(a, wa2_ref[...]) + ba_ref[...]
    g_ref[...] = (jnp.minimum(z, 0.0) - jnp.log(1.0 + jnp.exp(-jnp.abs(z)))) * (1.0 / GATE_TAU)


def _gla_in(x2, ng, wm, wa, wa2, ba):
    n = x2.shape[0]
    tm = _row_tile(n, 512)
    row = lambda w: pl.BlockSpec((tm, w), lambda i: (i, 0))
    return pl.pallas_call(
        _gla_in_body,
        grid=(n // tm,),
        in_specs=[row(D_MODEL), _const_spec(ng.shape), _const_spec(wm.shape), _const_spec(wa.shape),
                  _const_spec(wa2.shape), _const_spec(ba.shape)],
        out_specs=[row(GLA_DK), row(GLA_DK), row(GLA_DV), row(GLA_DV), row(GLA_DK)],
        out_shape=[jax.ShapeDtypeStruct((n, w), F32) for w in (GLA_DK, GLA_DK, GLA_DV, GLA_DV, GLA_DK)],
        compiler_params=_params("parallel"),
        name="gla_in",
    )(x2, ng, wm, wa, wa2, ba)


def _gla_body(c_real, q_ref, k_ref, v_ref, g_ref, s0_ref, o_ref, st_ref, s_scr):
    t = pl.program_id(1)
    c = GLA_CHUNK
    nsb = c // GLA_SUB
    n_chunks = q_ref.shape[1] // c_real

    @pl.when(t == 0)
    def _():
        s_scr[...] = s0_ref[0]

    ri = lax.broadcasted_iota(jnp.int32, (c, c), 0)
    ci = lax.broadcasted_iota(jnp.int32, (c, c), 1)
    tril_bf = jnp.where(ri >= ci, 1.0, 0.0).astype(BF)
    ri2 = lax.broadcasted_iota(jnp.int32, (c, LANES), 0)
    ci2 = lax.broadcasted_iota(jnp.int32, (c, LANES), 1)
    causal = ri2 >= ci2

    def pad_rows(a, rows):
        if a.shape[0] == rows:
            return a
        return jnp.concatenate([a, jnp.zeros((rows - a.shape[0], a.shape[1]), a.dtype)], axis=0)

    def chunk(idx, carry):
        rows = pl.ds(pl.multiple_of(idx * c_real, c_real), c_real)
        g_all = pad_rows(g_ref[0, rows, :], c)
        g_hi = g_all.astype(BF)
        g_lo = (g_all - g_hi.astype(F32)).astype(BF)
        b_all = _dot(tril_bf, g_hi) + _dot(tril_bf, g_lo)
        for h in range(GLA_HEADS):
            ks = slice(h * DKH, (h + 1) * DKH)
            vs = slice(h * DVH, (h + 1) * DVH)
            b = b_all[:, ks]
            qh = pad_rows(q_ref[0, rows, ks], c)
            kh = pad_rows(k_ref[0, rows, ks], c)
            vh = pad_rows(v_ref[0, rows, vs], LANES)
            b_last = b[c - 1:c, :]
            st = s_scr[h]
            o_inter = _dot_nt((qh * jnp.exp(b)).astype(BF), st.astype(BF))
            qparts, kparts = [], []
            for j in range(nsb):
                lo, hi = j * GLA_SUB, (j + 1) * GLA_SUB
                ref_row = b[lo:lo + 1, :]
                qparts.append((qh * jnp.exp(jnp.minimum(b - ref_row, 0.0))).astype(BF))
                kj = (kh[lo:hi] * jnp.exp(ref_row - b[lo:hi])).astype(BF)
                pieces = []
                if lo > 0:
                    pieces.append(jnp.zeros((lo, DKH), BF))
                pieces.append(kj)
                pieces.append(jnp.zeros((LANES - hi, DKH), BF))
                kparts.append(jnp.concatenate(pieces, axis=0))
            qcat = jnp.concatenate(qparts, axis=1)
            kcat = jnp.concatenate(kparts, axis=1)
            attn = jnp.where(causal, _dot_nt(qcat, kcat), 0.0).astype(BF)
            o = o_inter + _dot(attn, vh.astype(BF))
            o_ref[0, rows, vs] = o[0:c_real]
            k2 = pad_rows((kh * jnp.exp(b_last - b)).astype(BF), LANES)
            vt = vh.T.astype(BF)
            s_scr[h] = st * jnp.exp(b_last) + _dot(vt, k2)
        return carry

    lax.fori_loop(0, n_chunks, chunk, 0)

    @pl.when(t == pl.num_programs(1) - 1)
    def _():
        st_ref[0] = s_scr[...]


def _gla(q, k, v, g, s0t):
    bsz, t, _ = q.shape
    c_real = min(t, GLA_CHUNK)
    tc = min(t, 512)
    assert t % tc == 0 and tc % c_real == 0
    blk = lambda w: pl.BlockSpec((1, tc, w), lambda b, i: (b, i, 0))
    st_spec = pl.BlockSpec((1, GLA_HEADS, DVH, DKH), lambda b, i: (b, 0, 0, 0))
    return pl.pallas_call(
        functools.partial(_gla_body, c_real),
        grid=(bsz, t // tc),
        in_specs=[blk(GLA_DK), blk(GLA_DK), blk(GLA_DV), blk(GLA_DK), st_spec],
        out_specs=[blk(GLA_DV), st_spec],
        out_shape=[jax.ShapeDtypeStruct((bsz, t, GLA_DV), F32),
                   jax.ShapeDtypeStruct((bsz, GLA_HEADS, DVH, DKH), F32)],
        scratch_shapes=[pltpu.VMEM((GLA_HEADS, DVH, DKH), F32)],
        compiler_params=_params("parallel", "arbitrary"),
        name="gla",
    )(q, k, v, g, s0t)


def _mix_ffn_body(gated, *refs):
    if gated:
        o_ref, r_ref, x_ref, gon_ref, wo_ref, ng_ref, wgu_ref, wd_ref, out_ref = refs
        o = o_ref[...]
        r = r_ref[...]
        on = jnp.concatenate(
            [_rms(o[:, h * DVH:(h + 1) * DVH], gon_ref[...]) for h in range(GLA_HEADS)], axis=1)
        m = on * (r * _sigmoid(r))
    else:
        m_ref, x_ref, wo_ref, ng_ref, wgu_ref, wd_ref, out_ref = refs
        m = m_ref[...]
    y = _dot(m.astype(BF), wo_ref[...])
    h1 = x_ref[...] + _rms(y, ng_ref[1:2, :])
    u = _rms(h1, ng_ref[2:3, :]).astype(BF)
    gu = _dot(u, wgu_ref[...])
    gate = gu[:, :D_FF]
    act = (gate * _sigmoid(gate) * gu[:, D_FF:]).astype(BF)
    f = _dot(act, wd_ref[...])
    out_ref[...] = h1 + _rms(f, ng_ref[3:4, :])


def _mix_ffn(mix_inputs, x2, gon, wo, ng, wgu, wd):
    n = x2.shape[0]
    tm = _row_tile(n, 256)
    row = pl.BlockSpec((tm, D_MODEL), lambda i: (i, 0))
    gated = gon is not None
    args = list(mix_inputs) + [x2] + ([gon] if gated else []) + [wo, ng, wgu, wd]
    in_specs = [row] * (len(mix_inputs) + 1) + [_const_spec(a.shape) for a in args[len(mix_inputs) + 1:]]
    return pl.pallas_call(
        functools.partial(_mix_ffn_body, gated),
        grid=(n // tm,),
        in_specs=in_specs,
        out_specs=row,
        out_shape=jax.ShapeDtypeStruct((n, D_MODEL), F32),
        compiler_params=_params("parallel"),
        name="mix_ffn_gated" if gated else "mix_ffn",
    )(*args)


def _qkv_body(h_ref, gkv_ref, ng_ref, wkv_ref, wq_ref, kv_ref, q_ref):
    h = h_ref[...]
    hn = h * lax.rsqrt(jnp.mean(h * h, axis=-1, keepdims=True) + EPS)
    kv_ref[...] = _dot((hn * gkv_ref[...]).astype(BF), wkv_ref[...])
    q_ref[...] = _dot((hn * ng_ref[0:1, :]).astype(BF), wq_ref[...])


def _qkv(h2, gkv, ng, wkv, wq):
    n = h2.shape[0]
    tm = _row_tile(n, 512)
    row = lambda w: pl.BlockSpec((tm, w), lambda i: (i, 0))
    nkv, nq = wkv.shape[1], wq.shape[1]
    return pl.pallas_call(
        _qkv_body,
        grid=(n // tm,),
        in_specs=[row(D_MODEL), _const_spec(gkv.shape), _const_spec(ng.shape), _const_spec(wkv.shape),
                  _const_spec(wq.shape)],
        out_specs=[row(nkv), row(nq)],
        out_shape=[jax.ShapeDtypeStruct((n, nkv), F32), jax.ShapeDtypeStruct((n, nq), F32)],
        compiler_params=_params("parallel"),
        name="qkv",
    )(h2, gkv, ng, wkv, wq)


def _merge_groups(o_parts, lse_parts):
    mx = jnp.maximum(jnp.maximum(lse_parts[0], lse_parts[1]), lse_parts[2])
    es = [jnp.exp(l - mx) for l in lse_parts]
    num = es[0] * o_parts[0] + es[1] * o_parts[1] + es[2] * o_parts[2]
    return num / (es[0] + es[1] + es[2])


def _attn_prompt_body(q00, q01, q10, q11, q20, q21, kv0, kv1, kv2, bias_ref, o_ref, o_scr, l_scr):
    t_len = kv0.shape[1]
    lane = lax.broadcasted_iota(jnp.int32, (1, LANES), 1)
    low = lane < HEAD_DIM

    def rows_of(start, d):
        if d > 1:
            return pl.ds(start, QB, stride=d)
        return pl.ds(start if isinstance(start, int) else pl.multiple_of(start, QB), QB)

    def block(g, q_refs, kv_ref, d, start_cur, start_prev):
        rows_c = rows_of(start_cur, d)
        kvf = kv_ref[0, rows_c, :]
        if start_prev is not None:
            kvf = jnp.concatenate([kv_ref[0, rows_of(start_prev, d), :], kvf], axis=0)
        nk = kvf.shape[0]
        kv = kvf.astype(BF)
        kv_sw = pltpu.roll(kvf, HEAD_DIM, axis=1).astype(BF)
        zero = jnp.zeros_like(kv)
        k_low = jnp.where(low, kv, zero)
        k_high = jnp.where(low, zero, kv_sw)
        ones = jnp.ones((nk, LANES), BF)
        rhs_even = jnp.concatenate([ones, kv_sw], axis=1)
        rhs_odd = jnp.concatenate([ones, kv], axis=1)
        for pair in range(Q_PER_KV // 2):
            qq = q_refs[pair][0, rows_c, :].astype(BF)
            outs, lses = [], []
            for odd in range(2):
                qpk = 2 * pair + odd
                bias = bias_ref[g, qpk]
                if start_prev is None:
                    bias = bias[:, QB:]
                s = _dot_nt(qq, k_high if odd else k_low) + bias
                m = jnp.max(s, axis=-1, keepdims=True)
                p = jnp.exp(s - m).astype(BF)
                r = _dot(p, rhs_odd if odd else rhs_even)
                l = r[:, :LANES]
                outs.append(r[:, LANES:] / l)
                lses.append(m + jnp.log(l))
            o_scr[2 * g + pair, rows_c, :] = jnp.where(low, outs[0], outs[1])
            l_scr[2 * g + pair, rows_c, :] = jnp.where(low, lses[0], lses[1])

    for g, (q_refs, kv_ref) in enumerate((((q00, q01), kv0), ((q10, q11), kv1), ((q20, q21), kv2))):
        d = DILATIONS[g]
        sub_len = t_len // d
        nblk = sub_len // QB

        def first(rho, carry, g=g, q_refs=q_refs, kv_ref=kv_ref, d=d):
            block(g, q_refs, kv_ref, d, rho, None)
            return carry

        def rest(n, carry, g=g, q_refs=q_refs, kv_ref=kv_ref, d=d, nblk=nblk):
            rho = n // (nblk - 1)
            i = n % (nblk - 1) + 1
            cur = rho + d * QB * i
            block(g, q_refs, kv_ref, d, cur, cur - d * QB)
            return carry

        lax.fori_loop(0, d, first, 0)
        if nblk > 1:
            lax.fori_loop(0, d * (nblk - 1), rest, 0)

    def merge(i, carry):
        rows = pl.ds(pl.multiple_of(i * QB, QB), QB)
        for pair in range(Q_PER_KV // 2):
            o_ref[0, rows, pair * LANES:(pair + 1) * LANES] = _merge_groups(
                [o_scr[2 * g + pair, rows, :] for g in range(N_GROUPS)],
                [l_scr[2 * g + pair, rows, :] for g in range(N_GROUPS)])
        return carry

    lax.fori_loop(0, t_len // QB, merge, 0)


def _attn_prompt(q, kvp, bias):
    bsz, t, _ = q.shape
    assert t % (QB * DILATIONS[-1]) == 0
    wq = Q_PER_KV * HEAD_DIM
    n_pair = Q_PER_KV // 2
    q_specs = [pl.BlockSpec((1, t, LANES), lambda b, h, g=g, p=p: (b, 0, (g * KV_HEADS + h) * n_pair + p))
               for g in range(N_GROUPS) for p in range(n_pair)]
    kv_specs = [pl.BlockSpec((1, t, LANES), lambda b, h, g=g: (b, 0, g * KV_HEADS + h)) for g in range(N_GROUPS)]
    bias_spec = pl.BlockSpec((N_GROUPS, Q_PER_KV, QB, 2 * QB), lambda b, h: (0, h, 0, 0))
    return pl.pallas_call(
        _attn_prompt_body,
        grid=(bsz, KV_HEADS),
        in_specs=q_specs + kv_specs + [bias_spec],
        out_specs=pl.BlockSpec((1, t, wq), lambda b, h: (b, 0, h)),
        out_shape=jax.ShapeDtypeStruct((bsz, t, KV_HEADS * wq), F32),
        scratch_shapes=[pltpu.VMEM((N_GROUPS * n_pair, t, LANES), F32)] * 2,
        compiler_params=_params("parallel", "arbitrary"),
        name="attn_prompt",
    )(*([q] * (N_GROUPS * n_pair)), kvp, kvp, kvp, bias)


def _attn_sample_body(qp_ref, new_ref, c0, c1, c2, bc0, bc1, bc2, bn_ref, o_ref, w0, w1, w2):
    n_new = new_ref.shape[1]
    half = KV_HEADS * HEAD_DIM
    lane = lax.broadcasted_iota(jnp.int32, (1, half), 1)
    o_parts, lse_parts = [], []
    for g, (c_ref, bc_ref, w_ref) in enumerate(((c0, bc0, w0), (c1, bc1, w1), (c2, bc2, w2))):
        w = c_ref.shape[1]
        new = new_ref[0, :, g * 2 * half:(g + 1) * 2 * half]
        w_ref[0, 0:w - n_new, :] = c_ref[0, n_new:w, :]
        w_ref[0, w - n_new:w, :] = new
        newp = jnp.concatenate([new, jnp.zeros((LANES - n_new, 2 * half), F32)], axis=0).astype(BF)
        kc = c_ref[0, :, 0:half].astype(BF)
        vc = c_ref[0, :, half:].astype(BF)
        kn, vn = newp[:, :half], newp[:, half:]
        o_g = jnp.zeros((Q_PER_KV * n_new, half), F32)
        l_g = jnp.zeros((Q_PER_KV * n_new, half), F32)
        for h in range(KV_HEADS):
            qp = qp_ref[0, g, h]
            s_c = _dot_nt(qp, kc) + bc_ref[h]
            s_n = _dot_nt(qp, kn) + bn_ref[g, h]
            m = jnp.maximum(jnp.max(s_c, axis=-1, keepdims=True), jnp.max(s_n, axis=-1, keepdims=True))
            p_c = jnp.exp(s_c - m)
            p_n = jnp.exp(s_n - m)
            l = jnp.sum(p_c, axis=-1, keepdims=True) + jnp.sum(p_n, axis=-1, keepdims=True)
            o = (_dot(p_c.astype(BF), vc) + _dot(p_n.astype(BF), vn)) / l
            mine = (lane >= h * HEAD_DIM) & (lane < (h + 1) * HEAD_DIM)
            o_g = jnp.where(mine, o, o_g)
            l_g = jnp.where(mine, m + jnp.log(l), l_g)
        o_parts.append(o_g)
        lse_parts.append(l_g)
    o_ref[0] = _merge_groups(o_parts, lse_parts)


def _attn_sample(qpad, new_kv, caches, bias_c, bias_n):
    bsz = qpad.shape[0]
    n_new = new_kv.shape[1]
    rows = Q_PER_KV * n_new
    half = KV_HEADS * HEAD_DIM
    per_b = lambda shape: pl.BlockSpec((1,) + shape[1:], lambda b: (b,) + (0,) * (len(shape) - 1))
    cache_specs = [per_b(c.shape) for c in caches]
    return pl.pallas_call(
        _attn_sample_body,
        grid=(bsz,),
        in_specs=[per_b(qpad.shape), per_b(new_kv.shape)] + cache_specs
                 + [_const_spec(b.shape) for b in bias_c] + [_const_spec(bias_n.shape)],
        out_specs=[per_b((bsz, rows, half))] + cache_specs,
        out_shape=[jax.ShapeDtypeStruct((bsz, rows, half), F32)]
                  + [jax.ShapeDtypeStruct(c.shape, F32) for c in caches],
        compiler_params=_params("parallel"),
        name="attn_sample",
    )(qpad, new_kv, *caches, *bias_c, bias_n)


def _t5_buckets(dist):
    d = np.asarray(dist)
    large = MAX_EXACT + (np.log(np.maximum(d, 1) / MAX_EXACT) / np.log(MAX_DISTANCE / MAX_EXACT)
                         * (N_BUCKETS - MAX_EXACT)).astype(np.int64)
    large = np.minimum(large, N_BUCKETS - 1)
    return np.where(d < MAX_EXACT, d, large).astype(np.int32)


def _group_bias(rel_bias, g):
    bk = _t5_buckets(DILATIONS[g] * np.arange(N_KEYS))
    return rel_bias[bk][:, g * HEADS_PER_GROUP:(g + 1) * HEADS_PER_GROUP].T.astype(F32)


def _prompt_bias_table(rel_bias):
    qi = np.arange(QB)[:, None]
    kj = np.arange(2 * QB)[None, :]
    rel = qi + QB - kj
    valid = (rel >= 0) & (rel < N_KEYS)
    idx = np.clip(rel, 0, N_KEYS - 1)
    tabs = [jnp.where(valid[None], _group_bias(rel_bias, g)[:, idx], NEG) for g in range(N_GROUPS)]
    return jnp.stack(tabs, axis=0)


def _sample_bias_tables(rel_bias, n_new):
    r = np.tile(np.arange(n_new), Q_PER_KV)[:, None]
    tabs_c, tabs_n = [], []
    for g in range(N_GROUPS):
        w, d = WINDOWS[g], DILATIONS[g]
        kpos = np.arange(w + LANES)[None, :]
        rel = w + r - kpos
        valid = (rel >= 0) & (rel % d == 0) & (rel // d < N_KEYS) & (kpos < w + n_new)
        idx = np.clip(rel // d, 0, N_KEYS - 1)
        gb = _group_bias(rel_bias, g).reshape(KV_HEADS, Q_PER_KV, N_KEYS)
        rows_q = np.repeat(np.arange(Q_PER_KV), n_new)[:, None]
        tab = jnp.where(valid[None], gb[:, rows_q, idx], NEG)
        tabs_c.append(tab[:, :, :w])
        tabs_n.append(tab[:, :, w:])
    return tabs_c, jnp.stack(tabs_n, axis=0)


def _prep_weights(w_in_a, w_a2, b_a, w_o_a, w_kv, w_q_b, w_o_b, w_gate_up, w_down):
    n_main = 2 * GLA_DK + 2 * GLA_DV
    w_in = w_in_a[0]
    wa = jnp.pad(w_in[:, n_main:], ((0, 0), (0, LANES - GATE_RANK)))
    wa2 = jnp.pad(w_a2[0], ((0, LANES - GATE_RANK), (0, 0)))
    wkv_packed = w_kv.reshape(D_MODEL, N_GROUPS, 2, KV_HEADS, HEAD_DIM).transpose(0, 1, 3, 2, 4)
    return dict(
        wm=w_in[:, :n_main].astype(BF), wa=wa.astype(BF), wa2=wa2.astype(BF), ba=b_a[0][None, :],
        wo_a=w_o_a[0].astype(BF),
        wkv=w_kv.astype(BF), wkv_packed=wkv_packed.reshape(D_MODEL, -1).astype(BF),
        wq=(w_q_b[0] * HEAD_DIM ** -0.5).astype(BF),
        wo_b=w_o_b[0].astype(BF),
        wgu=[w_gate_up[l].astype(BF) for l in range(2)], wd=[w_down[l].astype(BF) for l in range(2)])


def _layer0(x, s0, norm_g, g_onorm, wts):
    bsz, t, _ = x.shape
    x2 = x.reshape(bsz * t, D_MODEL)
    q, k, v, r, g = _gla_in(x2, norm_g[0], wts["wm"], wts["wa"], wts["wa2"], wts["ba"])
    sh = lambda a: a.reshape(bsz, t, a.shape[-1])
    o, st = _gla(sh(q), sh(k), sh(v), sh(g), jnp.swapaxes(s0, -1, -2))
    h = _mix_ffn([o.reshape(bsz * t, GLA_DV), r], x2, g_onorm, wts["wo_a"], norm_g[0], wts["wgu"][0],
                 wts["wd"][0])
    return h, jnp.swapaxes(st, -1, -2)


def kernel(x_prompt, x_sample, state_gla, cache_win1, cache_win2, cache_win3, norm_g, w_in_a, w_a2, b_a,
           g_onorm, w_o_a, g_kv, w_kv, w_q_b, w_o_b, rel_bias, w_gate_up, w_down):
    wts = _prep_weights(w_in_a, w_a2, b_a, w_o_a, w_kv, w_q_b, w_o_b, w_gate_up, w_down)
    gkv = g_kv[None, :]

    bp, tp, _ = x_prompt.shape
    s0p = jnp.zeros((bp, GLA_HEADS, DKH, DVH), F32)
    h_p, gla_p = _layer0(x_prompt, s0p, norm_g, g_onorm, wts)
    kvp, q_p = _qkv(h_p, gkv, norm_g[1], wts["wkv_packed"], wts["wq"])
    o_p = _attn_prompt(q_p.reshape(bp, tp, -1), kvp.reshape(bp, tp, -1), _prompt_bias_table(rel_bias))
    y_p = _mix_ffn([o_p.reshape(bp * tp, D_MODEL)], h_p, None, wts["wo_b"], norm_g[1], wts["wgu"][1],
                   wts["wd"][1])
    kv5 = kvp.reshape(bp, tp, N_GROUPS, KV_HEADS, 2, HEAD_DIM)
    win_p = [kv5[:, tp - min(WINDOWS[g], tp):, g].transpose(0, 1, 3, 2, 4) for g in range(N_GROUPS)]

    bs, ts, _ = x_sample.shape
    h_s, gla_s = _layer0(x_sample, state_gla[0], norm_g, g_onorm, wts)
    kv_s, q_s = _qkv(h_s, gkv, norm_g[1], wts["wkv"], wts["wq"])
    q6 = q_s.reshape(bs, ts, N_GROUPS, KV_HEADS, Q_PER_KV, HEAD_DIM).transpose(0, 2, 3, 4, 1, 5)
    q6 = q6.reshape(bs, N_GROUPS, KV_HEADS, Q_PER_KV * ts, 1, HEAD_DIM)
    slot = jnp.eye(KV_HEADS, dtype=F32)[None, None, :, None, :, None]
    qpad = (q6 * slot).reshape(bs, N_GROUPS, KV_HEADS, Q_PER_KV * ts, KV_HEADS * HEAD_DIM).astype(BF)
    caches = [c.reshape(bs, c.shape[1], 2 * KV_HEADS * HEAD_DIM) for c in (cache_win1, cache_win2, cache_win3)]
    bias_c, bias_n = _sample_bias_tables(rel_bias, ts)
    o_s, *win_s = _attn_sample(qpad, kv_s.reshape(bs, ts, -1), caches, bias_c, bias_n)
    o_s = o_s.reshape(bs, Q_PER_KV, ts, KV_HEADS, HEAD_DIM).transpose(0, 2, 3, 1, 4).reshape(bs * ts, D_MODEL)
    y_s = _mix_ffn([o_s], h_s, None, wts["wo_b"], norm_g[1], wts["wgu"][1], wts["wd"][1])
    win_s = [w.reshape(bs, w.shape[1], 2, KV_HEADS, HEAD_DIM) for w in win_s]

    return (y_p.reshape(bp, tp, D_MODEL), y_s.reshape(bs, ts, D_MODEL), gla_p[None], win_p[0], win_p[1],
            win_p[2], gla_s[None], win_s[0], win_s[1], win_s[2])
```

```python
import functools

import numpy as np
import jax
import jax.numpy as jnp
from jax import lax
from jax.experimental import pallas as pl
from jax.experimental.pallas import tpu as pltpu

BF = jnp.bfloat16
F32 = jnp.float32

D_MODEL = 1024
GLA_HEADS = 4
GLA_DK = 512
GLA_DV = 1024
DKH = GLA_DK // GLA_HEADS
DVH = GLA_DV // GLA_HEADS
GATE_RANK = 16
GATE_TAU = 16.0
GLA_CHUNK = 64
GLA_SUB = 16
WINDOWS = (128, 512, 2048)
DILATIONS = (1, 4, 16)
N_GROUPS = 3
HEAD_DIM = 64
HEADS_PER_GROUP = 16
KV_HEADS = 4
Q_PER_KV = 4
N_PAIR = Q_PER_KV // 2
N_KEYS = 129
N_BUCKETS = 32
MAX_EXACT = 16
MAX_DISTANCE = 2048
D_FF = 2816
EPS = 1e-6
NEG = -1e30
QB = 128
LANES = 128
KV_WIDTH = 2 * KV_HEADS * HEAD_DIM
VMEM_LIMIT_BYTES = 56 * 1024 * 1024


def _dot(a, b):
    return jnp.dot(a, b, preferred_element_type=F32)


def _dot_nt(a, b):
    return lax.dot_general(a, b, (((1,), (1,)), ((), ())), preferred_element_type=F32)


def _rms(x, g):
    return x * lax.rsqrt(jnp.mean(x * x, axis=-1, keepdims=True) + EPS) * g


def _sigmoid(x):
    return 1.0 / (1.0 + jnp.exp(-x))


def _const_spec(shape):
    nd = len(shape)
    return pl.BlockSpec(shape, lambda *_: (0,) * nd, pipeline_mode=pl.Buffered(1))


def _params(*sem):
    return pltpu.CompilerParams(dimension_semantics=sem, vmem_limit_bytes=VMEM_LIMIT_BYTES)


def _row_tile(n, want):
    tm = min(n, want)
    assert n % tm == 0
    return tm


def _gla_in_body(x_ref, ng_ref, wm_ref, wa_ref, wa2_ref, ba_ref, q_ref, k_ref, v_ref, r_ref, g_ref):
    xn = _rms(x_ref[...], ng_ref[0:1, :]).astype(BF)
    q_ref[...] = _dot(xn, wm_ref[:, 0:GLA_DK]) * (DKH ** -0.5)
    k_ref[...] = _dot(xn, wm_ref[:, GLA_DK:2 * GLA_DK])
    v_ref[...] = _dot(xn, wm_ref[:, 2 * GLA_DK:2 * GLA_DK + GLA_DV])
    r_ref[...] = _dot(xn, wm_ref[:, 2 * GLA_DK + GLA_DV:])
    a = _dot(xn, wa_ref[...]).astype(BF)
    z = _dot(a, wa2_ref[...]) + ba_ref[...]
    g_ref[...] = (jnp.minimum(z, 0.0) - jnp.log(1.0 + jnp.exp(-jnp.abs(z)))) * (1.0 / GATE_TAU)


def _gla_in(x2, ng, wm, wa, wa2, ba):
    n = x2.shape[0]
    tm = _row_tile(n, 512)
    row = lambda w: pl.BlockSpec((tm, w), lambda i: (i, 0))
    return pl.pallas_call(
        _gla_in_body,
        grid=(n // tm,),
        in_specs=[row(D_MODEL), _const_spec(ng.shape), _const_spec(wm.shape), _const_spec(wa.shape),
                  _const_spec(wa2.shape), _const_spec(ba.shape)],
        out_specs=[row(GLA_DK), row(GLA_DK), row(GLA_DV), row(GLA_DV), row(GLA_DK)],
        out_shape=[jax.ShapeDtypeStruct((n, w), F32) for w in (GLA_DK, GLA_DK, GLA_DV, GLA_DV, GLA_DK)],
        compiler_params=_params("parallel"),
        name="gla_in",
    )(x2, ng, wm, wa, wa2, ba)


def _gla_body(c_real, q_ref, k_ref, v_ref, g_ref, s0_ref, o_ref, st_ref, s_scr):
    t = pl.program_id(1)
    c = GLA_CHUNK
    nsb = c // GLA_SUB
    n_chunks = q_ref.shape[1] // c_real

    @pl.when(t == 0)
    def _():
        s_scr[...] = s0_ref[0]

    ri = lax.broadcasted_iota(jnp.int32, (c, c), 0)
    ci = lax.broadcasted_iota(jnp.int32, (c, c), 1)
    tril_bf = jnp.where(ri >= ci, 1.0, 0.0).astype(BF)
    ri2 = lax.broadcasted_iota(jnp.int32, (c, LANES), 0)
    ci2 = lax.broadcasted_iota(jnp.int32, (c, LANES), 1)
    causal = ri2 >= ci2

    def pad_rows(a, rows):
        if a.shape[0] == rows:
            return a
        return jnp.concatenate([a, jnp.zeros((rows - a.shape[0], a.shape[1]), a.dtype)], axis=0)

    def chunk(idx, carry):
        rows = pl.ds(pl.multiple_of(idx * c_real, c_real), c_real)
        g_all = pad_rows(g_ref[0, rows, :], c)
        g_hi = g_all.astype(BF)
        g_lo = (g_all - g_hi.astype(F32)).astype(BF)
        b_all = _dot(tril_bf, g_hi) + _dot(tril_bf, g_lo)
        for h in range(GLA_HEADS):
            ks = slice(h * DKH, (h + 1) * DKH)
            vs = slice(h * DVH, (h + 1) * DVH)
            b = b_all[:, ks]
            qh = pad_rows(q_ref[0, rows, ks], c)
            kh = pad_rows(k_ref[0, rows, ks], c)
            vh = pad_rows(v_ref[0, rows, vs], LANES)
            b_last = b[c - 1:c, :]
            st = s_scr[h]
            o_inter = _dot_nt((qh * jnp.exp(b)).astype(BF), st.astype(BF))
            qparts, kparts = [], []
            for j in range(nsb):
                lo, hi = j * GLA_SUB, (j + 1) * GLA_SUB
                ref_row = b[lo:lo + 1, :]
                qparts.append((qh * jnp.exp(jnp.minimum(b - ref_row, 0.0))).astype(BF))
                kj = (kh[lo:hi] * jnp.exp(ref_row - b[lo:hi])).astype(BF)
                pieces = []
                if lo > 0:
                    pieces.append(jnp.zeros((lo, DKH), BF))
                pieces.append(kj)
                pieces.append(jnp.zeros((LANES - hi, DKH), BF))
                kparts.append(jnp.concatenate(pieces, axis=0))
            qcat = jnp.concatenate(qparts, axis=1)
            kcat = jnp.concatenate(kparts, axis=1)
            attn = jnp.where(causal, _dot_nt(qcat, kcat), 0.0).astype(BF)
            o = o_inter + _dot(attn, vh.astype(BF))
            o_ref[0, rows, vs] = o[0:c_real]
            k2 = pad_rows((kh * jnp.exp(b_last - b)).astype(BF), LANES)
            vt = vh.T.astype(BF)
            s_scr[h] = st * jnp.exp(b_last) + _dot(vt, k2)
        return carry

    lax.fori_loop(0, n_chunks, chunk, 0)

    @pl.when(t == pl.num_programs(1) - 1)
    def _():
        st_ref[0] = s_scr[...]


def _gla(q, k, v, g, s0t):
    bsz, t, _ = q.shape
    c_real = min(t, GLA_CHUNK)
    tc = min(t, 512)
    assert t % tc == 0 and tc % c_real == 0
    blk = lambda w: pl.BlockSpec((1, tc, w), lambda b, i: (b, i, 0))
    st_spec = pl.BlockSpec((1, GLA_HEADS, DVH, DKH), lambda b, i: (b, 0, 0, 0))
    return pl.pallas_call(
        functools.partial(_gla_body, c_real),
        grid=(bsz, t // tc),
        in_specs=[blk(GLA_DK), blk(GLA_DK), blk(GLA_DV), blk(GLA_DK), st_spec],
        out_specs=[blk(GLA_DV), st_spec],
        out_shape=[jax.ShapeDtypeStruct((bsz, t, GLA_DV), F32),
                   jax.ShapeDtypeStruct((bsz, GLA_HEADS, DVH, DKH), F32)],
        scratch_shapes=[pltpu.VMEM((GLA_HEADS, DVH, DKH), F32)],
        compiler_params=_params("parallel", "arbitrary"),
        name="gla",
    )(q, k, v, g, s0t)


def _mix_ffn_body(gated, *refs):
    if gated:
        o_ref, r_ref, x_ref, gon_ref, wo_ref, ng_ref, wgu_ref, wd_ref, out_ref = refs
        o = o_ref[...]
        r = r_ref[...]
        on = jnp.concatenate(
            [_rms(o[:, h * DVH:(h + 1) * DVH], gon_ref[...]) for h in range(GLA_HEADS)], axis=1)
        m = on * (r * _sigmoid(r))
    else:
        m_ref, x_ref, wo_ref, ng_ref, wgu_ref, wd_ref, out_ref = refs
        m = m_ref[...]
    y = _dot(m.astype(BF), wo_ref[...])
    h1 = x_ref[...] + _rms(y, ng_ref[1:2, :])
    u = _rms(h1, ng_ref[2:3, :]).astype(BF)
    gu = _dot(u, wgu_ref[...])
    gate = gu[:, :D_FF]
    act = (gate * _sigmoid(gate) * gu[:, D_FF:]).astype(BF)
    f = _dot(act, wd_ref[...])
    out_ref[...] = h1 + _rms(f, ng_ref[3:4, :])


def _mix_ffn(mix_inputs, x2, gon, wo, ng, wgu, wd):
    n = x2.shape[0]
    tm = _row_tile(n, 256)
    row = pl.BlockSpec((tm, D_MODEL), lambda i: (i, 0))
    gated = gon is not None
    args = list(mix_inputs) + [x2] + ([gon] if gated else []) + [wo, ng, wgu, wd]
    in_specs = [row] * (len(mix_inputs) + 1) + [_const_spec(a.shape) for a in args[len(mix_inputs) + 1:]]
    return pl.pallas_call(
        functools.partial(_mix_ffn_body, gated),
        grid=(n // tm,),
        in_specs=in_specs,
        out_specs=row,
        out_shape=jax.ShapeDtypeStruct((n, D_MODEL), F32),
        compiler_params=_params("parallel"),
        name="mix_ffn_gated" if gated else "mix_ffn",
    )(*args)


def _qkv_body(h_ref, gkv_ref, ng_ref, wkv_ref, wq_ref, kv0_ref, kv1_ref, kv2_ref, q_ref):
    h = h_ref[...]
    hn = h * lax.rsqrt(jnp.mean(h * h, axis=-1, keepdims=True) + EPS)
    hkv = (hn * gkv_ref[...]).astype(BF)
    for g, kv_ref in enumerate((kv0_ref, kv1_ref, kv2_ref)):
        kv_ref[...] = _dot(hkv, wkv_ref[:, g * KV_WIDTH:(g + 1) * KV_WIDTH])
    q_ref[...] = _dot((hn * ng_ref[0:1, :]).astype(BF), wq_ref[...])


def _qkv(h2, gkv, ng, wkv, wq):
    n = h2.shape[0]
    tm = _row_tile(n, 512)
    row = lambda w: pl.BlockSpec((tm, w), lambda i: (i, 0))
    nq = wq.shape[1]
    return pl.pallas_call(
        _qkv_body,
        grid=(n // tm,),
        in_specs=[row(D_MODEL), _const_spec(gkv.shape), _const_spec(ng.shape), _const_spec(wkv.shape),
                  _const_spec(wq.shape)],
        out_specs=[row(KV_WIDTH)] * N_GROUPS + [row(nq)],
        out_shape=[jax.ShapeDtypeStruct((n, KV_WIDTH), F32)] * N_GROUPS + [jax.ShapeDtypeStruct((n, nq), F32)],
        compiler_params=_params("parallel"),
        name="qkv",
    )(h2, gkv, ng, wkv, wq)


def _attn_prompt_body(q00, q01, q10, q11, q20, q21, k0, k1, k2, v0, v1, v2, bias_rest_ref, bias_first_ref,
                      o_ref, o_scr, l_scr, m_scr, klo_scr, khi_scr, vv_scr):
    t_len = k0.shape[1]
    par = pl.program_id(1) % 2
    lane = lax.broadcasted_iota(jnp.int32, (1, LANES), 1)
    low = lane < HEAD_DIM
    same = (lane >= HEAD_DIM).astype(jnp.int32) == par

    def strided_rows(start, d):
        if d > 1:
            return pl.ds(start, QB, stride=d)
        return pl.ds(start if isinstance(start, int) else pl.multiple_of(start, QB), QB)

    def block_rows(blk, n=1):
        return pl.ds(pl.multiple_of(blk * QB, QB), n * QB)

    def prepare(g, k_ref, v_ref, d, nblk, blk):
        rho = blk // nblk
        rows = strided_rows(rho + d * QB * (blk - rho * nblk), d)
        kp = k_ref[0, rows, :]
        vp = v_ref[0, rows, :]
        k_both = jnp.where(same, kp, pltpu.roll(kp, HEAD_DIM, axis=1))
        v_both = jnp.where(same, vp, pltpu.roll(vp, HEAD_DIM, axis=1))
        dst = block_rows(blk)
        klo_scr[dst, :] = jnp.where(low, k_both, 0.0).astype(BF)
        khi_scr[dst, :] = jnp.where(low, 0.0, k_both).astype(BF)
        vv_scr[dst, :] = v_both.astype(BF)

    def attend(g, q_refs, d, nblk, blk, with_prev):
        rho = blk // nblk
        rows_q = strided_rows(rho + d * QB * (blk - rho * nblk), d)
        keys = block_rows(blk - 1, 2) if with_prev else block_rows(blk)
        nk = 2 * QB if with_prev else QB
        q = jnp.concatenate([q_refs[p][0, rows_q, :] for p in range(N_PAIR)], axis=0).astype(BF)
        k_cat = jnp.concatenate([klo_scr[keys, :], khi_scr[keys, :]], axis=0)
        bias = bias_rest_ref[g] if with_prev else bias_first_ref[g]
        s = _dot_nt(q, k_cat) + bias
        rhs = jnp.concatenate([jnp.ones((nk, LANES), BF), vv_scr[keys, :]], axis=1)
        res, mx = [], []
        for odd in range(2):
            sh = s[:, odd * nk:(odd + 1) * nk]
            m = jnp.max(sh, axis=-1, keepdims=True)
            p = jnp.exp(sh - m).astype(BF)
            res.append(_dot(p, rhs))
            mx.append(m)
        l = jnp.where(low, res[0][:, :LANES], res[1][:, :LANES])
        o = jnp.where(low, res[0][:, LANES:], res[1][:, LANES:])
        m = jnp.where(low, mx[0], mx[1])
        for p in range(N_PAIR):
            slot = N_PAIR * g + p
            o_scr[slot, rows_q, :] = o[p * QB:(p + 1) * QB]
            l_scr[slot, rows_q, :] = l[p * QB:(p + 1) * QB]
            m_scr[slot, rows_q, :] = m[p * QB:(p + 1) * QB]

    groups = (((q00, q01), k0, v0), ((q10, q11), k1, v1), ((q20, q21), k2, v2))
    for g, (q_refs, k_ref, v_ref) in enumerate(groups):
        d = DILATIONS[g]
        nblk = t_len // d // QB

        def prep_step(blk, carry, g=g, k_ref=k_ref, v_ref=v_ref, d=d, nblk=nblk):
            prepare(g, k_ref, v_ref, d, nblk, blk)
            return carry

        def first_step(rho, carry, g=g, q_refs=q_refs, d=d, nblk=nblk):
            attend(g, q_refs, d, nblk, rho * nblk, False)
            return carry

        def rest_step(n, carry, g=g, q_refs=q_refs, d=d, nblk=nblk):
            rho = n // (nblk - 1)
            attend(g, q_refs, d, nblk, n + rho + 1, True)
            return carry

        lax.fori_loop(0, d * nblk, prep_step, 0, unroll=4)
        lax.fori_loop(0, d, first_step, 0, unroll=min(d, 4))
        if nblk > 1:
            n_rest = d * (nblk - 1)
            lax.fori_loop(0, n_rest, rest_step, 0, unroll=5 if n_rest % 5 == 0 else 4)

    def merge(i, carry):
        rows = block_rows(i)
        for p in range(N_PAIR):
            slots = [N_PAIR * g + p for g in range(N_GROUPS)]
            ms = [m_scr[s_, rows, :] for s_ in slots]
            top = jnp.maximum(jnp.maximum(ms[0], ms[1]), ms[2])
            es = [jnp.exp(m - top) for m in ms]
            num = sum(e * o_scr[s_, rows, :] for e, s_ in zip(es, slots))
            den = sum(e * l_scr[s_, rows, :] for e, s_ in zip(es, slots))
            o_ref[0, rows, p * LANES:(p + 1) * LANES] = num / den
        return carry

    lax.fori_loop(0, t_len // QB, merge, 0)


def _attn_prompt(q, kvs, bias_rest, bias_first):
    bsz, t, _ = q.shape
    assert t % (QB * DILATIONS[-1]) == 0
    wq = Q_PER_KV * HEAD_DIM
    heads_per_block = LANES // HEAD_DIM
    q_specs = [pl.BlockSpec((1, t, LANES), lambda b, h, g=g, p=p: (b, 0, (g * KV_HEADS + h) * N_PAIR + p))
               for g in range(N_GROUPS) for p in range(N_PAIR)]
    k_spec = pl.BlockSpec((1, t, LANES), lambda b, h: (b, 0, h // heads_per_block))
    v_spec = pl.BlockSpec((1, t, LANES), lambda b, h: (b, 0, KV_HEADS // heads_per_block + h // heads_per_block))
    bias_specs = [pl.BlockSpec((N_GROUPS, N_PAIR * QB, tab.shape[-1]), lambda b, h: (0, h, 0))
                  for tab in (bias_rest, bias_first)]
    slots = N_GROUPS * N_PAIR
    return pl.pallas_call(
        _attn_prompt_body,
        grid=(bsz, KV_HEADS),
        in_specs=q_specs + [k_spec] * N_GROUPS + [v_spec] * N_GROUPS + bias_specs,
        out_specs=pl.BlockSpec((1, t, wq), lambda b, h: (b, 0, h)),
        out_shape=jax.ShapeDtypeStruct((bsz, t, KV_HEADS * wq), F32),
        scratch_shapes=[pltpu.VMEM((slots, t, LANES), F32)] * 3 + [pltpu.VMEM((t, LANES), BF)] * 3,
        compiler_params=_params("parallel", "arbitrary"),
        name="attn_prompt",
    )(*([q] * slots), *kvs, *kvs, bias_rest, bias_first)


def _merge_groups(o_parts, lse_parts):
    mx = jnp.maximum(jnp.maximum(lse_parts[0], lse_parts[1]), lse_parts[2])
    es = [jnp.exp(l - mx) for l in lse_parts]
    num = es[0] * o_parts[0] + es[1] * o_parts[1] + es[2] * o_parts[2]
    return num / (es[0] + es[1] + es[2])


def _attn_sample_body(qp_ref, n0, n1, n2, c0, c1, c2, bc0, bc1, bc2, bn_ref, o_ref, w0, w1, w2):
    n_new = n0.shape[1]
    half = KV_HEADS * HEAD_DIM
    lane = lax.broadcasted_iota(jnp.int32, (1, half), 1)
    o_parts, lse_parts = [], []
    for g, (new_ref, c_ref, bc_ref, w_ref) in enumerate(((n0, c0, bc0, w0), (n1, c1, bc1, w1), (n2, c2, bc2, w2))):
        w = c_ref.shape[1]
        new = new_ref[0]
        w_ref[0, 0:w - n_new, :] = c_ref[0, n_new:w, :]
        w_ref[0, w - n_new:w, :] = new
        newp = jnp.concatenate([new, jnp.zeros((LANES - n_new, 2 * half), F32)], axis=0).astype(BF)
        kc = c_ref[0, :, 0:half].astype(BF)
        vc = c_ref[0, :, half:].astype(BF)
        kn, vn = newp[:, :half], newp[:, half:]
        o_g = jnp.zeros((Q_PER_KV * n_new, half), F32)
        l_g = jnp.zeros((Q_PER_KV * n_new, half), F32)
        for h in range(KV_HEADS):
            qp = qp_ref[0, g, h]
            s_c = _dot_nt(qp, kc) + bc_ref[h]
            s_n = _dot_nt(qp, kn) + bn_ref[g, h]
            m = jnp.maximum(jnp.max(s_c, axis=-1, keepdims=True), jnp.max(s_n, axis=-1, keepdims=True))
            p_c = jnp.exp(s_c - m)
            p_n = jnp.exp(s_n - m)
            l = jnp.sum(p_c, axis=-1, keepdims=True) + jnp.sum(p_n, axis=-1, keepdims=True)
            o = (_dot(p_c.astype(BF), vc) + _dot(p_n.astype(BF), vn)) / l
            mine = (lane >= h * HEAD_DIM) & (lane < (h + 1) * HEAD_DIM)
            o_g = jnp.where(mine, o, o_g)
            l_g = jnp.where(mine, m + jnp.log(l), l_g)
        o_parts.append(o_g)
        lse_parts.append(l_g)
    o_ref[0] = _merge_groups(o_parts, lse_parts)


def _attn_sample(qpad, new_kvs, caches, bias_c, bias_n):
    bsz = qpad.shape[0]
    n_new = new_kvs[0].shape[1]
    rows = Q_PER_KV * n_new
    half = KV_HEADS * HEAD_DIM
    per_b = lambda shape: pl.BlockSpec((1,) + shape[1:], lambda b: (b,) + (0,) * (len(shape) - 1))
    cache_specs = [per_b(c.shape) for c in caches]
    return pl.pallas_call(
        _attn_sample_body,
        grid=(bsz,),
        in_specs=[per_b(qpad.shape)] + [per_b(a.shape) for a in new_kvs] + cache_specs
                 + [_const_spec(b.shape) for b in bias_c] + [_const_spec(bias_n.shape)],
        out_specs=[per_b((bsz, rows, half))] + cache_specs,
        out_shape=[jax.ShapeDtypeStruct((bsz, rows, half), F32)]
                  + [jax.ShapeDtypeStruct(c.shape, F32) for c in caches],
        compiler_params=_params("parallel"),
        name="attn_sample",
    )(qpad, *new_kvs, *caches, *bias_c, bias_n)


def _t5_buckets(dist):
    d = np.asarray(dist)
    large = MAX_EXACT + (np.log(np.maximum(d, 1) / MAX_EXACT) / np.log(MAX_DISTANCE / MAX_EXACT)
                         * (N_BUCKETS - MAX_EXACT)).astype(np.int64)
    large = np.minimum(large, N_BUCKETS - 1)
    return np.where(d < MAX_EXACT, d, large).astype(np.int32)


def _group_bias(rel_bias, g):
    bk = _t5_buckets(DILATIONS[g] * np.arange(N_KEYS))
    return rel_bias[bk][:, g * HEADS_PER_GROUP:(g + 1) * HEADS_PER_GROUP].T.astype(F32)


def _prompt_bias_tables(rel_bias):
    period = 2 * QB
    rest, first = [], []
    for g in range(N_GROUPS):
        bv = _group_bias(rel_bias, g)
        row0 = jnp.concatenate([bv[:, ::-1], jnp.full((HEADS_PER_GROUP, period - N_KEYS), NEG, F32)], axis=1)
        wrap = jnp.concatenate([row0, row0, row0[:, :1]], axis=1)
        skew = jnp.broadcast_to(wrap[:, None, :], (HEADS_PER_GROUP, QB, 2 * period + 1))
        skew = skew.reshape(HEADS_PER_GROUP, -1)[:, :QB * 2 * period].reshape(HEADS_PER_GROUP, QB, 2 * period)
        tab = skew[:, :, period:]
        for out, part in ((rest, tab), (first, tab[:, :, QB:])):
            nk = part.shape[-1]
            pairs = part.reshape(HEADS_PER_GROUP // 2, 2, QB, nk).transpose(0, 2, 1, 3)
            out.append(pairs.reshape(HEADS_PER_GROUP // 2 * QB, 2 * nk))
    return jnp.stack(rest, axis=0), jnp.stack(first, axis=0)


def _sample_bias_tables(rel_bias, n_new):
    tabs_c, tabs_n = [], []
    for g in range(N_GROUPS):
        w, d = WINDOWS[g], DILATIONS[g]
        bv = _group_bias(rel_bias, g)
        fill = jnp.full(bv.shape, NEG, F32)
        dil = jnp.stack([bv] + [fill] * (d - 1), axis=-1).reshape(HEADS_PER_GROUP, N_KEYS * d)[:, :w + 1]
        ext = jnp.concatenate([jnp.full((HEADS_PER_GROUP, LANES - 1), NEG, F32), dil,
                               jnp.full((HEADS_PER_GROUP, n_new - 1), NEG, F32)], axis=1)
        rev = ext[:, ::-1]
        rows = [rev[:, n_new - 1 - r:n_new - 1 - r + w + LANES] for r in range(n_new)]
        tab = jnp.stack(rows, axis=1).reshape(KV_HEADS, Q_PER_KV * n_new, w + LANES)
        tabs_c.append(tab[:, :, :w])
        tabs_n.append(tab[:, :, w:])
    return tabs_c, jnp.stack(tabs_n, axis=0)


def _prep_weights(w_in_a, w_a2, b_a, w_o_a, w_kv, w_q_b, w_o_b, w_gate_up, w_down):
    n_main = 2 * GLA_DK + 2 * GLA_DV
    w_in = w_in_a[0]
    wa = jnp.pad(w_in[:, n_main:], ((0, 0), (0, LANES - GATE_RANK)))
    wa2 = jnp.pad(w_a2[0], ((0, LANES - GATE_RANK), (0, 0)))
    return dict(
        wm=w_in[:, :n_main].astype(BF), wa=wa.astype(BF), wa2=wa2.astype(BF), ba=b_a[0][None, :],
        wo_a=w_o_a[0].astype(BF), wkv=w_kv.astype(BF),
        wq=(w_q_b[0] * HEAD_DIM ** -0.5).astype(BF),
        wo_b=w_o_b[0].astype(BF),
        wgu=[w_gate_up[l].astype(BF) for l in range(2)], wd=[w_down[l].astype(BF) for l in range(2)])


def _layer0(x, s0, norm_g, g_onorm, wts):
    bsz, t, _ = x.shape
    x2 = x.reshape(bsz * t, D_MODEL)
    q, k, v, r, g = _gla_in(x2, norm_g[0], wts["wm"], wts["wa"], wts["wa2"], wts["ba"])
    sh = lambda a: a.reshape(bsz, t, a.shape[-1])
    o, st = _gla(sh(q), sh(k), sh(v), sh(g), jnp.swapaxes(s0, -1, -2))
    h = _mix_ffn([o.reshape(bsz * t, GLA_DV), r], x2, g_onorm, wts["wo_a"], norm_g[0], wts["wgu"][0],
                 wts["wd"][0])
    return h, jnp.swapaxes(st, -1, -2)


def kernel(x_prompt, x_sample, state_gla, cache_win1, cache_win2, cache_win3, norm_g, w_in_a, w_a2, b_a,
           g_onorm, w_o_a, g_kv, w_kv, w_q_b, w_o_b, rel_bias, w_gate_up, w_down):
    wts = _prep_weights(w_in_a, w_a2, b_a, w_o_a, w_kv, w_q_b, w_o_b, w_gate_up, w_down)
    gkv = g_kv[None, :]

    bp, tp, _ = x_prompt.shape
    s0p = jnp.zeros((bp, GLA_HEADS, DKH, DVH), F32)
    h_p, gla_p = _layer0(x_prompt, s0p, norm_g, g_onorm, wts)
    *kv_p, q_p = _qkv(h_p, gkv, norm_g[1], wts["wkv"], wts["wq"])
    kv_p = [a.reshape(bp, tp, KV_WIDTH) for a in kv_p]
    o_p = _attn_prompt(q_p.reshape(bp, tp, -1), kv_p, *_prompt_bias_tables(rel_bias))
    y_p = _mix_ffn([o_p.reshape(bp * tp, D_MODEL)], h_p, None, wts["wo_b"], norm_g[1], wts["wgu"][1],
                   wts["wd"][1])
    win_p = [kv_p[g][:, tp - min(WINDOWS[g], tp):].reshape(bp, -1, 2, KV_HEADS, HEAD_DIM) for g in range(N_GROUPS)]

    bs, ts, _ = x_sample.shape
    h_s, gla_s = _layer0(x_sample, state_gla[0], norm_g, g_onorm, wts)
    *kv_s, q_s = _qkv(h_s, gkv, norm_g[1], wts["wkv"], wts["wq"])
    q6 = q_s.reshape(bs, ts, N_GROUPS, KV_HEADS, Q_PER_KV, HEAD_DIM).transpose(0, 2, 3, 4, 1, 5)
    q6 = q6.reshape(bs, N_GROUPS, KV_HEADS, Q_PER_KV * ts, 1, HEAD_DIM)
    slot = jnp.eye(KV_HEADS, dtype=F32)[None, None, :, None, :, None]
    qpad = (q6 * slot).reshape(bs, N_GROUPS, KV_HEADS, Q_PER_KV * ts, KV_HEADS * HEAD_DIM).astype(BF)
    caches = [c.reshape(bs, c.shape[1], KV_WIDTH) for c in (cache_win1, cache_win2, cache_win3)]
    bias_c, bias_n = _sample_bias_tables(rel_bias, ts)
    o_s, *win_s = _attn_sample(qpad, [a.reshape(bs, ts, KV_WIDTH) for a in kv_s], caches, bias_c, bias_n)
    o_s = o_s.reshape(bs, Q_PER_KV, ts, KV_HEADS, HEAD_DIM).transpose(0, 2, 3, 1, 4).reshape(bs * ts, D_MODEL)
    y_s = _mix_ffn([o_s], h_s, None, wts["wo_b"], norm_g[1], wts["wgu"][1], wts["wd"][1])
    win_s = [w.reshape(bs, w.shape[1], 2, KV_HEADS, HEAD_DIM) for w in win_s]

    return (y_p.reshape(bp, tp, D_MODEL), y_s.reshape(bs, ts, D_MODEL), gla_p[None], win_p[0], win_p[1],
            win_p[2], gla_s[None], win_s[0], win_s[1], win_s[2])
```

```python
import functools

import numpy as np
import jax
import jax.numpy as jnp
from jax import lax
from jax.experimental import pallas as pl
from jax.experimental.pallas import tpu as pltpu

BF = jnp.bfloat16
F32 = jnp.float32

D_MODEL = 1024
GLA_HEADS = 4
GLA_DK = 512
GLA_DV = 1024
DKH = GLA_DK // GLA_HEADS
DVH = GLA_DV // GLA_HEADS
GATE_RANK = 16
GATE_TAU = 16.0
GLA_CHUNK = 64
GLA_SUB = 16
WINDOWS = (128, 512, 2048)
DILATIONS = (1, 4, 16)
N_GROUPS = 3
HEAD_DIM = 64
HEADS_PER_GROUP = 16
KV_HEADS = 4
Q_PER_KV = 4
N_PAIR = Q_PER_KV // 2
N_KEYS = 129
N_BUCKETS = 32
MAX_EXACT = 16
MAX_DISTANCE = 2048
D_FF = 2816
EPS = 1e-6
NEG = -1e30
QB = 128
LANES = 128
KV_WIDTH = 2 * KV_HEADS * HEAD_DIM
VMEM_LIMIT_BYTES = 56 * 1024 * 1024


def _dot(a, b):
    return jnp.dot(a, b, preferred_element_type=F32)


def _dot_nt(a, b):
    return lax.dot_general(a, b, (((1,), (1,)), ((), ())), preferred_element_type=F32)


def _rms(x, g):
    return x * lax.rsqrt(jnp.mean(x * x, axis=-1, keepdims=True) + EPS) * g


def _sigmoid(x):
    return 1.0 / (1.0 + jnp.exp(-x))


def _const_spec(shape):
    nd = len(shape)
    return pl.BlockSpec(shape, lambda *_: (0,) * nd, pipeline_mode=pl.Buffered(1))


def _params(*sem):
    return pltpu.CompilerParams(dimension_semantics=sem, vmem_limit_bytes=VMEM_LIMIT_BYTES)


def _row_tile(n, want):
    tm = min(n, want)
    assert n % tm == 0
    return tm


def _gla_in_body(x_ref, ng_ref, wm_ref, wa_ref, wa2_ref, ba_ref, q_ref, k_ref, v_ref, r_ref, g_ref):
    xn = _rms(x_ref[...], ng_ref[0:1, :]).astype(BF)
    q_ref[...] = _dot(xn, wm_ref[:, 0:GLA_DK]) * (DKH ** -0.5)
    k_ref[...] = _dot(xn, wm_ref[:, GLA_DK:2 * GLA_DK])
    v_ref[...] = _dot(xn, wm_ref[:, 2 * GLA_DK:2 * GLA_DK + GLA_DV])
    r_ref[...] = _dot(xn, wm_ref[:, 2 * GLA_DK + GLA_DV:])
    a = _dot(xn, wa_ref[...]).astype(BF)
    z = _dot(a, wa2_ref[...]) + ba_ref[...]
    g_ref[...] = (jnp.minimum(z, 0.0) - jnp.log(1.0 + jnp.exp(-jnp.abs(z)))) * (1.0 / GATE_TAU)


def _gla_in(x2, ng, wm, wa, wa2, ba):
    n = x2.shape[0]
    tm = _row_tile(n, 512)
    row = lambda w: pl.BlockSpec((tm, w), lambda i: (i, 0))
    return pl.pallas_call(
        _gla_in_body,
        grid=(n // tm,),
        in_specs=[row(D_MODEL), _const_spec(ng.shape), _const_spec(wm.shape), _const_spec(wa.shape),
                  _const_spec(wa2.shape), _const_spec(ba.shape)],
        out_specs=[row(GLA_DK), row(GLA_DK), row(GLA_DV), row(GLA_DV), row(GLA_DK)],
        out_shape=[jax.ShapeDtypeStruct((n, w), F32) for w in (GLA_DK, GLA_DK, GLA_DV, GLA_DV, GLA_DK)],
        compiler_params=_params("parallel"),
        name="gla_in",
    )(x2, ng, wm, wa, wa2, ba)


def _gla_body(c_real, q_ref, k_ref, v_ref, g_ref, s0_ref, o_ref, st_ref, s_scr):
    t = pl.program_id(1)
    c = GLA_CHUNK
    nsb = c // GLA_SUB
    n_chunks = q_ref.shape[1] // c_real

    @pl.when(t == 0)
    def _():
        s_scr[...] = s0_ref[0]

    ri = lax.broadcasted_iota(jnp.int32, (c, c), 0)
    ci = lax.broadcasted_iota(jnp.int32, (c, c), 1)
    tril_bf = jnp.where(ri >= ci, 1.0, 0.0).astype(BF)
    ri2 = lax.broadcasted_iota(jnp.int32, (c, LANES), 0)
    ci2 = lax.broadcasted_iota(jnp.int32, (c, LANES), 1)
    causal = ri2 >= ci2

    def pad_rows(a, rows):
        if a.shape[0] == rows:
            return a
        return jnp.concatenate([a, jnp.zeros((rows - a.shape[0], a.shape[1]), a.dtype)], axis=0)

    def chunk(idx, carry):
        rows = pl.ds(pl.multiple_of(idx * c_real, c_real), c_real)
        g_all = pad_rows(g_ref[0, rows, :], c)
        g_hi = g_all.astype(BF)
        g_lo = (g_all - g_hi.astype(F32)).astype(BF)
        b_all = _dot(tril_bf, g_hi) + _dot(tril_bf, g_lo)
        for h in range(GLA_HEADS):
            ks = slice(h * DKH, (h + 1) * DKH)
            vs = slice(h * DVH, (h + 1) * DVH)
            b = b_all[:, ks]
            qh = pad_rows(q_ref[0, rows, ks], c)
            kh = pad_rows(k_ref[0, rows, ks], c)
            vh = pad_rows(v_ref[0, rows, vs], LANES)
            b_last = b[c - 1:c, :]
            st = s_scr[h]
            o_inter = _dot_nt((qh * jnp.exp(b)).astype(BF), st.astype(BF))
            qparts, kparts = [], []
            for j in range(nsb):
                lo, hi = j * GLA_SUB, (j + 1) * GLA_SUB
                ref_row = b[lo:lo + 1, :]
                qparts.append((qh * jnp.exp(jnp.minimum(b - ref_row, 0.0))).astype(BF))
                kj = (kh[lo:hi] * jnp.exp(ref_row - b[lo:hi])).astype(BF)
                pieces = []
                if lo > 0:
                    pieces.append(jnp.zeros((lo, DKH), BF))
                pieces.append(kj)
                pieces.append(jnp.zeros((LANES - hi, DKH), BF))
                kparts.append(jnp.concatenate(pieces, axis=0))
            qcat = jnp.concatenate(qparts, axis=1)
            kcat = jnp.concatenate(kparts, axis=1)
            attn = jnp.where(causal, _dot_nt(qcat, kcat), 0.0).astype(BF)
            o = o_inter + _dot(attn, vh.astype(BF))
            o_ref[0, rows, vs] = o[0:c_real]
            k2 = pad_rows((kh * jnp.exp(b_last - b)).astype(BF), LANES)
            vt = vh.T.astype(BF)
            s_scr[h] = st * jnp.exp(b_last) + _dot(vt, k2)
        return carry

    lax.fori_loop(0, n_chunks, chunk, 0)

    @pl.when(t == pl.num_programs(1) - 1)
    def _():
        st_ref[0] = s_scr[...]


def _gla(q, k, v, g, s0t):
    bsz, t, _ = q.shape
    c_real = min(t, GLA_CHUNK)
    tc = min(t, 512)
    assert t % tc == 0 and tc % c_real == 0
    blk = lambda w: pl.BlockSpec((1, tc, w), lambda b, i: (b, i, 0))
    st_spec = pl.BlockSpec((1, GLA_HEADS, DVH, DKH), lambda b, i: (b, 0, 0, 0))
    return pl.pallas_call(
        functools.partial(_gla_body, c_real),
        grid=(bsz, t // tc),
        in_specs=[blk(GLA_DK), blk(GLA_DK), blk(GLA_DV), blk(GLA_DK), st_spec],
        out_specs=[blk(GLA_DV), st_spec],
        out_shape=[jax.ShapeDtypeStruct((bsz, t, GLA_DV), F32),
                   jax.ShapeDtypeStruct((bsz, GLA_HEADS, DVH, DKH), F32)],
        scratch_shapes=[pltpu.VMEM((GLA_HEADS, DVH, DKH), F32)],
        compiler_params=_params("parallel", "arbitrary"),
        name="gla",
    )(q, k, v, g, s0t)


def _mix_ffn_body(gated, *refs):
    if gated:
        o_ref, r_ref, x_ref, gon_ref, wo_ref, ng_ref, wgu_ref, wd_ref, out_ref = refs
        o = o_ref[...]
        r = r_ref[...]
        on = jnp.concatenate(
            [_rms(o[:, h * DVH:(h + 1) * DVH], gon_ref[...]) for h in range(GLA_HEADS)], axis=1)
        m = on * (r * _sigmoid(r))
    else:
        m_ref, x_ref, wo_ref, ng_ref, wgu_ref, wd_ref, out_ref = refs
        m = m_ref[...]
    y = _dot(m.astype(BF), wo_ref[...])
    h1 = x_ref[...] + _rms(y, ng_ref[1:2, :])
    u = _rms(h1, ng_ref[2:3, :]).astype(BF)
    gu = _dot(u, wgu_ref[...])
    gate = gu[:, :D_FF]
    act = (gate * _sigmoid(gate) * gu[:, D_FF:]).astype(BF)
    f = _dot(act, wd_ref[...])
    out_ref[...] = h1 + _rms(f, ng_ref[3:4, :])


def _mix_ffn(mix_inputs, x2, gon, wo, ng, wgu, wd):
    n = x2.shape[0]
    tm = _row_tile(n, 256)
    row = pl.BlockSpec((tm, D_MODEL), lambda i: (i, 0))
    gated = gon is not None
    args = list(mix_inputs) + [x2] + ([gon] if gated else []) + [wo, ng, wgu, wd]
    in_specs = [row] * (len(mix_inputs) + 1) + [_const_spec(a.shape) for a in args[len(mix_inputs) + 1:]]
    return pl.pallas_call(
        functools.partial(_mix_ffn_body, gated),
        grid=(n // tm,),
        in_specs=in_specs,
        out_specs=row,
        out_shape=jax.ShapeDtypeStruct((n, D_MODEL), F32),
        compiler_params=_params("parallel"),
        name="mix_ffn_gated" if gated else "mix_ffn",
    )(*args)


def _qkv_body(h_ref, gkv_ref, ng_ref, wkv_ref, wq_ref, kv0_ref, kv1_ref, kv2_ref, q_ref):
    h = h_ref[...]
    hn = h * lax.rsqrt(jnp.mean(h * h, axis=-1, keepdims=True) + EPS)
    hkv = (hn * gkv_ref[...]).astype(BF)
    for g, kv_ref in enumerate((kv0_ref, kv1_ref, kv2_ref)):
        kv_ref[...] = _dot(hkv, wkv_ref[:, g * KV_WIDTH:(g + 1) * KV_WIDTH])
    q_ref[...] = _dot((hn * ng_ref[0:1, :]).astype(BF), wq_ref[...])


def _qkv(h2, gkv, ng, wkv, wq):
    n = h2.shape[0]
    tm = _row_tile(n, 512)
    row = lambda w: pl.BlockSpec((tm, w), lambda i: (i, 0))
    nq = wq.shape[1]
    return pl.pallas_call(
        _qkv_body,
        grid=(n // tm,),
        in_specs=[row(D_MODEL), _const_spec(gkv.shape), _const_spec(ng.shape), _const_spec(wkv.shape),
                  _const_spec(wq.shape)],
        out_specs=[row(KV_WIDTH)] * N_GROUPS + [row(nq)],
        out_shape=[jax.ShapeDtypeStruct((n, KV_WIDTH), F32)] * N_GROUPS + [jax.ShapeDtypeStruct((n, nq), F32)],
        compiler_params=_params("parallel"),
        name="qkv",
    )(h2, gkv, ng, wkv, wq)


def _attn_prompt_body(q00, q01, q10, q11, q20, q21, k0, k1, k2, v0, v1, v2, bias_rest_ref, bias_first_ref,
                      o_ref, o_scr, l_scr, m_scr, klo_scr, khi_scr, vlo_scr, vhi_scr):
    t_len = k0.shape[1]
    par = pl.program_id(1) % 2
    lane = lax.broadcasted_iota(jnp.int32, (1, LANES), 1)
    low = lane < HEAD_DIM
    same = (lane >= HEAD_DIM).astype(jnp.int32) == par

    def strided_rows(start, d):
        if d > 1:
            return pl.ds(start, QB, stride=d)
        return pl.ds(start if isinstance(start, int) else pl.multiple_of(start, QB), QB)

    def block_rows(blk, n=1):
        return pl.ds(pl.multiple_of(blk * QB, QB), n * QB)

    def prepare(g, k_ref, v_ref, d, nblk, blk):
        rho = blk // nblk
        rows = strided_rows(rho + d * QB * (blk - rho * nblk), d)
        kp = k_ref[0, rows, :]
        vp = v_ref[0, rows, :]
        k_both = jnp.where(same, kp, pltpu.roll(kp, HEAD_DIM, axis=1))
        v_both = jnp.where(same, vp, pltpu.roll(vp, HEAD_DIM, axis=1))
        dst = block_rows(blk)
        klo_scr[dst, :] = jnp.where(low, k_both, 0.0).astype(BF)
        khi_scr[dst, :] = jnp.where(low, 0.0, k_both).astype(BF)
        vlo_scr[dst, :] = jnp.where(low, v_both, 0.0).astype(BF)
        vhi_scr[dst, :] = jnp.where(low, 0.0, v_both).astype(BF)

    def attend(g, q_refs, d, nblk, blk, with_prev):
        rho = blk // nblk
        rows_q = strided_rows(rho + d * QB * (blk - rho * nblk), d)
        keys = block_rows(blk - 1, 2) if with_prev else block_rows(blk)
        nk = 2 * QB if with_prev else QB
        q = jnp.concatenate([q_refs[p][0, rows_q, :] for p in range(N_PAIR)], axis=0).astype(BF)
        k_cat = jnp.concatenate([klo_scr[keys, :], khi_scr[keys, :]], axis=0)
        bias = bias_rest_ref[g] if with_prev else bias_first_ref[g]
        s = _dot_nt(q, k_cat) + bias
        mx, ps = [], []
        for odd in range(2):
            sh = s[:, odd * nk:(odd + 1) * nk]
            mx.append(jnp.max(sh, axis=-1, keepdims=True))
            ps.append(jnp.exp(sh - mx[odd]).astype(BF))
        ones_lo = jnp.broadcast_to(jnp.where(low, 1.0, 0.0), (nk, LANES)).astype(BF)
        ones_hi = jnp.broadcast_to(jnp.where(low, 0.0, 1.0), (nk, LANES)).astype(BF)
        rhs = jnp.concatenate([jnp.concatenate([vlo_scr[keys, :], ones_lo], axis=1),
                               jnp.concatenate([vhi_scr[keys, :], ones_hi], axis=1)], axis=0)
        res = _dot(jnp.concatenate(ps, axis=1), rhs)
        m = jnp.where(low, mx[0], mx[1])
        for p in range(N_PAIR):
            slot = N_PAIR * g + p
            o_scr[slot, rows_q, :] = res[p * QB:(p + 1) * QB, :LANES]
            l_scr[slot, rows_q, :] = res[p * QB:(p + 1) * QB, LANES:]
            m_scr[slot, rows_q, :] = m[p * QB:(p + 1) * QB]

    groups = (((q00, q01), k0, v0), ((q10, q11), k1, v1), ((q20, q21), k2, v2))
    for g, (q_refs, k_ref, v_ref) in enumerate(groups):
        d = DILATIONS[g]
        nblk = t_len // d // QB

        def prep_step(blk, carry, g=g, k_ref=k_ref, v_ref=v_ref, d=d, nblk=nblk):
            prepare(g, k_ref, v_ref, d, nblk, blk)
            return carry

        def first_step(rho, carry, g=g, q_refs=q_refs, d=d, nblk=nblk):
            attend(g, q_refs, d, nblk, rho * nblk, False)
            return carry

        def rest_step(n, carry, g=g, q_refs=q_refs, d=d, nblk=nblk):
            rho = n // (nblk - 1)
            attend(g, q_refs, d, nblk, n + rho + 1, True)
            return carry

        lax.fori_loop(0, d * nblk, prep_step, 0, unroll=4)
        lax.fori_loop(0, d, first_step, 0, unroll=min(d, 4))
        if nblk > 1:
            n_rest = d * (nblk - 1)
            lax.fori_loop(0, n_rest, rest_step, 0, unroll=5 if n_rest % 5 == 0 else 4)

    def merge(i, carry):
        rows = block_rows(i)
        for p in range(N_PAIR):
            slots = [N_PAIR * g + p for g in range(N_GROUPS)]
            ms = [m_scr[s_, rows, :] for s_ in slots]
            top = jnp.maximum(jnp.maximum(ms[0], ms[1]), ms[2])
            es = [jnp.exp(m - top) for m in ms]
            num = sum(e * o_scr[s_, rows, :] for e, s_ in zip(es, slots))
            den = sum(e * l_scr[s_, rows, :] for e, s_ in zip(es, slots))
            o_ref[0, rows, p * LANES:(p + 1) * LANES] = num / den
        return carry

    lax.fori_loop(0, t_len // QB, merge, 0)


def _attn_prompt(q, kvs, bias_rest, bias_first):
    bsz, t, _ = q.shape
    assert t % (QB * DILATIONS[-1]) == 0
    wq = Q_PER_KV * HEAD_DIM
    heads_per_block = LANES // HEAD_DIM
    q_specs = [pl.BlockSpec((1, t, LANES), lambda b, h, g=g, p=p: (b, 0, (g * KV_HEADS + h) * N_PAIR + p))
               for g in range(N_GROUPS) for p in range(N_PAIR)]
    k_spec = pl.BlockSpec((1, t, LANES), lambda b, h: (b, 0, h // heads_per_block))
    v_spec = pl.BlockSpec((1, t, LANES), lambda b, h: (b, 0, KV_HEADS // heads_per_block + h // heads_per_block))
    bias_specs = [pl.BlockSpec((N_GROUPS, N_PAIR * QB, tab.shape[-1]), lambda b, h: (0, h, 0))
                  for tab in (bias_rest, bias_first)]
    slots = N_GROUPS * N_PAIR
    return pl.pallas_call(
        _attn_prompt_body,
        grid=(bsz, KV_HEADS),
        in_specs=q_specs + [k_spec] * N_GROUPS + [v_spec] * N_GROUPS + bias_specs,
        out_specs=pl.BlockSpec((1, t, wq), lambda b, h: (b, 0, h)),
        out_shape=jax.ShapeDtypeStruct((bsz, t, KV_HEADS * wq), F32),
        scratch_shapes=[pltpu.VMEM((slots, t, LANES), F32)] * 3 + [pltpu.VMEM((t, LANES), BF)] * 4,
        compiler_params=_params("parallel", "arbitrary"),
        name="attn_prompt",
    )(*([q] * slots), *kvs, *kvs, bias_rest, bias_first)


def _merge_groups(o_parts, lse_parts):
    mx = jnp.maximum(jnp.maximum(lse_parts[0], lse_parts[1]), lse_parts[2])
    es = [jnp.exp(l - mx) for l in lse_parts]
    num = es[0] * o_parts[0] + es[1] * o_parts[1] + es[2] * o_parts[2]
    return num / (es[0] + es[1] + es[2])


def _attn_sample_body(qp_ref, n0, n1, n2, c0, c1, c2, bc0, bc1, bc2, bn_ref, o_ref, w0, w1, w2):
    n_new = n0.shape[1]
    half = KV_HEADS * HEAD_DIM
    lane = lax.broadcasted_iota(jnp.int32, (1, half), 1)
    o_parts, lse_parts = [], []
    for g, (new_ref, c_ref, bc_ref, w_ref) in enumerate(((n0, c0, bc0, w0), (n1, c1, bc1, w1), (n2, c2, bc2, w2))):
        w = c_ref.shape[1]
        new = new_ref[0]
        w_ref[0, 0:w - n_new, :] = c_ref[0, n_new:w, :]
        w_ref[0, w - n_new:w, :] = new
        newp = jnp.concatenate([new, jnp.zeros((LANES - n_new, 2 * half), F32)], axis=0).astype(BF)
        kc = c_ref[0, :, 0:half].astype(BF)
        vc = c_ref[0, :, half:].astype(BF)
        kn, vn = newp[:, :half], newp[:, half:]
        o_g = jnp.zeros((Q_PER_KV * n_new, half), F32)
        l_g = jnp.zeros((Q_PER_KV * n_new, half), F32)
        for h in range(KV_HEADS):
            qp = qp_ref[0, g, h]
            s_c = _dot_nt(qp, kc) + bc_ref[h]
            s_n = _dot_nt(qp, kn) + bn_ref[g, h]
            m = jnp.maximum(jnp.max(s_c, axis=-1, keepdims=True), jnp.max(s_n, axis=-1, keepdims=True))
            p_c = jnp.exp(s_c - m)
            p_n = jnp.exp(s_n - m)
            l = jnp.sum(p_c, axis=-1, keepdims=True) + jnp.sum(p_n, axis=-1, keepdims=True)
            o = (_dot(p_c.astype(BF), vc) + _dot(p_n.astype(BF), vn)) / l
            mine = (lane >= h * HEAD_DIM) & (lane < (h + 1) * HEAD_DIM)
            o_g = jnp.where(mine, o, o_g)
            l_g = jnp.where(mine, m + jnp.log(l), l_g)
        o_parts.append(o_g)
        lse_parts.append(l_g)
    o_ref[0] = _merge_groups(o_parts, lse_parts)


def _attn_sample(qpad, new_kvs, caches, bias_c, bias_n):
    bsz = qpad.shape[0]
    n_new = new_kvs[0].shape[1]
    rows = Q_PER_KV * n_new
    half = KV_HEADS * HEAD_DIM
    per_b = lambda shape: pl.BlockSpec((1,) + shape[1:], lambda b: (b,) + (0,) * (len(shape) - 1))
    cache_specs = [per_b(c.shape) for c in caches]
    return pl.pallas_call(
        _attn_sample_body,
        grid=(bsz,),
        in_specs=[per_b(qpad.shape)] + [per_b(a.shape) for a in new_kvs] + cache_specs
                 + [_const_spec(b.shape) for b in bias_c] + [_const_spec(bias_n.shape)],
        out_specs=[per_b((bsz, rows, half))] + cache_specs,
        out_shape=[jax.ShapeDtypeStruct((bsz, rows, half), F32)]
                  + [jax.ShapeDtypeStruct(c.shape, F32) for c in caches],
        compiler_params=_params("parallel"),
        name="attn_sample",
    )(qpad, *new_kvs, *caches, *bias_c, bias_n)


def _t5_buckets(dist):
    d = np.asarray(dist)
    large = MAX_EXACT + (np.log(np.maximum(d, 1) / MAX_EXACT) / np.log(MAX_DISTANCE / MAX_EXACT)
                         * (N_BUCKETS - MAX_EXACT)).astype(np.int64)
    large = np.minimum(large, N_BUCKETS - 1)
    return np.where(d < MAX_EXACT, d, large).astype(np.int32)


def _group_bias(rel_bias, g):
    bk = _t5_buckets(DILATIONS[g] * np.arange(N_KEYS))
    return rel_bias[bk][:, g * HEADS_PER_GROUP:(g + 1) * HEADS_PER_GROUP].T.astype(F32)


def _prompt_bias_tables(rel_bias):
    period = 2 * QB
    rest, first = [], []
    for g in range(N_GROUPS):
        bv = _group_bias(rel_bias, g)
        row0 = jnp.concatenate([bv[:, ::-1], jnp.full((HEADS_PER_GROUP, period - N_KEYS), NEG, F32)], axis=1)
        wrap = jnp.concatenate([row0, row0, row0[:, :1]], axis=1)
        skew = jnp.broadcast_to(wrap[:, None, :], (HEADS_PER_GROUP, QB, 2 * period + 1))
        skew = skew.reshape(HEADS_PER_GROUP, -1)[:, :QB * 2 * period].reshape(HEADS_PER_GROUP, QB, 2 * period)
        tab = skew[:, :, period:]
        for out, part in ((rest, tab), (first, tab[:, :, QB:])):
            nk = part.shape[-1]
            pairs = part.reshape(HEADS_PER_GROUP // 2, 2, QB, nk).transpose(0, 2, 1, 3)
            out.append(pairs.reshape(HEADS_PER_GROUP // 2 * QB, 2 * nk))
    return jnp.stack(rest, axis=0), jnp.stack(first, axis=0)


def _sample_bias_tables(rel_bias, n_new):
    tabs_c, tabs_n = [], []
    for g in range(N_GROUPS):
        w, d = WINDOWS[g], DILATIONS[g]
        bv = _group_bias(rel_bias, g)
        fill = jnp.full(bv.shape, NEG, F32)
        dil = jnp.stack([bv] + [fill] * (d - 1), axis=-1).reshape(HEADS_PER_GROUP, N_KEYS * d)[:, :w + 1]
        ext = jnp.concatenate([jnp.full((HEADS_PER_GROUP, LANES - 1), NEG, F32), dil,
                               jnp.full((HEADS_PER_GROUP, n_new - 1), NEG, F32)], axis=1)
        rev = ext[:, ::-1]
        rows = [rev[:, n_new - 1 - r:n_new - 1 - r + w + LANES] for r in range(n_new)]
        tab = jnp.stack(rows, axis=1).reshape(KV_HEADS, Q_PER_KV * n_new, w + LANES)
        tabs_c.append(tab[:, :, :w])
        tabs_n.append(tab[:, :, w:])
    return tabs_c, jnp.stack(tabs_n, axis=0)


def _prep_weights(w_in_a, w_a2, b_a, w_o_a, w_kv, w_q_b, w_o_b, w_gate_up, w_down):
    n_main = 2 * GLA_DK + 2 * GLA_DV
    w_in = w_in_a[0]
    wa = jnp.pad(w_in[:, n_main:], ((0, 0), (0, LANES - GATE_RANK)))
    wa2 = jnp.pad(w_a2[0], ((0, LANES - GATE_RANK), (0, 0)))
    return dict(
        wm=w_in[:, :n_main].astype(BF), wa=wa.astype(BF), wa2=wa2.astype(BF), ba=b_a[0][None, :],
        wo_a=w_o_a[0].astype(BF), wkv=w_kv.astype(BF),
        wq=(w_q_b[0] * HEAD_DIM ** -0.5).astype(BF),
        wo_b=w_o_b[0].astype(BF),
        wgu=[w_gate_up[l].astype(BF) for l in range(2)], wd=[w_down[l].astype(BF) for l in range(2)])


def _layer0(x, s0, norm_g, g_onorm, wts):
    bsz, t, _ = x.shape
    x2 = x.reshape(bsz * t, D_MODEL)
    q, k, v, r, g = _gla_in(x2, norm_g[0], wts["wm"], wts["wa"], wts["wa2"], wts["ba"])
    sh = lambda a: a.reshape(bsz, t, a.shape[-1])
    o, st = _gla(sh(q), sh(k), sh(v), sh(g), jnp.swapaxes(s0, -1, -2))
    h = _mix_ffn([o.reshape(bsz * t, GLA_DV), r], x2, g_onorm, wts["wo_a"], norm_g[0], wts["wgu"][0],
                 wts["wd"][0])
    return h, jnp.swapaxes(st, -1, -2)


def kernel(x_prompt, x_sample, state_gla, cache_win1, cache_win2, cache_win3, norm_g, w_in_a, w_a2, b_a,
           g_onorm, w_o_a, g_kv, w_kv, w_q_b, w_o_b, rel_bias, w_gate_up, w_down):
    wts = _prep_weights(w_in_a, w_a2, b_a, w_o_a, w_kv, w_q_b, w_o_b, w_gate_up, w_down)
    gkv = g_kv[None, :]

    bp, tp, _ = x_prompt.shape
    s0p = jnp.zeros((bp, GLA_HEADS, DKH, DVH), F32)
    h_p, gla_p = _layer0(x_prompt, s0p, norm_g, g_onorm, wts)
    *kv_p, q_p = _qkv(h_p, gkv, norm_g[1], wts["wkv"], wts["wq"])
    kv_p = [a.reshape(bp, tp, KV_WIDTH) for a in kv_p]
    o_p = _attn_prompt(q_p.reshape(bp, tp, -1), kv_p, *_prompt_bias_tables(rel_bias))
    y_p = _mix_ffn([o_p.reshape(bp * tp, D_MODEL)], h_p, None, wts["wo_b"], norm_g[1], wts["wgu"][1],
                   wts["wd"][1])
    win_p = [kv_p[g][:, tp - min(WINDOWS[g], tp):].reshape(bp, -1, 2, KV_HEADS, HEAD_DIM) for g in range(N_GROUPS)]

    bs, ts, _ = x_sample.shape
    h_s, gla_s = _layer0(x_sample, state_gla[0], norm_g, g_onorm, wts)
    *kv_s, q_s = _qkv(h_s, gkv, norm_g[1], wts["wkv"], wts["wq"])
    q6 = q_s.reshape(bs, ts, N_GROUPS, KV_HEADS, Q_PER_KV, HEAD_DIM).transpose(0, 2, 3, 4, 1, 5)
    q6 = q6.reshape(bs, N_GROUPS, KV_HEADS, Q_PER_KV * ts, 1, HEAD_DIM)
    slot = jnp.eye(KV_HEADS, dtype=F32)[None, None, :, None, :, None]
    qpad = (q6 * slot).reshape(bs, N_GROUPS, KV_HEADS, Q_PER_KV * ts, KV_HEADS * HEAD_DIM).astype(BF)
    caches = [c.reshape(bs, c.shape[1], KV_WIDTH) for c in (cache_win1, cache_win2, cache_win3)]
    bias_c, bias_n = _sample_bias_tables(rel_bias, ts)
    o_s, *win_s = _attn_sample(qpad, [a.reshape(bs, ts, KV_WIDTH) for a in kv_s], caches, bias_c, bias_n)
    o_s = o_s.reshape(bs, Q_PER_KV, ts, KV_HEADS, HEAD_DIM).transpose(0, 2, 3, 1, 4).reshape(bs * ts, D_MODEL)
    y_s = _mix_ffn([o_s], h_s, None, wts["wo_b"], norm_g[1], wts["wgu"][1], wts["wd"][1])
    win_s = [w.reshape(bs, w.shape[1], 2, KV_HEADS, HEAD_DIM) for w in win_s]

    return (y_p.reshape(bp, tp, D_MODEL), y_s.reshape(bs, ts, D_MODEL), gla_p[None], win_p[0], win_p[1],
            win_p[2], gla_s[None], win_s[0], win_s[1], win_s[2])
```

```python
import functools

import numpy as np
import jax
import jax.numpy as jnp
from jax import lax
from jax.experimental import pallas as pl
from jax.experimental.pallas import tpu as pltpu

BF = jnp.bfloat16
F32 = jnp.float32

D_MODEL = 1024
GLA_HEADS = 4
GLA_DK = 512
GLA_DV = 1024
DKH = GLA_DK // GLA_HEADS
DVH = GLA_DV // GLA_HEADS
GATE_RANK = 16
GATE_TAU = 16.0
GLA_CHUNK = 64
GLA_SUB = 16
GLA_UNROLL = 4
WINDOWS = (128, 512, 2048)
DILATIONS = (1, 4, 16)
N_GROUPS = 3
HEAD_DIM = 64
HEADS_PER_GROUP = 16
KV_HEADS = 4
Q_PER_KV = 4
N_PAIR = Q_PER_KV // 2
N_KEYS = 129
N_BUCKETS = 32
MAX_EXACT = 16
MAX_DISTANCE = 2048
D_FF = 2816
EPS = 1e-6
NEG = -1e30
QB = 128
LANES = 128
KV_WIDTH = 2 * KV_HEADS * HEAD_DIM
VMEM_LIMIT_BYTES = 56 * 1024 * 1024


def _dot(a, b):
    return jnp.dot(a, b, preferred_element_type=F32)


def _dot_nt(a, b):
    return lax.dot_general(a, b, (((1,), (1,)), ((), ())), preferred_element_type=F32)


def _rms(x, g):
    return x * lax.rsqrt(jnp.mean(x * x, axis=-1, keepdims=True) + EPS) * g


def _sigmoid(x):
    return 1.0 / (1.0 + jnp.exp(-x))


def _const_spec(shape):
    nd = len(shape)
    return pl.BlockSpec(shape, lambda *_: (0,) * nd, pipeline_mode=pl.Buffered(1))


def _params(*sem):
    return pltpu.CompilerParams(dimension_semantics=sem, vmem_limit_bytes=VMEM_LIMIT_BYTES)


def _row_tile(n, want):
    tm = min(n, want)
    assert n % tm == 0
    return tm


def _gla_in_body(x_ref, ng_ref, wm_ref, wa_ref, wa2_ref, ba_ref, q_ref, k_ref, v_ref, r_ref, g_ref):
    xn = _rms(x_ref[...], ng_ref[0:1, :]).astype(BF)
    q_ref[...] = _dot(xn, wm_ref[:, 0:GLA_DK]) * (DKH ** -0.5)
    k_ref[...] = _dot(xn, wm_ref[:, GLA_DK:2 * GLA_DK])
    v_ref[...] = _dot(xn, wm_ref[:, 2 * GLA_DK:2 * GLA_DK + GLA_DV])
    r_ref[...] = _dot(xn, wm_ref[:, 2 * GLA_DK + GLA_DV:])
    a = _dot(xn, wa_ref[...]).astype(BF)
    z = _dot(a, wa2_ref[...]) + ba_ref[...]
    g_ref[...] = (jnp.minimum(z, 0.0) - jnp.log(1.0 + jnp.exp(-jnp.abs(z)))) * (1.0 / GATE_TAU)


def _gla_in(x2, ng, wm, wa, wa2, ba):
    n = x2.shape[0]
    tm = _row_tile(n, 512)
    row = lambda w: pl.BlockSpec((tm, w), lambda i: (i, 0))
    return pl.pallas_call(
        _gla_in_body,
        grid=(n // tm,),
        in_specs=[row(D_MODEL), _const_spec(ng.shape), _const_spec(wm.shape), _const_spec(wa.shape),
                  _const_spec(wa2.shape), _const_spec(ba.shape)],
        out_specs=[row(GLA_DK), row(GLA_DK), row(GLA_DV), row(GLA_DV), row(GLA_DK)],
        out_shape=[jax.ShapeDtypeStruct((n, w), F32) for w in (GLA_DK, GLA_DK, GLA_DV, GLA_DV, GLA_DK)],
        compiler_params=_params("parallel"),
        name="gla_in",
    )(x2, ng, wm, wa, wa2, ba)


def _gla_body(c_real, q_ref, k_ref, v_ref, g_ref, s0_ref, o_ref, st_ref, s_scr):
    t = pl.program_id(1)
    c = GLA_CHUNK
    nsb = c // GLA_SUB
    n_chunks = q_ref.shape[1] // c_real

    @pl.when(t == 0)
    def _():
        s_scr[...] = s0_ref[0]

    gc = min(n_chunks, GLA_UNROLL)
    assert n_chunks % gc == 0
    ri = lax.broadcasted_iota(jnp.int32, (gc * c, gc * c), 0)
    ci = lax.broadcasted_iota(jnp.int32, (gc * c, gc * c), 1)
    tril_bf = jnp.where((ri >= ci) & (ri // c == ci // c), 1.0, 0.0).astype(BF)
    ri2 = lax.broadcasted_iota(jnp.int32, (c, LANES), 0)
    ci2 = lax.broadcasted_iota(jnp.int32, (c, LANES), 1)
    causal = ri2 >= ci2

    def pad_rows(a, rows):
        if a.shape[0] == rows:
            return a
        return jnp.concatenate([a, jnp.zeros((rows - a.shape[0], a.shape[1]), a.dtype)], axis=0)

    heads = range(GLA_HEADS)
    ks = [slice(h * DKH, (h + 1) * DKH) for h in heads]
    vs = [slice(h * DVH, (h + 1) * DVH) for h in heads]
    units = [(j, h) for j in range(gc) for h in heads]

    def group(idx, carry):
        span = gc * c_real
        grows = pl.ds(pl.multiple_of(idx * span, span), span)
        rows = [pl.ds(pl.multiple_of(idx * span + j * c_real, c_real), c_real) for j in range(gc)]
        g_all = pad_rows(g_ref[0, grows, :], gc * c)
        g_hi = g_all.astype(BF)
        g_lo = (g_all - g_hi.astype(F32)).astype(BF)
        b_all = _dot(tril_bf, g_hi) + _dot(tril_bf, g_lo)
        b = {(j, h): b_all[j * c:(j + 1) * c, ks[h]] for j, h in units}
        qh = {(j, h): pad_rows(q_ref[0, rows[j], ks[h]], c) for j, h in units}
        kh = {(j, h): pad_rows(k_ref[0, rows[j], ks[h]], c) for j, h in units}
        vh = {(j, h): pad_rows(v_ref[0, rows[j], vs[h]], LANES) for j, h in units}
        b_last = {u: b[u][c - 1:c, :] for u in units}
        scores = {}
        for u in units:
            qparts, kparts = [], []
            for sbi in range(nsb):
                lo, hi = sbi * GLA_SUB, (sbi + 1) * GLA_SUB
                ref_row = b[u][lo:lo + 1, :]
                qj = (qh[u][lo:] * jnp.exp(b[u][lo:] - ref_row)).astype(BF)
                qparts.append(jnp.concatenate([jnp.zeros((lo, DKH), BF), qj], axis=0) if lo else qj)
                kj = (kh[u][lo:hi] * jnp.exp(ref_row - b[u][lo:hi])).astype(BF)
                pieces = []
                if lo > 0:
                    pieces.append(jnp.zeros((lo, DKH), BF))
                pieces.append(kj)
                pieces.append(jnp.zeros((LANES - hi, DKH), BF))
                kparts.append(jnp.concatenate(pieces, axis=0))
            qcat = jnp.concatenate(qparts, axis=1)
            kcat = jnp.concatenate(kparts, axis=1)
            scores[u] = _dot_nt(qcat, kcat)
        upd, decay = {}, {}
        for u in units:
            k2 = pad_rows(kh[u] * jnp.exp(b_last[u] - b[u]), LANES)
            upd[u] = _dot(k2.T.astype(BF), vh[u].astype(BF))
            col = jnp.broadcast_to(jnp.exp(b_last[u]), (LANES, DKH)).T
            decay[u] = jnp.concatenate([col] * (DVH // LANES), axis=1)
        q_in = {u: (qh[u] * jnp.exp(b[u])).astype(BF) for u in units}
        o_intra = {u: _dot(jnp.where(causal, scores[u], 0.0).astype(BF), vh[u].astype(BF)) for u in units}
        st = [s_scr[h] for h in heads]
        for j in range(gc):
            for h in heads:
                o = _dot(q_in[j, h], st[h].astype(BF)) + o_intra[j, h]
                o_ref[0, rows[j], vs[h]] = o[0:c_real]
            st = [st[h] * decay[j, h] + upd[j, h] for h in heads]
        for h in heads:
            s_scr[h] = st[h]
        return carry

    lax.fori_loop(0, n_chunks // gc, group, 0)

    @pl.when(t == pl.num_programs(1) - 1)
    def _():
        st_ref[0] = s_scr[...]


def _gla(q, k, v, g, s0):
    bsz, t, _ = q.shape
    c_real = min(t, GLA_CHUNK)
    tc = min(t, 512)
    assert t % tc == 0 and tc % c_real == 0
    blk = lambda w: pl.BlockSpec((1, tc, w), lambda b, i: (b, i, 0))
    st_spec = pl.BlockSpec((1, GLA_HEADS, DKH, DVH), lambda b, i: (b, 0, 0, 0))
    return pl.pallas_call(
        functools.partial(_gla_body, c_real),
        grid=(bsz, t // tc),
        in_specs=[blk(GLA_DK), blk(GLA_DK), blk(GLA_DV), blk(GLA_DK), st_spec],
        out_specs=[blk(GLA_DV), st_spec],
        out_shape=[jax.ShapeDtypeStruct((bsz, t, GLA_DV), F32),
                   jax.ShapeDtypeStruct((bsz, GLA_HEADS, DKH, DVH), F32)],
        scratch_shapes=[pltpu.VMEM((GLA_HEADS, DKH, DVH), F32)],
        compiler_params=_params("parallel", "arbitrary"),
        name="gla",
    )(q, k, v, g, s0)


def _mix_ffn_body(gated, *refs):
    if gated:
        o_ref, r_ref, x_ref, gon_ref, wo_ref, ng_ref, wgu_ref, wd_ref, out_ref = refs
        o = o_ref[...]
        r = r_ref[...]
        on = jnp.concatenate(
            [_rms(o[:, h * DVH:(h + 1) * DVH], gon_ref[...]) for h in range(GLA_HEADS)], axis=1)
        m = on * (r * _sigmoid(r))
    else:
        m_ref, x_ref, wo_ref, ng_ref, wgu_ref, wd_ref, out_ref = refs
        m = m_ref[...]
    y = _dot(m.astype(BF), wo_ref[...])
    h1 = x_ref[...] + _rms(y, ng_ref[1:2, :])
    u = _rms(h1, ng_ref[2:3, :]).astype(BF)
    gu = _dot(u, wgu_ref[...])
    gate = gu[:, :D_FF]
    act = (gate * _sigmoid(gate) * gu[:, D_FF:]).astype(BF)
    f = _dot(act, wd_ref[...])
    out_ref[...] = h1 + _rms(f, ng_ref[3:4, :])


def _mix_ffn(mix_inputs, x2, gon, wo, ng, wgu, wd):
    n = x2.shape[0]
    tm = _row_tile(n, 256)
    row = pl.BlockSpec((tm, D_MODEL), lambda i: (i, 0))
    gated = gon is not None
    args = list(mix_inputs) + [x2] + ([gon] if gated else []) + [wo, ng, wgu, wd]
    in_specs = [row] * (len(mix_inputs) + 1) + [_const_spec(a.shape) for a in args[len(mix_inputs) + 1:]]
    return pl.pallas_call(
        functools.partial(_mix_ffn_body, gated),
        grid=(n // tm,),
        in_specs=in_specs,
        out_specs=row,
        out_shape=jax.ShapeDtypeStruct((n, D_MODEL), F32),
        compiler_params=_params("parallel"),
        name="mix_ffn_gated" if gated else "mix_ffn",
    )(*args)


def _qkv_body(h_ref, gkv_ref, ng_ref, wkv_ref, wq_ref, kv0_ref, kv1_ref, kv2_ref, q_ref, *kvt_refs):
    h = h_ref[...]
    hn = h * lax.rsqrt(jnp.mean(h * h, axis=-1, keepdims=True) + EPS)
    hkv = (hn * gkv_ref[...]).astype(BF)
    for g, kv_ref in enumerate((kv0_ref, kv1_ref, kv2_ref)):
        kv = _dot(hkv, wkv_ref[:, g * KV_WIDTH:(g + 1) * KV_WIDTH])
        kv_ref[...] = kv
        if kvt_refs and g == N_GROUPS - 1:
            kvt_refs[0][0] = kv.T
    q_ref[...] = _dot((hn * ng_ref[0:1, :]).astype(BF), wq_ref[...])


def _qkv(h2, gkv, ng, wkv, wq, seq_len=None):
    n = h2.shape[0]
    tm = _row_tile(n, 512)
    row = lambda w: pl.BlockSpec((tm, w), lambda i: (i, 0))
    nq = wq.shape[1]
    out_specs = [row(KV_WIDTH)] * N_GROUPS + [row(nq)]
    out_shape = [jax.ShapeDtypeStruct((n, KV_WIDTH), F32)] * N_GROUPS + [jax.ShapeDtypeStruct((n, nq), F32)]
    if seq_len is not None:
        assert seq_len % tm == 0
        per_seq = seq_len // tm
        out_specs.append(pl.BlockSpec((1, KV_WIDTH, tm), lambda i: (i // per_seq, 0, i % per_seq)))
        out_shape.append(jax.ShapeDtypeStruct((n // seq_len, KV_WIDTH, seq_len), F32))
    return pl.pallas_call(
        _qkv_body,
        grid=(n // tm,),
        in_specs=[row(D_MODEL), _const_spec(gkv.shape), _const_spec(ng.shape), _const_spec(wkv.shape),
                  _const_spec(wq.shape)],
        out_specs=out_specs,
        out_shape=out_shape,
        compiler_params=_params("parallel"),
        name="qkv",
    )(h2, gkv, ng, wkv, wq)


def _attn_prompt_body(q00, q01, q10, q11, q20, q21, k0, k1, k2, v0, v1, v2, bias_rest_ref, bias_first_ref,
                      o_ref, o_scr, l_scr, m_scr, klo_scr, khi_scr, vlo_scr, vhi_scr):
    t_len = k0.shape[1]
    par = pl.program_id(1) % 2
    lane = lax.broadcasted_iota(jnp.int32, (1, LANES), 1)
    low = lane < HEAD_DIM
    same = (lane >= HEAD_DIM).astype(jnp.int32) == par

    def strided_rows(start, d):
        if d > 1:
            return pl.ds(start, QB, stride=d)
        return pl.ds(start if isinstance(start, int) else pl.multiple_of(start, QB), QB)

    def block_rows(blk, n=1):
        return pl.ds(pl.multiple_of(blk * QB, QB), n * QB)

    def prepare(g, k_ref, v_ref, d, nblk, blk):
        rho = blk // nblk
        rows = strided_rows(rho + d * QB * (blk - rho * nblk), d)
        kp = k_ref[0, rows, :]
        vp = v_ref[0, rows, :]
        k_both = jnp.where(same, kp, pltpu.roll(kp, HEAD_DIM, axis=1))
        v_both = jnp.where(same, vp, pltpu.roll(vp, HEAD_DIM, axis=1))
        dst = block_rows(blk)
        klo_scr[dst, :] = jnp.where(low, k_both, 0.0).astype(BF)
        khi_scr[dst, :] = jnp.where(low, 0.0, k_both).astype(BF)
        vlo_scr[dst, :] = jnp.where(low, v_both, 0.0).astype(BF)
        vhi_scr[dst, :] = jnp.where(low, 0.0, v_both).astype(BF)

    def attend(g, q_refs, d, nblk, blk, with_prev):
        rho = blk // nblk
        rows_q = strided_rows(rho + d * QB * (blk - rho * nblk), d)
        keys = block_rows(blk - 1, 2) if with_prev else block_rows(blk)
        nk = 2 * QB if with_prev else QB
        q = jnp.concatenate([q_refs[p][0, rows_q, :] for p in range(N_PAIR)], axis=0).astype(BF)
        k_cat = jnp.concatenate([klo_scr[keys, :], khi_scr[keys, :]], axis=0)
        bias = bias_rest_ref[g] if with_prev else bias_first_ref[g]
        s = _dot_nt(q, k_cat) + bias
        mx, ps = [], []
        for odd in range(2):
            sh = s[:, odd * nk:(odd + 1) * nk]
            mx.append(jnp.max(sh, axis=-1, keepdims=True))
            ps.append(jnp.exp(sh - mx[odd]).astype(BF))
        ones_lo = jnp.broadcast_to(jnp.where(low, 1.0, 0.0), (nk, LANES)).astype(BF)
        ones_hi = jnp.broadcast_to(jnp.where(low, 0.0, 1.0), (nk, LANES)).astype(BF)
        rhs = jnp.concatenate([jnp.concatenate([vlo_scr[keys, :], ones_lo], axis=1),
                               jnp.concatenate([vhi_scr[keys, :], ones_hi], axis=1)], axis=0)
        res = _dot(jnp.concatenate(ps, axis=1), rhs)
        m = jnp.where(low, mx[0], mx[1])
        for p in range(N_PAIR):
            slot = N_PAIR * g + p
            o_scr[slot, rows_q, :] = res[p * QB:(p + 1) * QB, :LANES]
            l_scr[slot, rows_q, :] = res[p * QB:(p + 1) * QB, LANES:]
            m_scr[slot, rows_q, :] = m[p * QB:(p + 1) * QB]

    groups = (((q00, q01), k0, v0), ((q10, q11), k1, v1), ((q20, q21), k2, v2))
    for g, (q_refs, k_ref, v_ref) in enumerate(groups):
        d = DILATIONS[g]
        nblk = t_len // d // QB

        def prep_step(blk, carry, g=g, k_ref=k_ref, v_ref=v_ref, d=d, nblk=nblk):
            prepare(g, k_ref, v_ref, d, nblk, blk)
            return carry

        def first_step(rho, carry, g=g, q_refs=q_refs, d=d, nblk=nblk):
            attend(g, q_refs, d, nblk, rho * nblk, False)
            return carry

        def rest_step(n, carry, g=g, q_refs=q_refs, d=d, nblk=nblk):
            rho = n // (nblk - 1)
            attend(g, q_refs, d, nblk, n + rho + 1, True)
            return carry

        lax.fori_loop(0, d * nblk, prep_step, 0, unroll=4)
        lax.fori_loop(0, d, first_step, 0, unroll=min(d, 4))
        if nblk > 1:
            n_rest = d * (nblk - 1)
            lax.fori_loop(0, n_rest, rest_step, 0, unroll=5 if n_rest % 5 == 0 else 4)

    def merge(i, carry):
        rows = block_rows(i)
        for p in range(N_PAIR):
            slots = [N_PAIR * g + p for g in range(N_GROUPS)]
            ms = [m_scr[s_, rows, :] for s_ in slots]
            top = jnp.maximum(jnp.maximum(ms[0], ms[1]), ms[2])
            es = [jnp.exp(m - top) for m in ms]
            num = sum(e * o_scr[s_, rows, :] for e, s_ in zip(es, slots))
            den = sum(e * l_scr[s_, rows, :] for e, s_ in zip(es, slots))
            o_ref[0, rows, p * LANES:(p + 1) * LANES] = num / den
        return carry

    lax.fori_loop(0, t_len // QB, merge, 0)


def _attn_prompt(q, kvs, bias_rest, bias_first):
    bsz, t, _ = q.shape
    assert t % (QB * DILATIONS[-1]) == 0
    wq = Q_PER_KV * HEAD_DIM
    heads_per_block = LANES // HEAD_DIM
    q_specs = [pl.BlockSpec((1, t, LANES), lambda b, h, g=g, p=p: (b, 0, (g * KV_HEADS + h) * N_PAIR + p))
               for g in range(N_GROUPS) for p in range(N_PAIR)]
    k_spec = pl.BlockSpec((1, t, LANES), lambda b, h: (b, 0, h // heads_per_block))
    v_spec = pl.BlockSpec((1, t, LANES), lambda b, h: (b, 0, KV_HEADS // heads_per_block + h // heads_per_block))
    bias_specs = [pl.BlockSpec((N_GROUPS, N_PAIR * QB, tab.shape[-1]), lambda b, h: (0, h, 0))
                  for tab in (bias_rest, bias_first)]
    slots = N_GROUPS * N_PAIR
    return pl.pallas_call(
        _attn_prompt_body,
        grid=(bsz, KV_HEADS),
        in_specs=q_specs + [k_spec] * N_GROUPS + [v_spec] * N_GROUPS + bias_specs,
        out_specs=pl.BlockSpec((1, t, wq), lambda b, h: (b, 0, h)),
        out_shape=jax.ShapeDtypeStruct((bsz, t, KV_HEADS * wq), F32),
        scratch_shapes=[pltpu.VMEM((slots, t, LANES), F32)] * 3 + [pltpu.VMEM((t, LANES), BF)] * 4,
        compiler_params=_params("parallel", "arbitrary"),
        name="attn_prompt",
    )(*([q] * slots), *kvs, *kvs, bias_rest, bias_first)


def _merge_groups(o_parts, lse_parts):
    mx = jnp.maximum(jnp.maximum(lse_parts[0], lse_parts[1]), lse_parts[2])
    es = [jnp.exp(l - mx) for l in lse_parts]
    num = es[0] * o_parts[0] + es[1] * o_parts[1] + es[2] * o_parts[2]
    return num / (es[0] + es[1] + es[2])


def _attn_sample_body(qp_ref, n0, n1, n2, c0, c1, c2, bc0, bc1, bc2, bn_ref, o_ref, w0, w1, w2):
    n_new = n0.shape[1]
    half = KV_HEADS * HEAD_DIM
    lane = lax.broadcasted_iota(jnp.int32, (1, half), 1)
    tail_lane = lax.broadcasted_iota(jnp.int32, (1, LANES), 1)
    o_parts, lse_parts = [], []
    for g, (new_ref, c_ref, bc_ref, w_ref) in enumerate(((n0, c0, bc0, w0), (n1, c1, bc1, w1), (n2, c2, bc2, w2))):
        w = c_ref.shape[2]
        new = new_ref[0]
        newt = jnp.concatenate([new, jnp.zeros((LANES - n_new, 2 * half), F32)], axis=0).T
        ct = c_ref[0]
        shifted = pltpu.roll(ct, w - n_new, axis=1)
        tail = jnp.where(tail_lane >= LANES - n_new, pltpu.roll(newt, LANES - n_new, axis=1), shifted[:, w - LANES:])
        w_ref[0] = tail if w == LANES else jnp.concatenate([shifted[:, :w - LANES], tail], axis=1)
        kc, vc = ct[:half].astype(BF), ct[half:].astype(BF)
        kn, vn = newt[:half].astype(BF), newt[half:].astype(BF)
        o_g = jnp.zeros((Q_PER_KV * n_new, half), F32)
        l_g = jnp.zeros((Q_PER_KV * n_new, half), F32)
        for h in range(KV_HEADS):
            qp = qp_ref[0, g, h]
            s_c = _dot(qp, kc) + bc_ref[h]
            s_n = _dot(qp, kn) + bn_ref[g, h]
            m = jnp.maximum(jnp.max(s_c, axis=-1, keepdims=True), jnp.max(s_n, axis=-1, keepdims=True))
            p_c = jnp.exp(s_c - m)
            p_n = jnp.exp(s_n - m)
            l = jnp.sum(p_c, axis=-1, keepdims=True) + jnp.sum(p_n, axis=-1, keepdims=True)
            o = (_dot_nt(p_c.astype(BF), vc) + _dot_nt(p_n.astype(BF), vn)) / l
            mine = (lane >= h * HEAD_DIM) & (lane < (h + 1) * HEAD_DIM)
            o_g = jnp.where(mine, o, o_g)
            l_g = jnp.where(mine, m + jnp.log(l), l_g)
        o_parts.append(o_g)
        lse_parts.append(l_g)
    o_ref[0] = _merge_groups(o_parts, lse_parts)


def _attn_sample(qpad, new_kvs, caches, bias_c, bias_n):
    bsz = qpad.shape[0]
    n_new = new_kvs[0].shape[1]
    rows = Q_PER_KV * n_new
    half = KV_HEADS * HEAD_DIM
    per_b = lambda shape: pl.BlockSpec((1,) + shape[1:], lambda b: (b,) + (0,) * (len(shape) - 1))
    cache_specs = [per_b(c.shape) for c in caches]
    return pl.pallas_call(
        _attn_sample_body,
        grid=(bsz,),
        in_specs=[per_b(qpad.shape)] + [per_b(a.shape) for a in new_kvs] + cache_specs
                 + [_const_spec(b.shape) for b in bias_c] + [_const_spec(bias_n.shape)],
        out_specs=[per_b((bsz, rows, half))] + cache_specs,
        out_shape=[jax.ShapeDtypeStruct((bsz, rows, half), F32)]
                  + [jax.ShapeDtypeStruct(c.shape, F32) for c in caches],
        compiler_params=_params("parallel"),
        name="attn_sample",
    )(qpad, *new_kvs, *caches, *bias_c, bias_n)


def _t5_buckets(dist):
    d = np.asarray(dist)
    large = MAX_EXACT + (np.log(np.maximum(d, 1) / MAX_EXACT) / np.log(MAX_DISTANCE / MAX_EXACT)
                         * (N_BUCKETS - MAX_EXACT)).astype(np.int64)
    large = np.minimum(large, N_BUCKETS - 1)
    return np.where(d < MAX_EXACT, d, large).astype(np.int32)


def _group_bias(rel_bias, g):
    bk = _t5_buckets(DILATIONS[g] * np.arange(N_KEYS))
    return rel_bias[bk][:, g * HEADS_PER_GROUP:(g + 1) * HEADS_PER_GROUP].T.astype(F32)


def _prompt_bias_tables(rel_bias):
    period = 2 * QB
    rest, first = [], []
    for g in range(N_GROUPS):
        bv = _group_bias(rel_bias, g)
        row0 = jnp.concatenate([bv[:, ::-1], jnp.full((HEADS_PER_GROUP, period - N_KEYS), NEG, F32)], axis=1)
        wrap = jnp.concatenate([row0, row0, row0[:, :1]], axis=1)
        skew = jnp.broadcast_to(wrap[:, None, :], (HEADS_PER_GROUP, QB, 2 * period + 1))
        skew = skew.reshape(HEADS_PER_GROUP, -1)[:, :QB * 2 * period].reshape(HEADS_PER_GROUP, QB, 2 * period)
        tab = skew[:, :, period:]
        for out, part in ((rest, tab), (first, tab[:, :, QB:])):
            nk = part.shape[-1]
            pairs = part.reshape(HEADS_PER_GROUP // 2, 2, QB, nk).transpose(0, 2, 1, 3)
            out.append(pairs.reshape(HEADS_PER_GROUP // 2 * QB, 2 * nk))
    return jnp.stack(rest, axis=0), jnp.stack(first, axis=0)


def _sample_bias_tables(rel_bias, n_new):
    tabs_c, tabs_n = [], []
    for g in range(N_GROUPS):
        w, d = WINDOWS[g], DILATIONS[g]
        bv = _group_bias(rel_bias, g)
        fill = jnp.full(bv.shape, NEG, F32)
        dil = jnp.stack([bv] + [fill] * (d - 1), axis=-1).reshape(HEADS_PER_GROUP, N_KEYS * d)[:, :w + 1]
        ext = jnp.concatenate([jnp.full((HEADS_PER_GROUP, LANES - 1), NEG, F32), dil,
                               jnp.full((HEADS_PER_GROUP, n_new - 1), NEG, F32)], axis=1)
        rev = ext[:, ::-1]
        rows = [rev[:, n_new - 1 - r:n_new - 1 - r + w + LANES] for r in range(n_new)]
        tab = jnp.stack(rows, axis=1).reshape(KV_HEADS, Q_PER_KV * n_new, w + LANES)
        tabs_c.append(tab[:, :, :w])
        tabs_n.append(tab[:, :, w:])
    return tabs_c, jnp.stack(tabs_n, axis=0)


def _prep_weights(w_in_a, w_a2, b_a, w_o_a, w_kv, w_q_b, w_o_b, w_gate_up, w_down):
    n_main = 2 * GLA_DK + 2 * GLA_DV
    w_in = w_in_a[0]
    wa = jnp.pad(w_in[:, n_main:], ((0, 0), (0, LANES - GATE_RANK)))
    wa2 = jnp.pad(w_a2[0], ((0, LANES - GATE_RANK), (0, 0)))
    return dict(
        wm=w_in[:, :n_main].astype(BF), wa=wa.astype(BF), wa2=wa2.astype(BF), ba=b_a[0][None, :],
        wo_a=w_o_a[0].astype(BF), wkv=w_kv.astype(BF),
        wq=(w_q_b[0] * HEAD_DIM ** -0.5).astype(BF),
        wo_b=w_o_b[0].astype(BF),
        wgu=[w_gate_up[l].astype(BF) for l in range(2)], wd=[w_down[l].astype(BF) for l in range(2)])


def _from_positions_last(a):
    return jnp.transpose(a.reshape(a.shape[0], 2, KV_HEADS, HEAD_DIM, a.shape[-1]), (0, 4, 1, 2, 3))


def _layer0(x, s0, norm_g, g_onorm, wts):
    bsz, t, _ = x.shape
    x2 = x.reshape(bsz * t, D_MODEL)
    q, k, v, r, g = _gla_in(x2, norm_g[0], wts["wm"], wts["wa"], wts["wa2"], wts["ba"])
    sh = lambda a: a.reshape(bsz, t, a.shape[-1])
    o, st = _gla(sh(q), sh(k), sh(v), sh(g), s0)
    h = _mix_ffn([o.reshape(bsz * t, GLA_DV), r], x2, g_onorm, wts["wo_a"], norm_g[0], wts["wgu"][0],
                 wts["wd"][0])
    return h, st


def kernel(x_prompt, x_sample, state_gla, cache_win1, cache_win2, cache_win3, norm_g, w_in_a, w_a2, b_a,
           g_onorm, w_o_a, g_kv, w_kv, w_q_b, w_o_b, rel_bias, w_gate_up, w_down):
    wts = _prep_weights(w_in_a, w_a2, b_a, w_o_a, w_kv, w_q_b, w_o_b, w_gate_up, w_down)
    gkv = g_kv[None, :]

    bp, tp, _ = x_prompt.shape
    s0p = jnp.zeros((bp, GLA_HEADS, DKH, DVH), F32)
    h_p, gla_p = _layer0(x_prompt, s0p, norm_g, g_onorm, wts)
    *kv_p, q_p, kvt_last = _qkv(h_p, gkv, norm_g[1], wts["wkv"], wts["wq"], seq_len=tp)
    kv_p = [a.reshape(bp, tp, KV_WIDTH) for a in kv_p]
    o_p = _attn_prompt(q_p.reshape(bp, tp, -1), kv_p, *_prompt_bias_tables(rel_bias))
    y_p = _mix_ffn([o_p.reshape(bp * tp, D_MODEL)], h_p, None, wts["wo_b"], norm_g[1], wts["wgu"][1],
                   wts["wd"][1])
    win_p = [kv_p[g][:, tp - min(WINDOWS[g], tp):].reshape(bp, -1, 2, KV_HEADS, HEAD_DIM)
             for g in range(N_GROUPS - 1)]
    assert WINDOWS[-1] >= tp
    win_p.append(_from_positions_last(kvt_last))

    bs, ts, _ = x_sample.shape
    h_s, gla_s = _layer0(x_sample, state_gla[0], norm_g, g_onorm, wts)
    *kv_s, q_s = _qkv(h_s, gkv, norm_g[1], wts["wkv"], wts["wq"])
    q6 = q_s.reshape(bs, ts, N_GROUPS, KV_HEADS, Q_PER_KV, HEAD_DIM).transpose(0, 2, 3, 4, 1, 5)
    q6 = q6.reshape(bs, N_GROUPS, KV_HEADS, Q_PER_KV * ts, 1, HEAD_DIM)
    slot = jnp.eye(KV_HEADS, dtype=F32)[None, None, :, None, :, None]
    qpad = (q6 * slot).reshape(bs, N_GROUPS, KV_HEADS, Q_PER_KV * ts, KV_HEADS * HEAD_DIM).astype(BF)
    caches = [jnp.transpose(c, (0, 2, 3, 4, 1)).reshape(bs, KV_WIDTH, c.shape[1])
              for c in (cache_win1, cache_win2, cache_win3)]
    bias_c, bias_n = _sample_bias_tables(rel_bias, ts)
    o_s, *win_s = _attn_sample(qpad, [a.reshape(bs, ts, KV_WIDTH) for a in kv_s], caches, bias_c, bias_n)
    o_s = o_s.reshape(bs, Q_PER_KV, ts, KV_HEADS, HEAD_DIM).transpose(0, 2, 3, 1, 4).reshape(bs * ts, D_MODEL)
    y_s = _mix_ffn([o_s], h_s, None, wts["wo_b"], norm_g[1], wts["wgu"][1], wts["wd"][1])
    win_s = [_from_positions_last(w) for w in win_s]

    return (y_p.reshape(bp, tp, D_MODEL), y_s.reshape(bs, ts, D_MODEL), gla_p[None], win_p[0], win_p[1],
            win_p[2], gla_s[None], win_s[0], win_s[1], win_s[2])
```

```python
import functools

import numpy as np
import jax
import jax.numpy as jnp
from jax import lax
from jax.experimental import pallas as pl
from jax.experimental.pallas import tpu as pltpu

BF = jnp.bfloat16
F32 = jnp.float32

D_MODEL = 1024
GLA_HEADS = 4
GLA_DK = 512
GLA_DV = 1024
DKH = GLA_DK // GLA_HEADS
DVH = GLA_DV // GLA_HEADS
GATE_RANK = 16
GATE_TAU = 16.0
GLA_CHUNK = 64
GLA_SUB = 16
GLA_UNROLL = 4
WINDOWS = (128, 512, 2048)
DILATIONS = (1, 4, 16)
N_GROUPS = 3
HEAD_DIM = 64
HEADS_PER_GROUP = 16
KV_HEADS = 4
Q_PER_KV = 4
N_PAIR = Q_PER_KV // 2
N_KEYS = 129
N_BUCKETS = 32
MAX_EXACT = 16
MAX_DISTANCE = 2048
D_FF = 2816
FF_SPLITS = (0, 1536, D_FF)
EPS = 1e-6
NEG = -1e30
QB = 128
LANES = 128
KV_WIDTH = 2 * KV_HEADS * HEAD_DIM
VMEM_LIMIT_BYTES = 56 * 1024 * 1024


def _dot(a, b):
    return jnp.dot(a, b, preferred_element_type=F32)


def _dot_nt(a, b):
    return lax.dot_general(a, b, (((1,), (1,)), ((), ())), preferred_element_type=F32)


def _rms(x, g):
    return x * lax.rsqrt(jnp.mean(x * x, axis=-1, keepdims=True) + EPS) * g


def _sigmoid(x):
    return 1.0 / (1.0 + jnp.exp(-x))


def _const_spec(shape):
    nd = len(shape)
    return pl.BlockSpec(shape, lambda *_: (0,) * nd, pipeline_mode=pl.Buffered(1))


def _params(*sem):
    return pltpu.CompilerParams(dimension_semantics=sem, vmem_limit_bytes=VMEM_LIMIT_BYTES)


def _row_tile(n, want):
    tm = min(n, want)
    assert n % tm == 0
    return tm


def _gla_in_body(x_ref, ng_ref, wm_ref, wa_ref, wa2_ref, ba_ref, q_ref, k_ref, v_ref, r_ref, g_ref):
    xn = _rms(x_ref[...], ng_ref[0:1, :]).astype(BF)
    q_ref[...] = _dot(xn, wm_ref[:, 0:GLA_DK]) * (DKH ** -0.5)
    k_ref[...] = _dot(xn, wm_ref[:, GLA_DK:2 * GLA_DK])
    v_ref[...] = _dot(xn, wm_ref[:, 2 * GLA_DK:2 * GLA_DK + GLA_DV])
    r_ref[...] = _dot(xn, wm_ref[:, 2 * GLA_DK + GLA_DV:])
    a = _dot(xn, wa_ref[...]).astype(BF)
    z = _dot(a, wa2_ref[...]) + ba_ref[...]
    g_ref[...] = (jnp.minimum(z, 0.0) - jnp.log(1.0 + jnp.exp(-jnp.abs(z)))) * (1.0 / GATE_TAU)


def _gla_in(x2, ng, wm, wa, wa2, ba):
    n = x2.shape[0]
    tm = _row_tile(n, 512)
    row = lambda w: pl.BlockSpec((tm, w), lambda i: (i, 0))
    return pl.pallas_call(
        _gla_in_body,
        grid=(n // tm,),
        in_specs=[row(D_MODEL), _const_spec(ng.shape), _const_spec(wm.shape), _const_spec(wa.shape),
                  _const_spec(wa2.shape), _const_spec(ba.shape)],
        out_specs=[row(GLA_DK), row(GLA_DK), row(GLA_DV), row(GLA_DV), row(GLA_DK)],
        out_shape=[jax.ShapeDtypeStruct((n, w), F32) for w in (GLA_DK, GLA_DK, GLA_DV, GLA_DV, GLA_DK)],
        compiler_params=_params("parallel"),
        name="gla_in",
    )(x2, ng, wm, wa, wa2, ba)


def _gla_body(c_real, q_ref, k_ref, v_ref, g_ref, s0_ref, o_ref, st_ref, s_scr):
    t = pl.program_id(1)
    c = GLA_CHUNK
    nsb = c // GLA_SUB
    n_chunks = q_ref.shape[1] // c_real

    @pl.when(t == 0)
    def _():
        s_scr[...] = s0_ref[0]

    gc = min(n_chunks, GLA_UNROLL)
    assert n_chunks % gc == 0
    ri = lax.broadcasted_iota(jnp.int32, (gc * c, gc * c), 0)
    ci = lax.broadcasted_iota(jnp.int32, (gc * c, gc * c), 1)
    tril_bf = jnp.where((ri >= ci) & (ri // c == ci // c), 1.0, 0.0).astype(BF)
    ri2 = lax.broadcasted_iota(jnp.int32, (c, LANES), 0)
    ci2 = lax.broadcasted_iota(jnp.int32, (c, LANES), 1)
    causal = ri2 >= ci2

    def pad_rows(a, rows):
        if a.shape[0] == rows:
            return a
        return jnp.concatenate([a, jnp.zeros((rows - a.shape[0], a.shape[1]), a.dtype)], axis=0)

    heads = range(GLA_HEADS)
    ks = [slice(h * DKH, (h + 1) * DKH) for h in heads]
    vs = [slice(h * DVH, (h + 1) * DVH) for h in heads]
    units = [(j, h) for j in range(gc) for h in heads]

    def group(idx, carry):
        span = gc * c_real
        grows = pl.ds(pl.multiple_of(idx * span, span), span)
        rows = [pl.ds(pl.multiple_of(idx * span + j * c_real, c_real), c_real) for j in range(gc)]
        g_all = pad_rows(g_ref[0, grows, :], gc * c)
        g_hi = g_all.astype(BF)
        g_lo = (g_all - g_hi.astype(F32)).astype(BF)
        b_all = _dot(tril_bf, g_hi) + _dot(tril_bf, g_lo)
        b = {(j, h): b_all[j * c:(j + 1) * c, ks[h]] for j, h in units}
        qh = {(j, h): pad_rows(q_ref[0, rows[j], ks[h]], c) for j, h in units}
        kh = {(j, h): pad_rows(k_ref[0, rows[j], ks[h]], c) for j, h in units}
        vh = {(j, h): pad_rows(v_ref[0, rows[j], vs[h]], LANES) for j, h in units}
        b_last = {u: b[u][c - 1:c, :] for u in units}
        scores = {}
        for u in units:
            qparts, kparts = [], []
            for sbi in range(nsb):
                lo, hi = sbi * GLA_SUB, (sbi + 1) * GLA_SUB
                ref_row = b[u][lo:lo + 1, :]
                qj = (qh[u][lo:] * jnp.exp(b[u][lo:] - ref_row)).astype(BF)
                qparts.append(jnp.concatenate([jnp.zeros((lo, DKH), BF), qj], axis=0) if lo else qj)
                kj = (kh[u][lo:hi] * jnp.exp(ref_row - b[u][lo:hi])).astype(BF)
                pieces = []
                if lo > 0:
                    pieces.append(jnp.zeros((lo, DKH), BF))
                pieces.append(kj)
                pieces.append(jnp.zeros((LANES - hi, DKH), BF))
                kparts.append(jnp.concatenate(pieces, axis=0))
            qcat = jnp.concatenate(qparts, axis=1)
            kcat = jnp.concatenate(kparts, axis=1)
            scores[u] = _dot_nt(qcat, kcat)
        upd, decay = {}, {}
        for u in units:
            k2 = pad_rows(kh[u] * jnp.exp(b_last[u] - b[u]), LANES)
            upd[u] = _dot(k2.T.astype(BF), vh[u].astype(BF))
            col = jnp.broadcast_to(jnp.exp(b_last[u]), (LANES, DKH)).T
            decay[u] = jnp.concatenate([col] * (DVH // LANES), axis=1)
        q_in = {u: (qh[u] * jnp.exp(b[u])).astype(BF) for u in units}
        o_intra = {u: _dot(jnp.where(causal, scores[u], 0.0).astype(BF), vh[u].astype(BF)) for u in units}
        st = [s_scr[h] for h in heads]
        for j in range(gc):
            for h in heads:
                o = _dot(q_in[j, h], st[h].astype(BF)) + o_intra[j, h]
                o_ref[0, rows[j], vs[h]] = o[0:c_real]
            st = [st[h] * decay[j, h] + upd[j, h] for h in heads]
        for h in heads:
            s_scr[h] = st[h]
        return carry

    lax.fori_loop(0, n_chunks // gc, group, 0)

    @pl.when(t == pl.num_programs(1) - 1)
    def _():
        st_ref[0] = s_scr[...]


def _gla(q, k, v, g, s0):
    bsz, t, _ = q.shape
    c_real = min(t, GLA_CHUNK)
    tc = min(t, 512)
    assert t % tc == 0 and tc % c_real == 0
    blk = lambda w: pl.BlockSpec((1, tc, w), lambda b, i: (b, i, 0))
    st_spec = pl.BlockSpec((1, GLA_HEADS, DKH, DVH), lambda b, i: (b, 0, 0, 0))
    return pl.pallas_call(
        functools.partial(_gla_body, c_real),
        grid=(bsz, t // tc),
        in_specs=[blk(GLA_DK), blk(GLA_DK), blk(GLA_DV), blk(GLA_DK), st_spec],
        out_specs=[blk(GLA_DV), st_spec],
        out_shape=[jax.ShapeDtypeStruct((bsz, t, GLA_DV), F32),
                   jax.ShapeDtypeStruct((bsz, GLA_HEADS, DKH, DVH), F32)],
        scratch_shapes=[pltpu.VMEM((GLA_HEADS, DKH, DVH), F32)],
        compiler_params=_params("parallel", "arbitrary"),
        name="gla",
    )(q, k, v, g, s0)


def _mix_ffn_body(gated, *refs):
    if gated:
        o_ref, r_ref, x_ref, gon_ref, wo_ref, ng_ref, wgu_ref, wd_ref, out_ref = refs
        o = o_ref[...]
        r = r_ref[...]
        on = jnp.concatenate(
            [_rms(o[:, h * DVH:(h + 1) * DVH], gon_ref[...]) for h in range(GLA_HEADS)], axis=1)
        m = on * (r * _sigmoid(r))
    else:
        m_ref, x_ref, wo_ref, ng_ref, wgu_ref, wd_ref, out_ref = refs
        m = m_ref[...]
    y = _dot(m.astype(BF), wo_ref[...])
    h1 = x_ref[...] + _rms(y, ng_ref[1:2, :])
    u = _rms(h1, ng_ref[2:3, :]).astype(BF)
    f = None
    for lo, hi in zip(FF_SPLITS[:-1], FF_SPLITS[1:]):
        gate = _dot(u, wgu_ref[:, lo:hi])
        up = _dot(u, wgu_ref[:, D_FF + lo:D_FF + hi])
        part = _dot((gate * _sigmoid(gate) * up).astype(BF), wd_ref[lo:hi, :])
        f = part if f is None else f + part
    out_ref[...] = h1 + _rms(f, ng_ref[3:4, :])


def _mix_ffn(mix_inputs, x2, gon, wo, ng, wgu, wd):
    n = x2.shape[0]
    tm = _row_tile(n, 512)
    row = pl.BlockSpec((tm, D_MODEL), lambda i: (i, 0))
    gated = gon is not None
    args = list(mix_inputs) + [x2] + ([gon] if gated else []) + [wo, ng, wgu, wd]
    in_specs = [row] * (len(mix_inputs) + 1) + [_const_spec(a.shape) for a in args[len(mix_inputs) + 1:]]
    return pl.pallas_call(
        functools.partial(_mix_ffn_body, gated),
        grid=(n // tm,),
        in_specs=in_specs,
        out_specs=row,
        out_shape=jax.ShapeDtypeStruct((n, D_MODEL), F32),
        compiler_params=_params("parallel"),
        name="mix_ffn_gated" if gated else "mix_ffn",
    )(*args)


def _qkv_body(h_ref, gkv_ref, ng_ref, wkv_ref, wq_ref, kv0_ref, kv1_ref, kv2_ref, q_ref, *kvt_refs):
    h = h_ref[...]
    hn = h * lax.rsqrt(jnp.mean(h * h, axis=-1, keepdims=True) + EPS)
    hkv = (hn * gkv_ref[...]).astype(BF)
    for g, kv_ref in enumerate((kv0_ref, kv1_ref, kv2_ref)):
        kv = _dot(hkv, wkv_ref[:, g * KV_WIDTH:(g + 1) * KV_WIDTH])
        kv_ref[...] = kv
        if kvt_refs and g == N_GROUPS - 1:
            kvt_refs[0][0] = kv.T
    q_ref[...] = _dot((hn * ng_ref[0:1, :]).astype(BF), wq_ref[...])


def _qkv(h2, gkv, ng, wkv, wq, seq_len=None):
    n = h2.shape[0]
    tm = _row_tile(n, 512)
    row = lambda w: pl.BlockSpec((tm, w), lambda i: (i, 0))
    nq = wq.shape[1]
    out_specs = [row(KV_WIDTH)] * N_GROUPS + [row(nq)]
    out_shape = [jax.ShapeDtypeStruct((n, KV_WIDTH), F32)] * N_GROUPS + [jax.ShapeDtypeStruct((n, nq), F32)]
    if seq_len is not None:
        assert seq_len % tm == 0
        per_seq = seq_len // tm
        out_specs.append(pl.BlockSpec((1, KV_WIDTH, tm), lambda i: (i // per_seq, 0, i % per_seq)))
        out_shape.append(jax.ShapeDtypeStruct((n // seq_len, KV_WIDTH, seq_len), F32))
    return pl.pallas_call(
        _qkv_body,
        grid=(n // tm,),
        in_specs=[row(D_MODEL), _const_spec(gkv.shape), _const_spec(ng.shape), _const_spec(wkv.shape),
                  _const_spec(wq.shape)],
        out_specs=out_specs,
        out_shape=out_shape,
        compiler_params=_params("parallel"),
        name="qkv",
    )(h2, gkv, ng, wkv, wq)


def _attn_prompt_body(q00, q01, q10, q11, q20, q21, k0, k1, k2, v0, v1, v2, bias_rest_ref, bias_first_ref,
                      o_ref, o_scr, l_scr, m_scr, klo_scr, khi_scr, vlo_scr, vhi_scr):
    t_len = k0.shape[1]
    par = pl.program_id(1) % 2
    lane = lax.broadcasted_iota(jnp.int32, (1, LANES), 1)
    low = lane < HEAD_DIM
    same = (lane >= HEAD_DIM).astype(jnp.int32) == par

    def strided_rows(start, d):
        if d > 1:
            return pl.ds(start, QB, stride=d)
        return pl.ds(start if isinstance(start, int) else pl.multiple_of(start, QB), QB)

    def block_rows(blk, n=1):
        return pl.ds(pl.multiple_of(blk * QB, QB), n * QB)

    def prepare(g, k_ref, v_ref, d, nblk, blk):
        rho = blk // nblk
        rows = strided_rows(rho + d * QB * (blk - rho * nblk), d)
        kp = k_ref[0, rows, :]
        vp = v_ref[0, rows, :]
        k_both = jnp.where(same, kp, pltpu.roll(kp, HEAD_DIM, axis=1))
        v_both = jnp.where(same, vp, pltpu.roll(vp, HEAD_DIM, axis=1))
        dst = block_rows(blk)
        klo_scr[dst, :] = jnp.where(low, k_both, 0.0).astype(BF)
        khi_scr[dst, :] = jnp.where(low, 0.0, k_both).astype(BF)
        vlo_scr[dst, :] = jnp.where(low, v_both, 0.0).astype(BF)
        vhi_scr[dst, :] = jnp.where(low, 0.0, v_both).astype(BF)

    def attend(g, q_refs, d, nblk, blk, with_prev):
        rho = blk // nblk
        rows_q = strided_rows(rho + d * QB * (blk - rho * nblk), d)
        keys = block_rows(blk - 1, 2) if with_prev else block_rows(blk)
        nk = 2 * QB if with_prev else QB
        q = jnp.concatenate([q_refs[p][0, rows_q, :] for p in range(N_PAIR)], axis=0).astype(BF)
        k_cat = jnp.concatenate([klo_scr[keys, :], khi_scr[keys, :]], axis=0)
        bias = bias_rest_ref[g] if with_prev else bias_first_ref[g]
        s = _dot_nt(q, k_cat) + bias
        mx, ps = [], []
        for odd in range(2):
            sh = s[:, odd * nk:(odd + 1) * nk]
            mx.append(jnp.max(sh, axis=-1, keepdims=True))
            ps.append(jnp.exp(sh - mx[odd]).astype(BF))
        ones_lo = jnp.broadcast_to(jnp.where(low, 1.0, 0.0), (nk, LANES)).astype(BF)
        ones_hi = jnp.broadcast_to(jnp.where(low, 0.0, 1.0), (nk, LANES)).astype(BF)
        rhs = jnp.concatenate([jnp.concatenate([vlo_scr[keys, :], ones_lo], axis=1),
                               jnp.concatenate([vhi_scr[keys, :], ones_hi], axis=1)], axis=0)
        res = _dot(jnp.concatenate(ps, axis=1), rhs)
        m = jnp.where(low, mx[0], mx[1])
        for p in range(N_PAIR):
            part = slice(p * QB, (p + 1) * QB)
            o_g, l_g, m_g = res[part, :LANES], res[part, LANES:], m[part]
            if g > 0:
                slot = N_PAIR * (g - 1) + p
                o_scr[slot, rows_q, :] = o_g
                l_scr[slot, rows_q, :] = l_g
                m_scr[slot, rows_q, :] = m_g
            else:
                slots = [N_PAIR * other + p for other in range(N_GROUPS - 1)]
                ms = [m_g] + [m_scr[s_, rows_q, :] for s_ in slots]
                os_ = [o_g] + [o_scr[s_, rows_q, :] for s_ in slots]
                ls = [l_g] + [l_scr[s_, rows_q, :] for s_ in slots]
                top = jnp.maximum(jnp.maximum(ms[0], ms[1]), ms[2])
                es = [jnp.exp(mm - top) for mm in ms]
                num = es[0] * os_[0] + es[1] * os_[1] + es[2] * os_[2]
                den = es[0] * ls[0] + es[1] * ls[1] + es[2] * ls[2]
                o_ref[0, rows_q, p * LANES:(p + 1) * LANES] = num / den

    groups = (((q00, q01), k0, v0), ((q10, q11), k1, v1), ((q20, q21), k2, v2))
    for g in reversed(range(N_GROUPS)):
        q_refs, k_ref, v_ref = groups[g]
        d = DILATIONS[g]
        nblk = t_len // d // QB

        def prep_step(blk, carry, g=g, k_ref=k_ref, v_ref=v_ref, d=d, nblk=nblk):
            prepare(g, k_ref, v_ref, d, nblk, blk)
            return carry

        def first_step(rho, carry, g=g, q_refs=q_refs, d=d, nblk=nblk):
            attend(g, q_refs, d, nblk, rho * nblk, False)
            return carry

        def rest_step(n, carry, g=g, q_refs=q_refs, d=d, nblk=nblk):
            rho = n // (nblk - 1)
            attend(g, q_refs, d, nblk, n + rho + 1, True)
            return carry

        lax.fori_loop(0, d * nblk, prep_step, 0, unroll=8)
        lax.fori_loop(0, d, first_step, 0, unroll=min(d, 8))
        if nblk > 1:
            n_rest = d * (nblk - 1)
            lax.fori_loop(0, n_rest, rest_step, 0, unroll=n_rest // 2 if n_rest % 2 == 0 else n_rest)


def _attn_prompt(q, kvs, bias_rest, bias_first):
    bsz, t, _ = q.shape
    assert t % (QB * DILATIONS[-1]) == 0
    wq = Q_PER_KV * HEAD_DIM
    heads_per_block = LANES // HEAD_DIM
    q_specs = [pl.BlockSpec((1, t, LANES), lambda b, h, g=g, p=p: (b, 0, (g * KV_HEADS + h) * N_PAIR + p))
               for g in range(N_GROUPS) for p in range(N_PAIR)]
    k_spec = pl.BlockSpec((1, t, LANES), lambda b, h: (b, 0, h // heads_per_block))
    v_spec = pl.BlockSpec((1, t, LANES), lambda b, h: (b, 0, KV_HEADS // heads_per_block + h // heads_per_block))
    bias_specs = [pl.BlockSpec((N_GROUPS, N_PAIR * QB, tab.shape[-1]), lambda b, h: (0, h, 0))
                  for tab in (bias_rest, bias_first)]
    slots = (N_GROUPS - 1) * N_PAIR
    return pl.pallas_call(
        _attn_prompt_body,
        grid=(bsz, KV_HEADS),
        in_specs=q_specs + [k_spec] * N_GROUPS + [v_spec] * N_GROUPS + bias_specs,
        out_specs=pl.BlockSpec((1, t, wq), lambda b, h: (b, 0, h)),
        out_shape=jax.ShapeDtypeStruct((bsz, t, KV_HEADS * wq), F32),
        scratch_shapes=[pltpu.VMEM((slots, t, LANES), F32)] * 3 + [pltpu.VMEM((t, LANES), BF)] * 4,
        compiler_params=_params("parallel", "arbitrary"),
        name="attn_prompt",
    )(*([q] * len(q_specs)), *kvs, *kvs, bias_rest, bias_first)


def _merge_groups(o_parts, lse_parts):
    mx = jnp.maximum(jnp.maximum(lse_parts[0], lse_parts[1]), lse_parts[2])
    es = [jnp.exp(l - mx) for l in lse_parts]
    num = es[0] * o_parts[0] + es[1] * o_parts[1] + es[2] * o_parts[2]
    return num / (es[0] + es[1] + es[2])


def _attn_sample_body(qp_ref, n0, n1, n2, c0, c1, c2, bc0, bc1, bc2, bn_ref, o_ref, w0, w1, w2):
    n_new = n0.shape[1]
    half = KV_HEADS * HEAD_DIM
    lane = lax.broadcasted_iota(jnp.int32, (1, half), 1)
    tail_lane = lax.broadcasted_iota(jnp.int32, (1, LANES), 1)
    o_parts, lse_parts = [], []
    for g, (new_ref, c_ref, bc_ref, w_ref) in enumerate(((n0, c0, bc0, w0), (n1, c1, bc1, w1), (n2, c2, bc2, w2))):
        w = c_ref.shape[2]
        new = new_ref[0]
        newt = jnp.concatenate([new, jnp.zeros((LANES - n_new, 2 * half), F32)], axis=0).T
        ct = c_ref[0]
        shifted = pltpu.roll(ct, w - n_new, axis=1)
        tail = jnp.where(tail_lane >= LANES - n_new, pltpu.roll(newt, LANES - n_new, axis=1), shifted[:, w - LANES:])
        w_ref[0] = tail if w == LANES else jnp.concatenate([shifted[:, :w - LANES], tail], axis=1)
        kc, vc = ct[:half].astype(BF), ct[half:].astype(BF)
        kn, vn = newt[:half].astype(BF), newt[half:].astype(BF)
        o_g = jnp.zeros((Q_PER_KV * n_new, half), F32)
        l_g = jnp.zeros((Q_PER_KV * n_new, half), F32)
        for h in range(KV_HEADS):
            qp = qp_ref[0, g, h]
            s_c = _dot(qp, kc) + bc_ref[h]
            s_n = _dot(qp, kn) + bn_ref[g, h]
            m = jnp.maximum(jnp.max(s_c, axis=-1, keepdims=True), jnp.max(s_n, axis=-1, keepdims=True))
            p_c = jnp.exp(s_c - m)
            p_n = jnp.exp(s_n - m)
            l = jnp.sum(p_c, axis=-1, keepdims=True) + jnp.sum(p_n, axis=-1, keepdims=True)
            o = (_dot_nt(p_c.astype(BF), vc) + _dot_nt(p_n.astype(BF), vn)) / l
            mine = (lane >= h * HEAD_DIM) & (lane < (h + 1) * HEAD_DIM)
            o_g = jnp.where(mine, o, o_g)
            l_g = jnp.where(mine, m + jnp.log(l), l_g)
        o_parts.append(o_g)
        lse_parts.append(l_g)
    o_ref[0] = _merge_groups(o_parts, lse_parts)


def _attn_sample(qpad, new_kvs, caches, bias_c, bias_n):
    bsz = qpad.shape[0]
    n_new = new_kvs[0].shape[1]
    rows = Q_PER_KV * n_new
    half = KV_HEADS * HEAD_DIM
    per_b = lambda shape: pl.BlockSpec((1,) + shape[1:], lambda b: (b,) + (0,) * (len(shape) - 1))
    cache_specs = [per_b(c.shape) for c in caches]
    return pl.pallas_call(
        _attn_sample_body,
        grid=(bsz,),
        in_specs=[per_b(qpad.shape)] + [per_b(a.shape) for a in new_kvs] + cache_specs
                 + [_const_spec(b.shape) for b in bias_c] + [_const_spec(bias_n.shape)],
        out_specs=[per_b((bsz, rows, half))] + cache_specs,
        out_shape=[jax.ShapeDtypeStruct((bsz, rows, half), F32)]
                  + [jax.ShapeDtypeStruct(c.shape, F32) for c in caches],
        compiler_params=_params("parallel"),
        name="attn_sample",
    )(qpad, *new_kvs, *caches, *bias_c, bias_n)


def _t5_buckets(dist):
    d = np.asarray(dist)
    large = MAX_EXACT + (np.log(np.maximum(d, 1) / MAX_EXACT) / np.log(MAX_DISTANCE / MAX_EXACT)
                         * (N_BUCKETS - MAX_EXACT)).astype(np.int64)
    large = np.minimum(large, N_BUCKETS - 1)
    return np.where(d < MAX_EXACT, d, large).astype(np.int32)


def _group_bias(rel_bias, g):
    bk = _t5_buckets(DILATIONS[g] * np.arange(N_KEYS))
    return rel_bias[bk][:, g * HEADS_PER_GROUP:(g + 1) * HEADS_PER_GROUP].T.astype(F32)


def _prompt_bias_tables(rel_bias):
    period = 2 * QB
    rest, first = [], []
    for g in range(N_GROUPS):
        bv = _group_bias(rel_bias, g)
        row0 = jnp.concatenate([bv[:, ::-1], jnp.full((HEADS_PER_GROUP, period - N_KEYS), NEG, F32)], axis=1)
        wrap = jnp.concatenate([row0, row0, row0[:, :1]], axis=1)
        skew = jnp.broadcast_to(wrap[:, None, :], (HEADS_PER_GROUP, QB, 2 * period + 1))
        skew = skew.reshape(HEADS_PER_GROUP, -1)[:, :QB * 2 * period].reshape(HEADS_PER_GROUP, QB, 2 * period)
        tab = skew[:, :, period:]
        for out, part in ((rest, tab), (first, tab[:, :, QB:])):
            nk = part.shape[-1]
            pairs = part.reshape(HEADS_PER_GROUP // 2, 2, QB, nk).transpose(0, 2, 1, 3)
            out.append(pairs.reshape(HEADS_PER_GROUP // 2 * QB, 2 * nk))
    return jnp.stack(rest, axis=0), jnp.stack(first, axis=0)


def _sample_bias_tables(rel_bias, n_new):
    tabs_c, tabs_n = [], []
    for g in range(N_GROUPS):
        w, d = WINDOWS[g], DILATIONS[g]
        bv = _group_bias(rel_bias, g)
        fill = jnp.full(bv.shape, NEG, F32)
        dil = jnp.stack([bv] + [fill] * (d - 1), axis=-1).reshape(HEADS_PER_GROUP, N_KEYS * d)[:, :w + 1]
        ext = jnp.concatenate([jnp.full((HEADS_PER_GROUP, LANES - 1), NEG, F32), dil,
                               jnp.full((HEADS_PER_GROUP, n_new - 1), NEG, F32)], axis=1)
        rev = ext[:, ::-1]
        rows = [rev[:, n_new - 1 - r:n_new - 1 - r + w + LANES] for r in range(n_new)]
        tab = jnp.stack(rows, axis=1).reshape(KV_HEADS, Q_PER_KV * n_new, w + LANES)
        tabs_c.append(tab[:, :, :w])
        tabs_n.append(tab[:, :, w:])
    return tabs_c, jnp.stack(tabs_n, axis=0)


def _prep_weights(w_in_a, w_a2, b_a, w_o_a, w_kv, w_q_b, w_o_b, w_gate_up, w_down):
    n_main = 2 * GLA_DK + 2 * GLA_DV
    w_in = w_in_a[0]
    wa = jnp.pad(w_in[:, n_main:], ((0, 0), (0, LANES - GATE_RANK)))
    wa2 = jnp.pad(w_a2[0], ((0, LANES - GATE_RANK), (0, 0)))
    return dict(
        wm=w_in[:, :n_main].astype(BF), wa=wa.astype(BF), wa2=wa2.astype(BF), ba=b_a[0][None, :],
        wo_a=w_o_a[0].astype(BF), wkv=w_kv.astype(BF),
        wq=(w_q_b[0] * HEAD_DIM ** -0.5).astype(BF),
        wo_b=w_o_b[0].astype(BF),
        wgu=[w_gate_up[l].astype(BF) for l in range(2)], wd=[w_down[l].astype(BF) for l in range(2)])


def _from_positions_last(a):
    return jnp.transpose(a.reshape(a.shape[0], 2, KV_HEADS, HEAD_DIM, a.shape[-1]), (0, 4, 1, 2, 3))


def _layer0(x, s0, norm_g, g_onorm, wts):
    bsz, t, _ = x.shape
    x2 = x.reshape(bsz * t, D_MODEL)
    q, k, v, r, g = _gla_in(x2, norm_g[0], wts["wm"], wts["wa"], wts["wa2"], wts["ba"])
    sh = lambda a: a.reshape(bsz, t, a.shape[-1])
    o, st = _gla(sh(q), sh(k), sh(v), sh(g), s0)
    h = _mix_ffn([o.reshape(bsz * t, GLA_DV), r], x2, g_onorm, wts["wo_a"], norm_g[0], wts["wgu"][0],
                 wts["wd"][0])
    return h, st


def kernel(x_prompt, x_sample, state_gla, cache_win1, cache_win2, cache_win3, norm_g, w_in_a, w_a2, b_a,
           g_onorm, w_o_a, g_kv, w_kv, w_q_b, w_o_b, rel_bias, w_gate_up, w_down):
    wts = _prep_weights(w_in_a, w_a2, b_a, w_o_a, w_kv, w_q_b, w_o_b, w_gate_up, w_down)
    gkv = g_kv[None, :]

    bp, tp, _ = x_prompt.shape
    s0p = jnp.zeros((bp, GLA_HEADS, DKH, DVH), F32)
    h_p, gla_p = _layer0(x_prompt, s0p, norm_g, g_onorm, wts)
    *kv_p, q_p, kvt_last = _qkv(h_p, gkv, norm_g[1], wts["wkv"], wts["wq"], seq_len=tp)
    kv_p = [a.reshape(bp, tp, KV_WIDTH) for a in kv_p]
    o_p = _attn_prompt(q_p.reshape(bp, tp, -1), kv_p, *_prompt_bias_tables(rel_bias))
    y_p = _mix_ffn([o_p.reshape(bp * tp, D_MODEL)], h_p, None, wts["wo_b"], norm_g[1], wts["wgu"][1],
                   wts["wd"][1])
    win_p = [kv_p[g][:, tp - min(WINDOWS[g], tp):].reshape(bp, -1, 2, KV_HEADS, HEAD_DIM)
             for g in range(N_GROUPS - 1)]
    assert WINDOWS[-1] >= tp
    win_p.append(_from_positions_last(kvt_last))

    bs, ts, _ = x_sample.shape
    h_s, gla_s = _layer0(x_sample, state_gla[0], norm_g, g_onorm, wts)
    *kv_s, q_s = _qkv(h_s, gkv, norm_g[1], wts["wkv"], wts["wq"])
    q6 = q_s.reshape(bs, ts, N_GROUPS, KV_HEADS, Q_PER_KV, HEAD_DIM).transpose(0, 2, 3, 4, 1, 5)
    q6 = q6.reshape(bs, N_GROUPS, KV_HEADS, Q_PER_KV * ts, 1, HEAD_DIM)
    slot = jnp.eye(KV_HEADS, dtype=F32)[None, None, :, None, :, None]
    qpad = (q6 * slot).reshape(bs, N_GROUPS, KV_HEADS, Q_PER_KV * ts, KV_HEADS * HEAD_DIM).astype(BF)
    caches = [jnp.transpose(c, (0, 2, 3, 4, 1)).reshape(bs, KV_WIDTH, c.shape[1])
              for c in (cache_win1, cache_win2, cache_win3)]
    bias_c, bias_n = _sample_bias_tables(rel_bias, ts)
    o_s, *win_s = _attn_sample(qpad, [a.reshape(bs, ts, KV_WIDTH) for a in kv_s], caches, bias_c, bias_n)
    o_s = o_s.reshape(bs, Q_PER_KV, ts, KV_HEADS, HEAD_DIM).transpose(0, 2, 3, 1, 4).reshape(bs * ts, D_MODEL)
    y_s = _mix_ffn([o_s], h_s, None, wts["wo_b"], norm_g[1], wts["wgu"][1], wts["wd"][1])
    win_s = [_from_positions_last(w) for w in win_s]

    return (y_p.reshape(bp, tp, D_MODEL), y_s.reshape(bs, ts, D_MODEL), gla_p[None], win_p[0], win_p[1],
            win_p[2], gla_s[None], win_s[0], win_s[1], win_s[2])
```

```python
import functools

import numpy as np
import jax
import jax.numpy as jnp
from jax import lax
from jax.experimental import pallas as pl
from jax.experimental.pallas import tpu as pltpu

BF = jnp.bfloat16
F32 = jnp.float32

D_MODEL = 1024
GLA_HEADS = 4
GLA_DK = 512
GLA_DV = 1024
DKH = GLA_DK // GLA_HEADS
DVH = GLA_DV // GLA_HEADS
GATE_RANK = 16
GATE_TAU = 16.0
GLA_CHUNK = 64
GLA_SUB = 16
GLA_UNROLL = 4
WINDOWS = (128, 512, 2048)
DILATIONS = (1, 4, 16)
N_GROUPS = 3
HEAD_DIM = 64
HEADS_PER_GROUP = 16
KV_HEADS = 4
Q_PER_KV = 4
N_PAIR = Q_PER_KV // 2
N_KEYS = 129
N_BUCKETS = 32
MAX_EXACT = 16
MAX_DISTANCE = 2048
D_FF = 2816
FF_SPLITS = (0, 1536, D_FF)
EPS = 1e-6
NEG = -1e30
LOG2E = 1.4426950408889634
QB = 128
LANES = 128
KV_WIDTH = 2 * KV_HEADS * HEAD_DIM
VMEM_LIMIT_BYTES = 56 * 1024 * 1024


def _dot(a, b):
    return jnp.dot(a, b, preferred_element_type=F32)


def _dot_nt(a, b):
    return lax.dot_general(a, b, (((1,), (1,)), ((), ())), preferred_element_type=F32)


def _rms(x, g):
    return x * lax.rsqrt(jnp.mean(x * x, axis=-1, keepdims=True) + EPS) * g


def _sigmoid(x):
    return 1.0 / (1.0 + jnp.exp(-x))


def _const_spec(shape):
    nd = len(shape)
    return pl.BlockSpec(shape, lambda *_: (0,) * nd, pipeline_mode=pl.Buffered(1))


def _params(*sem):
    return pltpu.CompilerParams(dimension_semantics=sem, vmem_limit_bytes=VMEM_LIMIT_BYTES)


def _row_tile(n, want):
    tm = min(n, want)
    assert n % tm == 0
    return tm


def _gla_in_body(x_ref, ng_ref, wm_ref, wa_ref, wa2_ref, ba_ref, q_ref, k_ref, v_ref, r_ref, g_ref):
    xn = _rms(x_ref[...], ng_ref[0:1, :]).astype(BF)
    q_ref[...] = _dot(xn, wm_ref[:, 0:GLA_DK]) * (DKH ** -0.5)
    k_ref[...] = _dot(xn, wm_ref[:, GLA_DK:2 * GLA_DK])
    v_ref[...] = _dot(xn, wm_ref[:, 2 * GLA_DK:2 * GLA_DK + GLA_DV])
    r_ref[...] = _dot(xn, wm_ref[:, 2 * GLA_DK + GLA_DV:])
    a = _dot(xn, wa_ref[...]).astype(BF)
    z = _dot(a, wa2_ref[...]) + ba_ref[...]
    g_ref[...] = (jnp.minimum(z, 0.0) - jnp.log(1.0 + jnp.exp(-jnp.abs(z)))) * (1.0 / GATE_TAU)


def _gla_in(x2, ng, wm, wa, wa2, ba):
    n = x2.shape[0]
    tm = _row_tile(n, 512)
    row = lambda w: pl.BlockSpec((tm, w), lambda i: (i, 0))
    return pl.pallas_call(
        _gla_in_body,
        grid=(n // tm,),
        in_specs=[row(D_MODEL), _const_spec(ng.shape), _const_spec(wm.shape), _const_spec(wa.shape),
                  _const_spec(wa2.shape), _const_spec(ba.shape)],
        out_specs=[row(GLA_DK), row(GLA_DK), row(GLA_DV), row(GLA_DV), row(GLA_DK)],
        out_shape=[jax.ShapeDtypeStruct((n, w), F32) for w in (GLA_DK, GLA_DK, GLA_DV, GLA_DV, GLA_DK)],
        compiler_params=_params("parallel"),
        name="gla_in",
    )(x2, ng, wm, wa, wa2, ba)


def _gla_body(c_real, q_ref, k_ref, v_ref, g_ref, s0_ref, o_ref, st_ref, s_scr):
    t = pl.program_id(1)
    c = GLA_CHUNK
    nsb = c // GLA_SUB
    n_chunks = q_ref.shape[1] // c_real

    @pl.when(t == 0)
    def _():
        s_scr[...] = s0_ref[0]

    gc = min(n_chunks, GLA_UNROLL)
    assert n_chunks % gc == 0
    ri = lax.broadcasted_iota(jnp.int32, (gc * c, gc * c), 0)
    ci = lax.broadcasted_iota(jnp.int32, (gc * c, gc * c), 1)
    tril_bf = jnp.where((ri >= ci) & (ri // c == ci // c), 1.0, 0.0).astype(BF)
    ri2 = lax.broadcasted_iota(jnp.int32, (c, LANES), 0)
    ci2 = lax.broadcasted_iota(jnp.int32, (c, LANES), 1)
    causal = ri2 >= ci2

    def pad_rows(a, rows):
        if a.shape[0] == rows:
            return a
        return jnp.concatenate([a, jnp.zeros((rows - a.shape[0], a.shape[1]), a.dtype)], axis=0)

    heads = range(GLA_HEADS)
    ks = [slice(h * DKH, (h + 1) * DKH) for h in heads]
    vs = [slice(h * DVH, (h + 1) * DVH) for h in heads]
    units = [(j, h) for j in range(gc) for h in heads]

    def group(idx, carry):
        span = gc * c_real
        grows = pl.ds(pl.multiple_of(idx * span, span), span)
        rows = [pl.ds(pl.multiple_of(idx * span + j * c_real, c_real), c_real) for j in range(gc)]
        g_all = pad_rows(g_ref[0, grows, :], gc * c)
        g_hi = g_all.astype(BF)
        g_lo = (g_all - g_hi.astype(F32)).astype(BF)
        b_all = _dot(tril_bf, g_hi) + _dot(tril_bf, g_lo)
        b = {(j, h): b_all[j * c:(j + 1) * c, ks[h]] for j, h in units}
        qh = {(j, h): pad_rows(q_ref[0, rows[j], ks[h]], c) for j, h in units}
        kh = {(j, h): pad_rows(k_ref[0, rows[j], ks[h]], c) for j, h in units}
        vh = {(j, h): pad_rows(v_ref[0, rows[j], vs[h]], LANES) for j, h in units}
        b_last = {u: b[u][c - 1:c, :] for u in units}
        scores = {}
        for u in units:
            qparts, kparts = [], []
            for sbi in range(nsb):
                lo, hi = sbi * GLA_SUB, (sbi + 1) * GLA_SUB
                ref_row = b[u][lo:lo + 1, :]
                qj = (qh[u][lo:] * jnp.exp(b[u][lo:] - ref_row)).astype(BF)
                qparts.append(jnp.concatenate([jnp.zeros((lo, DKH), BF), qj], axis=0) if lo else qj)
                kj = (kh[u][lo:hi] * jnp.exp(ref_row - b[u][lo:hi])).astype(BF)
                pieces = []
                if lo > 0:
                    pieces.append(jnp.zeros((lo, DKH), BF))
                pieces.append(kj)
                pieces.append(jnp.zeros((LANES - hi, DKH), BF))
                kparts.append(jnp.concatenate(pieces, axis=0))
            qcat = jnp.concatenate(qparts, axis=1)
            kcat = jnp.concatenate(kparts, axis=1)
            scores[u] = _dot_nt(qcat, kcat)
        upd, decay = {}, {}
        for u in units:
            k2 = pad_rows(kh[u] * jnp.exp(b_last[u] - b[u]), LANES)
            upd[u] = _dot(k2.T.astype(BF), vh[u].astype(BF))
            col = jnp.broadcast_to(jnp.exp(b_last[u]), (LANES, DKH)).T
            decay[u] = jnp.concatenate([col] * (DVH // LANES), axis=1)
        q_in = {u: (qh[u] * jnp.exp(b[u])).astype(BF) for u in units}
        o_intra = {u: _dot(jnp.where(causal, scores[u], 0.0).astype(BF), vh[u].astype(BF)) for u in units}
        st = [s_scr[h] for h in heads]
        for j in range(gc):
            for h in heads:
                o = _dot(q_in[j, h], st[h].astype(BF)) + o_intra[j, h]
                o_ref[0, rows[j], vs[h]] = o[0:c_real]
            st = [st[h] * decay[j, h] + upd[j, h] for h in heads]
        for h in heads:
            s_scr[h] = st[h]
        return carry

    lax.fori_loop(0, n_chunks // gc, group, 0)

    @pl.when(t == pl.num_programs(1) - 1)
    def _():
        st_ref[0] = s_scr[...]


def _gla(q, k, v, g, s0):
    bsz, t, _ = q.shape
    c_real = min(t, GLA_CHUNK)
    tc = min(t, 512)
    assert t % tc == 0 and tc % c_real == 0
    blk = lambda w: pl.BlockSpec((1, tc, w), lambda b, i: (b, i, 0))
    st_spec = pl.BlockSpec((1, GLA_HEADS, DKH, DVH), lambda b, i: (b, 0, 0, 0))
    return pl.pallas_call(
        functools.partial(_gla_body, c_real),
        grid=(bsz, t // tc),
        in_specs=[blk(GLA_DK), blk(GLA_DK), blk(GLA_DV), blk(GLA_DK), st_spec],
        out_specs=[blk(GLA_DV), st_spec],
        out_shape=[jax.ShapeDtypeStruct((bsz, t, GLA_DV), F32),
                   jax.ShapeDtypeStruct((bsz, GLA_HEADS, DKH, DVH), F32)],
        scratch_shapes=[pltpu.VMEM((GLA_HEADS, DKH, DVH), F32)],
        compiler_params=_params("parallel", "arbitrary"),
        name="gla",
    )(q, k, v, g, s0)


def _mix_ffn_body(gated, *refs):
    if gated:
        o_ref, r_ref, x_ref, gon_ref, wo_ref, ng_ref, wgu_ref, wd_ref, out_ref = refs
        o = o_ref[...]
        r = r_ref[...]
        on = jnp.concatenate(
            [_rms(o[:, h * DVH:(h + 1) * DVH], gon_ref[...]) for h in range(GLA_HEADS)], axis=1)
        m = on * (r * _sigmoid(r))
    else:
        m_ref, x_ref, wo_ref, ng_ref, wgu_ref, wd_ref, out_ref = refs
        m = m_ref[...]
    y = _dot(m.astype(BF), wo_ref[...])
    h1 = x_ref[...] + _rms(y, ng_ref[1:2, :])
    u = _rms(h1, ng_ref[2:3, :]).astype(BF)
    f = None
    for lo, hi in zip(FF_SPLITS[:-1], FF_SPLITS[1:]):
        gate = _dot(u, wgu_ref[:, lo:hi])
        up = _dot(u, wgu_ref[:, D_FF + lo:D_FF + hi])
        part = _dot((gate * _sigmoid(gate) * up).astype(BF), wd_ref[lo:hi, :])
        f = part if f is None else f + part
    out_ref[...] = h1 + _rms(f, ng_ref[3:4, :])


def _mix_ffn(mix_inputs, x2, gon, wo, ng, wgu, wd):
    n = x2.shape[0]
    tm = _row_tile(n, 512)
    row = pl.BlockSpec((tm, D_MODEL), lambda i: (i, 0))
    gated = gon is not None
    args = list(mix_inputs) + [x2] + ([gon] if gated else []) + [wo, ng, wgu, wd]
    in_specs = [row] * (len(mix_inputs) + 1) + [_const_spec(a.shape) for a in args[len(mix_inputs) + 1:]]
    return pl.pallas_call(
        functools.partial(_mix_ffn_body, gated),
        grid=(n // tm,),
        in_specs=in_specs,
        out_specs=row,
        out_shape=jax.ShapeDtypeStruct((n, D_MODEL), F32),
        compiler_params=_params("parallel"),
        name="mix_ffn_gated" if gated else "mix_ffn",
    )(*args)


def _qkv_body(h_ref, gkv_ref, ng_ref, wkv_ref, wq_ref, kv0_ref, kv1_ref, kv2_ref, q_ref, *kvt_refs):
    h = h_ref[...]
    hn = h * lax.rsqrt(jnp.mean(h * h, axis=-1, keepdims=True) + EPS)
    hkv = (hn * gkv_ref[...]).astype(BF)
    for g, kv_ref in enumerate((kv0_ref, kv1_ref, kv2_ref)):
        kv = _dot(hkv, wkv_ref[:, g * KV_WIDTH:(g + 1) * KV_WIDTH])
        kv_ref[...] = kv
        if kvt_refs and g == N_GROUPS - 1:
            kvt_refs[0][0] = kv.T
    q_ref[...] = _dot((hn * ng_ref[0:1, :]).astype(BF), wq_ref[...])


def _qkv(h2, gkv, ng, wkv, wq, seq_len=None):
    n = h2.shape[0]
    tm = _row_tile(n, 512)
    row = lambda w: pl.BlockSpec((tm, w), lambda i: (i, 0))
    nq = wq.shape[1]
    out_specs = [row(KV_WIDTH)] * N_GROUPS + [row(nq)]
    out_shape = [jax.ShapeDtypeStruct((n, KV_WIDTH), F32)] * N_GROUPS + [jax.ShapeDtypeStruct((n, nq), F32)]
    if seq_len is not None:
        assert seq_len % tm == 0
        per_seq = seq_len // tm
        out_specs.append(pl.BlockSpec((1, KV_WIDTH, tm), lambda i: (i // per_seq, 0, i % per_seq)))
        out_shape.append(jax.ShapeDtypeStruct((n // seq_len, KV_WIDTH, seq_len), F32))
    return pl.pallas_call(
        _qkv_body,
        grid=(n // tm,),
        in_specs=[row(D_MODEL), _const_spec(gkv.shape), _const_spec(ng.shape), _const_spec(wkv.shape),
                  _const_spec(wq.shape)],
        out_specs=out_specs,
        out_shape=out_shape,
        compiler_params=_params("parallel"),
        name="qkv",
    )(h2, gkv, ng, wkv, wq)


def _attn_prompt_body(q00, q01, q10, q11, q20, q21, k0, k1, k2, v0, v1, v2, bias_rest_ref, bias_first_ref,
                      o_ref, o_scr, lse_scr, k_scr, v_scr):
    t_len = k0.shape[1]
    par = pl.program_id(1) % 2
    lane = lax.broadcasted_iota(jnp.int32, (1, LANES), 1)
    low = lane < HEAD_DIM
    same = (lane >= HEAD_DIM).astype(jnp.int32) == par

    def strided_rows(start, d):
        if d > 1:
            return pl.ds(start, QB, stride=d)
        return pl.ds(start if isinstance(start, int) else pl.multiple_of(start, QB), QB)

    def block_rows(blk, n=1):
        return pl.ds(pl.multiple_of(blk * QB, QB), n * QB)

    def prepare(g, k_ref, v_ref, d, nblk, blk):
        rho = blk // nblk
        rows = strided_rows(rho + d * QB * (blk - rho * nblk), d)
        dst = block_rows(blk)
        for src_ref, dst_scr in ((k_ref, k_scr), (v_ref, v_scr)):
            own = jnp.where(same, src_ref[0, rows, :], 0.0)
            dst_scr[par, dst, :] = own.astype(BF)
            dst_scr[1 - par, dst, :] = pltpu.roll(own, HEAD_DIM, axis=1).astype(BF)

    def attend(g, q_refs, d, nblk, blk, with_prev):
        rho = blk // nblk
        rows_q = strided_rows(rho + d * QB * (blk - rho * nblk), d)
        keys = block_rows(blk - 1, 2) if with_prev else block_rows(blk)
        nk = 2 * QB if with_prev else QB
        q = jnp.concatenate([q_refs[p][0, rows_q, :] for p in range(N_PAIR)], axis=0).astype(BF)
        k_cat = jnp.concatenate([k_scr[0, keys, :], k_scr[1, keys, :]], axis=0)
        bias = bias_rest_ref[g] if with_prev else bias_first_ref[g]
        s = _dot_nt(q, k_cat) + bias
        mx, ps = [], []
        for odd in range(2):
            sh = s[:, odd * nk:(odd + 1) * nk]
            mx.append(jnp.max(sh, axis=-1, keepdims=True))
            ps.append(jnp.exp2(sh - mx[odd]).astype(BF))
        ones_lo = jnp.broadcast_to(jnp.where(low, 1.0, 0.0), (nk, LANES)).astype(BF)
        ones_hi = jnp.broadcast_to(jnp.where(low, 0.0, 1.0), (nk, LANES)).astype(BF)
        rhs = jnp.concatenate([jnp.concatenate([v_scr[0, keys, :], ones_lo], axis=1),
                               jnp.concatenate([v_scr[1, keys, :], ones_hi], axis=1)], axis=0)
        res = _dot(jnp.concatenate(ps, axis=1), rhs)
        m = jnp.where(low, mx[0], mx[1])
        for p in range(N_PAIR):
            part = slice(p * QB, (p + 1) * QB)
            o_g, l_g, m_g = res[part, :LANES], res[part, LANES:], m[part]
            if g > 0:
                slot = N_PAIR * (g - 1) + p
                o_scr[slot, rows_q, :] = o_g / l_g
                lse_scr[slot, rows_q, :] = m_g + jnp.log2(l_g)
            else:
                slots = [N_PAIR * other + p for other in range(N_GROUPS - 1)]
                lses = [lse_scr[s_, rows_q, :] for s_ in slots]
                top = jnp.maximum(jnp.maximum(m_g, lses[0]), lses[1])
                e_own = jnp.exp2(m_g - top)
                es = [jnp.exp2(ll - top) for ll in lses]
                num = e_own * o_g + es[0] * o_scr[slots[0], rows_q, :] + es[1] * o_scr[slots[1], rows_q, :]
                den = e_own * l_g + es[0] + es[1]
                o_ref[0, rows_q, p * LANES:(p + 1) * LANES] = num / den

    groups = (((q00, q01), k0, v0), ((q10, q11), k1, v1), ((q20, q21), k2, v2))
    for g in reversed(range(N_GROUPS)):
        q_refs, k_ref, v_ref = groups[g]
        d = DILATIONS[g]
        nblk = t_len // d // QB

        def prep_step(blk, carry, g=g, k_ref=k_ref, v_ref=v_ref, d=d, nblk=nblk):
            prepare(g, k_ref, v_ref, d, nblk, blk)
            return carry

        def first_step(rho, carry, g=g, q_refs=q_refs, d=d, nblk=nblk):
            attend(g, q_refs, d, nblk, rho * nblk, False)
            return carry

        def rest_step(n, carry, g=g, q_refs=q_refs, d=d, nblk=nblk):
            rho = n // (nblk - 1)
            attend(g, q_refs, d, nblk, n + rho + 1, True)
            return carry

        lax.fori_loop(0, d * nblk, prep_step, 0, unroll=8)
        lax.fori_loop(0, d, first_step, 0, unroll=min(d, 8))
        if nblk > 1:
            n_rest = d * (nblk - 1)
            lax.fori_loop(0, n_rest, rest_step, 0, unroll=n_rest // 2 if n_rest % 2 == 0 else n_rest)


def _attn_prompt(q, kvs, bias_rest, bias_first):
    bsz, t, _ = q.shape
    assert t % (QB * DILATIONS[-1]) == 0
    wq = Q_PER_KV * HEAD_DIM
    heads_per_block = LANES // HEAD_DIM
    q_specs = [pl.BlockSpec((1, t, LANES), lambda b, h, g=g, p=p: (b, 0, (g * KV_HEADS + h) * N_PAIR + p))
               for g in range(N_GROUPS) for p in range(N_PAIR)]
    k_spec = pl.BlockSpec((1, t, LANES), lambda b, h: (b, 0, h // heads_per_block))
    v_spec = pl.BlockSpec((1, t, LANES), lambda b, h: (b, 0, KV_HEADS // heads_per_block + h // heads_per_block))
    bias_specs = [pl.BlockSpec((N_GROUPS, N_PAIR * QB, tab.shape[-1]), lambda b, h: (0, h, 0))
                  for tab in (bias_rest, bias_first)]
    slots = (N_GROUPS - 1) * N_PAIR
    return pl.pallas_call(
        _attn_prompt_body,
        grid=(bsz, KV_HEADS),
        in_specs=q_specs + [k_spec] * N_GROUPS + [v_spec] * N_GROUPS + bias_specs,
        out_specs=pl.BlockSpec((1, t, wq), lambda b, h: (b, 0, h)),
        out_shape=jax.ShapeDtypeStruct((bsz, t, KV_HEADS * wq), F32),
        scratch_shapes=[pltpu.VMEM((slots, t, LANES), F32)] * 2 + [pltpu.VMEM((2, t, LANES), BF)] * 2,
        compiler_params=_params("parallel", "arbitrary"),
        name="attn_prompt",
    )(*([q] * len(q_specs)), *kvs, *kvs, bias_rest, bias_first)


def _merge_groups(o_parts, lse_parts):
    mx = jnp.maximum(jnp.maximum(lse_parts[0], lse_parts[1]), lse_parts[2])
    es = [jnp.exp2(l - mx) for l in lse_parts]
    num = es[0] * o_parts[0] + es[1] * o_parts[1] + es[2] * o_parts[2]
    return num / (es[0] + es[1] + es[2])


def _attn_sample_body(qp_ref, n0, n1, n2, c0, c1, c2, bc0, bc1, bc2, bn_ref, o_ref, w0, w1, w2):
    n_new = n0.shape[1]
    half = KV_HEADS * HEAD_DIM
    lane = lax.broadcasted_iota(jnp.int32, (1, half), 1)
    tail_lane = lax.broadcasted_iota(jnp.int32, (1, LANES), 1)
    o_parts, lse_parts = [], []
    for g, (new_ref, c_ref, bc_ref, w_ref) in enumerate(((n0, c0, bc0, w0), (n1, c1, bc1, w1), (n2, c2, bc2, w2))):
        w = c_ref.shape[2]
        new = new_ref[0]
        newt = jnp.concatenate([new, jnp.zeros((LANES - n_new, 2 * half), F32)], axis=0).T
        ct = c_ref[0]
        shifted = pltpu.roll(ct, w - n_new, axis=1)
        tail = jnp.where(tail_lane >= LANES - n_new, pltpu.roll(newt, LANES - n_new, axis=1), shifted[:, w - LANES:])
        w_ref[0] = tail if w == LANES else jnp.concatenate([shifted[:, :w - LANES], tail], axis=1)
        kc, vc = ct[:half].astype(BF), ct[half:].astype(BF)
        kn, vn = newt[:half].astype(BF), newt[half:].astype(BF)
        o_g = jnp.zeros((Q_PER_KV * n_new, half), F32)
        l_g = jnp.zeros((Q_PER_KV * n_new, half), F32)
        for h in range(KV_HEADS):
            qp = qp_ref[0, g, h]
            s_c = _dot(qp, kc) + bc_ref[h]
            s_n = _dot(qp, kn) + bn_ref[g, h]
            m = jnp.maximum(jnp.max(s_c, axis=-1, keepdims=True), jnp.max(s_n, axis=-1, keepdims=True))
            p_c = jnp.exp2(s_c - m)
            p_n = jnp.exp2(s_n - m)
            l = jnp.sum(p_c, axis=-1, keepdims=True) + jnp.sum(p_n, axis=-1, keepdims=True)
            o = (_dot_nt(p_c.astype(BF), vc) + _dot_nt(p_n.astype(BF), vn)) / l
            mine = (lane >= h * HEAD_DIM) & (lane < (h + 1) * HEAD_DIM)
            o_g = jnp.where(mine, o, o_g)
            l_g = jnp.where(mine, m + jnp.log2(l), l_g)
        o_parts.append(o_g)
        lse_parts.append(l_g)
    o_ref[0] = _merge_groups(o_parts, lse_parts)


def _attn_sample(qpad, new_kvs, caches, bias_c, bias_n):
    bsz = qpad.shape[0]
    n_new = new_kvs[0].shape[1]
    rows = Q_PER_KV * n_new
    half = KV_HEADS * HEAD_DIM
    per_b = lambda shape: pl.BlockSpec((1,) + shape[1:], lambda b: (b,) + (0,) * (len(shape) - 1))
    cache_specs = [per_b(c.shape) for c in caches]
    return pl.pallas_call(
        _attn_sample_body,
        grid=(bsz,),
        in_specs=[per_b(qpad.shape)] + [per_b(a.shape) for a in new_kvs] + cache_specs
                 + [_const_spec(b.shape) for b in bias_c] + [_const_spec(bias_n.shape)],
        out_specs=[per_b((bsz, rows, half))] + cache_specs,
        out_shape=[jax.ShapeDtypeStruct((bsz, rows, half), F32)]
                  + [jax.ShapeDtypeStruct(c.shape, F32) for c in caches],
        compiler_params=_params("parallel"),
        name="attn_sample",
    )(qpad, *new_kvs, *caches, *bias_c, bias_n)


def _t5_buckets(dist):
    d = np.asarray(dist)
    large = MAX_EXACT + (np.log(np.maximum(d, 1) / MAX_EXACT) / np.log(MAX_DISTANCE / MAX_EXACT)
                         * (N_BUCKETS - MAX_EXACT)).astype(np.int64)
    large = np.minimum(large, N_BUCKETS - 1)
    return np.where(d < MAX_EXACT, d, large).astype(np.int32)


def _group_bias(rel_bias, g):
    bk = _t5_buckets(DILATIONS[g] * np.arange(N_KEYS))
    return rel_bias[bk][:, g * HEADS_PER_GROUP:(g + 1) * HEADS_PER_GROUP].T.astype(F32) * LOG2E


def _prompt_bias_tables(rel_bias):
    period = 2 * QB
    rest, first = [], []
    for g in range(N_GROUPS):
        bv = _group_bias(rel_bias, g)
        row0 = jnp.concatenate([bv[:, ::-1], jnp.full((HEADS_PER_GROUP, period - N_KEYS), NEG, F32)], axis=1)
        wrap = jnp.concatenate([row0, row0, row0[:, :1]], axis=1)
        skew = jnp.broadcast_to(wrap[:, None, :], (HEADS_PER_GROUP, QB, 2 * period + 1))
        skew = skew.reshape(HEADS_PER_GROUP, -1)[:, :QB * 2 * period].reshape(HEADS_PER_GROUP, QB, 2 * period)
        tab = skew[:, :, period:]
        for out, part in ((rest, tab), (first, tab[:, :, QB:])):
            nk = part.shape[-1]
            pairs = part.reshape(HEADS_PER_GROUP // 2, 2, QB, nk).transpose(0, 2, 1, 3)
            out.append(pairs.reshape(HEADS_PER_GROUP // 2 * QB, 2 * nk))
    return jnp.stack(rest, axis=0), jnp.stack(first, axis=0)


def _sample_bias_tables(rel_bias, n_new):
    tabs_c, tabs_n = [], []
    for g in range(N_GROUPS):
        w, d = WINDOWS[g], DILATIONS[g]
        bv = _group_bias(rel_bias, g)
        fill = jnp.full(bv.shape, NEG, F32)
        dil = jnp.stack([bv] + [fill] * (d - 1), axis=-1).reshape(HEADS_PER_GROUP, N_KEYS * d)[:, :w + 1]
        ext = jnp.concatenate([jnp.full((HEADS_PER_GROUP, LANES - 1), NEG, F32), dil,
                               jnp.full((HEADS_PER_GROUP, n_new - 1), NEG, F32)], axis=1)
        rev = ext[:, ::-1]
        rows = [rev[:, n_new - 1 - r:n_new - 1 - r + w + LANES] for r in range(n_new)]
        tab = jnp.stack(rows, axis=1).reshape(KV_HEADS, Q_PER_KV * n_new, w + LANES)
        tabs_c.append(tab[:, :, :w])
        tabs_n.append(tab[:, :, w:])
    return tabs_c, jnp.stack(tabs_n, axis=0)


def _prep_weights(w_in_a, w_a2, b_a, w_o_a, w_kv, w_q_b, w_o_b, w_gate_up, w_down):
    n_main = 2 * GLA_DK + 2 * GLA_DV
    w_in = w_in_a[0]
    wa = jnp.pad(w_in[:, n_main:], ((0, 0), (0, LANES - GATE_RANK)))
    wa2 = jnp.pad(w_a2[0], ((0, LANES - GATE_RANK), (0, 0)))
    return dict(
        wm=w_in[:, :n_main].astype(BF), wa=wa.astype(BF), wa2=wa2.astype(BF), ba=b_a[0][None, :],
        wo_a=w_o_a[0].astype(BF), wkv=w_kv.astype(BF),
        wq=(w_q_b[0] * (HEAD_DIM ** -0.5 * LOG2E)).astype(BF),
        wo_b=w_o_b[0].astype(BF),
        wgu=[w_gate_up[l].astype(BF) for l in range(2)], wd=[w_down[l].astype(BF) for l in range(2)])


def _from_positions_last(a):
    return jnp.transpose(a.reshape(a.shape[0], 2, KV_HEADS, HEAD_DIM, a.shape[-1]), (0, 4, 1, 2, 3))


def _layer0(x, s0, norm_g, g_onorm, wts):
    bsz, t, _ = x.shape
    x2 = x.reshape(bsz * t, D_MODEL)
    q, k, v, r, g = _gla_in(x2, norm_g[0], wts["wm"], wts["wa"], wts["wa2"], wts["ba"])
    sh = lambda a: a.reshape(bsz, t, a.shape[-1])
    o, st = _gla(sh(q), sh(k), sh(v), sh(g), s0)
    h = _mix_ffn([o.reshape(bsz * t, GLA_DV), r], x2, g_onorm, wts["wo_a"], norm_g[0], wts["wgu"][0],
                 wts["wd"][0])
    return h, st


def kernel(x_prompt, x_sample, state_gla, cache_win1, cache_win2, cache_win3, norm_g, w_in_a, w_a2, b_a,
           g_onorm, w_o_a, g_kv, w_kv, w_q_b, w_o_b, rel_bias, w_gate_up, w_down):
    wts = _prep_weights(w_in_a, w_a2, b_a, w_o_a, w_kv, w_q_b, w_o_b, w_gate_up, w_down)
    gkv = g_kv[None, :]

    bp, tp, _ = x_prompt.shape
    s0p = jnp.zeros((bp, GLA_HEADS, DKH, DVH), F32)
    h_p, gla_p = _layer0(x_prompt, s0p, norm_g, g_onorm, wts)
    *kv_p, q_p, kvt_last = _qkv(h_p, gkv, norm_g[1], wts["wkv"], wts["wq"], seq_len=tp)
    kv_p = [a.reshape(bp, tp, KV_WIDTH) for a in kv_p]
    o_p = _attn_prompt(q_p.reshape(bp, tp, -1), kv_p, *_prompt_bias_tables(rel_bias))
    y_p = _mix_ffn([o_p.reshape(bp * tp, D_MODEL)], h_p, None, wts["wo_b"], norm_g[1], wts["wgu"][1],
                   wts["wd"][1])
    win_p = [kv_p[g][:, tp - min(WINDOWS[g], tp):].reshape(bp, -1, 2, KV_HEADS, HEAD_DIM)
             for g in range(N_GROUPS - 1)]
    assert WINDOWS[-1] >= tp
    win_p.append(_from_positions_last(kvt_last))

    bs, ts, _ = x_sample.shape
    h_s, gla_s = _layer0(x_sample, state_gla[0], norm_g, g_onorm, wts)
    *kv_s, q_s = _qkv(h_s, gkv, norm_g[1], wts["wkv"], wts["wq"])
    q6 = q_s.reshape(bs, ts, N_GROUPS, KV_HEADS, Q_PER_KV, HEAD_DIM).transpose(0, 2, 3, 4, 1, 5)
    q6 = q6.reshape(bs, N_GROUPS, KV_HEADS, Q_PER_KV * ts, 1, HEAD_DIM)
    slot = jnp.eye(KV_HEADS, dtype=F32)[None, None, :, None, :, None]
    qpad = (q6 * slot).reshape(bs, N_GROUPS, KV_HEADS, Q_PER_KV * ts, KV_HEADS * HEAD_DIM).astype(BF)
    caches = [jnp.transpose(c, (0, 2, 3, 4, 1)).reshape(bs, KV_WIDTH, c.shape[1])
              for c in (cache_win1, cache_win2, cache_win3)]
    bias_c, bias_n = _sample_bias_tables(rel_bias, ts)
    o_s, *win_s = _attn_sample(qpad, [a.reshape(bs, ts, KV_WIDTH) for a in kv_s], caches, bias_c, bias_n)
    o_s = o_s.reshape(bs, Q_PER_KV, ts, KV_HEADS, HEAD_DIM).transpose(0, 2, 3, 1, 4).reshape(bs * ts, D_MODEL)
    y_s = _mix_ffn([o_s], h_s, None, wts["wo_b"], norm_g[1], wts["wgu"][1], wts["wd"][1])
    win_s = [_from_positions_last(w) for w in win_s]

    return (y_p.reshape(bp, tp, D_MODEL), y_s.reshape(bs, ts, D_MODEL), gla_p[None], win_p[0], win_p[1],
            win_p[2], gla_s[None], win_s[0], win_s[1], win_s[2])
```

```python
import functools

import numpy as np
import jax
import jax.numpy as jnp
from jax import lax
from jax.experimental import pallas as pl
from jax.experimental.pallas import tpu as pltpu

BF = jnp.bfloat16
F32 = jnp.float32

D_MODEL = 1024
GLA_HEADS = 4
GLA_DK = 512
GLA_DV = 1024
DKH = GLA_DK // GLA_HEADS
DVH = GLA_DV // GLA_HEADS
GATE_RANK = 16
GATE_TAU = 16.0
GLA_CHUNK = 64
GLA_SUB = 16
GLA_UNROLL = 4
WINDOWS = (128, 512, 2048)
DILATIONS = (1, 4, 16)
N_GROUPS = 3
HEAD_DIM = 64
HEADS_PER_GROUP = 16
KV_HEADS = 4
Q_PER_KV = 4
N_PAIR = Q_PER_KV // 2
N_KEYS = 129
N_BUCKETS = 32
MAX_EXACT = 16
MAX_DISTANCE = 2048
D_FF = 2816
FF_SPLITS = (0, 1536, D_FF)
EPS = 1e-6
NEG = -1e30
LOG2E = 1.4426950408889634
QB = 128
LANES = 128
ROW_ALIGN = 16
KV_WIDTH = 2 * KV_HEADS * HEAD_DIM
VMEM_LIMIT_BYTES = 56 * 1024 * 1024


def _dot(a, b):
    return jnp.dot(a, b, preferred_element_type=F32)


def _dot_nt(a, b):
    return lax.dot_general(a, b, (((1,), (1,)), ((), ())), preferred_element_type=F32)


def _rms(x, g):
    return x * lax.rsqrt(jnp.mean(x * x, axis=-1, keepdims=True) + EPS) * g


def _sigmoid(x):
    return 1.0 / (1.0 + jnp.exp(-x))


def _const_spec(shape):
    nd = len(shape)
    return pl.BlockSpec(shape, lambda *_: (0,) * nd, pipeline_mode=pl.Buffered(1))


def _params(*sem):
    return pltpu.CompilerParams(dimension_semantics=sem, vmem_limit_bytes=VMEM_LIMIT_BYTES)


def _row_tile(n, want):
    tm = min(n, want)
    assert n % tm == 0
    return tm


def _row_halves(tm):
    n_sub = 2 if tm % (2 * ROW_ALIGN) == 0 else 1
    return [slice(i * tm // n_sub, (i + 1) * tm // n_sub) for i in range(n_sub)]


def _gla_in_body(x_ref, ng_ref, wm_ref, wa_ref, wa2_ref, ba_ref, q_ref, k_ref, v_ref, r_ref, g_ref):
    subs = _row_halves(x_ref.shape[0])
    xn = [_rms(x_ref[rows, :], ng_ref[0:1, :]).astype(BF) for rows in subs]
    a = [_dot(xi, wa_ref[...]).astype(BF) for xi in xn]
    z = [_dot(ai, wa2_ref[...]) + ba_ref[...] for ai in a]
    for rows, zi in zip(subs, z):
        g_ref[rows, :] = (jnp.minimum(zi, 0.0) - jnp.log(1.0 + jnp.exp(-jnp.abs(zi)))) * (1.0 / GATE_TAU)
    for rows, xi in zip(subs, xn):
        q_ref[rows, :] = _dot(xi, wm_ref[:, 0:GLA_DK]) * (DKH ** -0.5)
        k_ref[rows, :] = _dot(xi, wm_ref[:, GLA_DK:2 * GLA_DK])
        v_ref[rows, :] = _dot(xi, wm_ref[:, 2 * GLA_DK:2 * GLA_DK + GLA_DV])
        r_ref[rows, :] = _dot(xi, wm_ref[:, 2 * GLA_DK + GLA_DV:])


def _gla_in(x2, ng, wm, wa, wa2, ba):
    n = x2.shape[0]
    tm = _row_tile(n, 512)
    row = lambda w: pl.BlockSpec((tm, w), lambda i: (i, 0))
    return pl.pallas_call(
        _gla_in_body,
        grid=(n // tm,),
        in_specs=[row(D_MODEL), _const_spec(ng.shape), _const_spec(wm.shape), _const_spec(wa.shape),
                  _const_spec(wa2.shape), _const_spec(ba.shape)],
        out_specs=[row(GLA_DK), row(GLA_DK), row(GLA_DV), row(GLA_DV), row(GLA_DK)],
        out_shape=[jax.ShapeDtypeStruct((n, w), F32) for w in (GLA_DK, GLA_DK, GLA_DV, GLA_DV, GLA_DK)],
        compiler_params=_params("parallel"),
        name="gla_in",
    )(x2, ng, wm, wa, wa2, ba)


def _gla_body(c_real, q_ref, k_ref, v_ref, g_ref, s0_ref, o_ref, st_ref, s_scr):
    t = pl.program_id(1)
    c = GLA_CHUNK
    nsb = c // GLA_SUB
    n_chunks = q_ref.shape[1] // c_real

    @pl.when(t == 0)
    def _():
        s_scr[...] = s0_ref[0]

    gc = min(n_chunks, GLA_UNROLL)
    assert n_chunks % gc == 0
    ri = lax.broadcasted_iota(jnp.int32, (gc * c, gc * c), 0)
    ci = lax.broadcasted_iota(jnp.int32, (gc * c, gc * c), 1)
    tril_bf = jnp.where((ri >= ci) & (ri // c == ci // c), 1.0, 0.0).astype(BF)
    ri2 = lax.broadcasted_iota(jnp.int32, (c, LANES), 0)
    ci2 = lax.broadcasted_iota(jnp.int32, (c, LANES), 1)
    causal = ri2 >= ci2

    def pad_rows(a, rows):
        if a.shape[0] == rows:
            return a
        return jnp.concatenate([a, jnp.zeros((rows - a.shape[0], a.shape[1]), a.dtype)], axis=0)

    heads = range(GLA_HEADS)
    ks = [slice(h * DKH, (h + 1) * DKH) for h in heads]
    vs = [slice(h * DVH, (h + 1) * DVH) for h in heads]
    units = [(j, h) for j in range(gc) for h in heads]

    def group(idx, carry):
        span = gc * c_real
        grows = pl.ds(pl.multiple_of(idx * span, span), span)
        rows = [pl.ds(pl.multiple_of(idx * span + j * c_real, c_real), c_real) for j in range(gc)]
        g_all = pad_rows(g_ref[0, grows, :], gc * c)
        g_hi = g_all.astype(BF)
        g_lo = (g_all - g_hi.astype(F32)).astype(BF)
        b_all = _dot(tril_bf, g_hi) + _dot(tril_bf, g_lo)
        b = {(j, h): b_all[j * c:(j + 1) * c, ks[h]] for j, h in units}
        qh = {(j, h): pad_rows(q_ref[0, rows[j], ks[h]], c) for j, h in units}
        kh = {(j, h): pad_rows(k_ref[0, rows[j], ks[h]], c) for j, h in units}
        vh = {(j, h): pad_rows(v_ref[0, rows[j], vs[h]], LANES) for j, h in units}
        b_last = {u: b[u][c - 1:c, :] for u in units}
        scores = {}
        for u in units:
            qparts, kparts = [], []
            for sbi in range(nsb):
                lo, hi = sbi * GLA_SUB, (sbi + 1) * GLA_SUB
                ref_row = b[u][lo:lo + 1, :]
                qj = (qh[u][lo:] * jnp.exp(b[u][lo:] - ref_row)).astype(BF)
                qparts.append(jnp.concatenate([jnp.zeros((lo, DKH), BF), qj], axis=0) if lo else qj)
                kj = (kh[u][lo:hi] * jnp.exp(ref_row - b[u][lo:hi])).astype(BF)
                pieces = []
                if lo > 0:
                    pieces.append(jnp.zeros((lo, DKH), BF))
                pieces.append(kj)
                pieces.append(jnp.zeros((LANES - hi, DKH), BF))
                kparts.append(jnp.concatenate(pieces, axis=0))
            qcat = jnp.concatenate(qparts, axis=1)
            kcat = jnp.concatenate(kparts, axis=1)
            scores[u] = _dot_nt(qcat, kcat)
        upd, decay = {}, {}
        for u in units:
            k2 = pad_rows(kh[u] * jnp.exp(b_last[u] - b[u]), LANES)
            upd[u] = _dot(k2.T.astype(BF), vh[u].astype(BF))
            col = jnp.broadcast_to(jnp.exp(b_last[u]), (LANES, DKH)).T
            decay[u] = jnp.concatenate([col] * (DVH // LANES), axis=1)
        lhs = {u: jnp.concatenate([(qh[u] * jnp.exp(b[u])).astype(BF),
                                   jnp.where(causal, scores[u], 0.0).astype(BF)], axis=1) for u in units}
        v_bf = {u: vh[u].astype(BF) for u in units}
        st = [s_scr[h] for h in heads]
        for j in range(gc):
            for h in heads:
                o = _dot(lhs[j, h], jnp.concatenate([st[h].astype(BF), v_bf[j, h]], axis=0))
                o_ref[0, rows[j], vs[h]] = o[0:c_real]
            st = [st[h] * decay[j, h] + upd[j, h] for h in heads]
        for h in heads:
            s_scr[h] = st[h]
        return carry

    lax.fori_loop(0, n_chunks // gc, group, 0)

    @pl.when(t == pl.num_programs(1) - 1)
    def _():
        st_ref[0] = s_scr[...]


def _gla(q, k, v, g, s0):
    bsz, t, _ = q.shape
    c_real = min(t, GLA_CHUNK)
    tc = min(t, 512)
    assert t % tc == 0 and tc % c_real == 0
    blk = lambda w: pl.BlockSpec((1, tc, w), lambda b, i: (b, i, 0))
    st_spec = pl.BlockSpec((1, GLA_HEADS, DKH, DVH), lambda b, i: (b, 0, 0, 0))
    return pl.pallas_call(
        functools.partial(_gla_body, c_real),
        grid=(bsz, t // tc),
        in_specs=[blk(GLA_DK), blk(GLA_DK), blk(GLA_DV), blk(GLA_DK), st_spec],
        out_specs=[blk(GLA_DV), st_spec],
        out_shape=[jax.ShapeDtypeStruct((bsz, t, GLA_DV), F32),
                   jax.ShapeDtypeStruct((bsz, GLA_HEADS, DKH, DVH), F32)],
        scratch_shapes=[pltpu.VMEM((GLA_HEADS, DKH, DVH), F32)],
        compiler_params=_params("parallel", "arbitrary"),
        name="gla",
    )(q, k, v, g, s0)


def _mix_ffn_body(gated, *refs):
    if gated:
        o_ref, r_ref, x_ref, gon_ref, wo_ref, ng_ref, wgu_ref, wd_ref, out_ref = refs
    else:
        m_ref, x_ref, wo_ref, ng_ref, wgu_ref, wd_ref, out_ref = refs
    subs = _row_halves(x_ref.shape[0])
    n_sub = len(subs)
    if gated:
        def mixed(rows):
            o = o_ref[rows, :]
            r = r_ref[rows, :]
            on = jnp.concatenate(
                [_rms(o[:, h * DVH:(h + 1) * DVH], gon_ref[...]) for h in range(GLA_HEADS)], axis=1)
            return on * (r * _sigmoid(r))
        m = [mixed(rows) for rows in subs]
    else:
        m = [m_ref[rows, :] for rows in subs]
    y = [_dot(mi.astype(BF), wo_ref[...]) for mi in m]
    h1 = [x_ref[rows, :] + _rms(yi, ng_ref[1:2, :]) for rows, yi in zip(subs, y)]
    u = [_rms(hi, ng_ref[2:3, :]).astype(BF) for hi in h1]
    f = [None] * n_sub
    for lo, hi in zip(FF_SPLITS[:-1], FF_SPLITS[1:]):
        gate = [_dot(ui, wgu_ref[:, lo:hi]) for ui in u]
        up = [_dot(ui, wgu_ref[:, D_FF + lo:D_FF + hi]) for ui in u]
        part = [_dot((gi * _sigmoid(gi) * pi).astype(BF), wd_ref[lo:hi, :]) for gi, pi in zip(gate, up)]
        f = [pi if fi is None else fi + pi for fi, pi in zip(f, part)]
    for rows, hi, fi in zip(subs, h1, f):
        out_ref[rows, :] = hi + _rms(fi, ng_ref[3:4, :])


def _mix_ffn(mix_inputs, x2, gon, wo, ng, wgu, wd):
    n = x2.shape[0]
    tm = _row_tile(n, 512)
    row = pl.BlockSpec((tm, D_MODEL), lambda i: (i, 0))
    gated = gon is not None
    args = list(mix_inputs) + [x2] + ([gon] if gated else []) + [wo, ng, wgu, wd]
    in_specs = [row] * (len(mix_inputs) + 1) + [_const_spec(a.shape) for a in args[len(mix_inputs) + 1:]]
    return pl.pallas_call(
        functools.partial(_mix_ffn_body, gated),
        grid=(n // tm,),
        in_specs=in_specs,
        out_specs=row,
        out_shape=jax.ShapeDtypeStruct((n, D_MODEL), F32),
        compiler_params=_params("parallel"),
        name="mix_ffn_gated" if gated else "mix_ffn",
    )(*args)


def _qkv_body(h_ref, gkv_ref, ng_ref, wkv_ref, wq_ref, kv0_ref, kv1_ref, kv2_ref, q_ref, *kvt_refs):
    subs = _row_halves(h_ref.shape[0])
    hn = []
    for rows in subs:
        h = h_ref[rows, :]
        hn.append(h * lax.rsqrt(jnp.mean(h * h, axis=-1, keepdims=True) + EPS))
    hkv = [(hi * gkv_ref[...]).astype(BF) for hi in hn]
    hq = [(hi * ng_ref[0:1, :]).astype(BF) for hi in hn]
    for rows, hi in zip(subs, hkv):
        for g, kv_ref in enumerate((kv0_ref, kv1_ref, kv2_ref)):
            kv = _dot(hi, wkv_ref[:, g * KV_WIDTH:(g + 1) * KV_WIDTH])
            kv_ref[rows, :] = kv
            if kvt_refs and g == N_GROUPS - 1:
                kvt_refs[0][0, :, rows] = kv.T
    for rows, hi in zip(subs, hq):
        q_ref[rows, :] = _dot(hi, wq_ref[...])


def _qkv(h2, gkv, ng, wkv, wq, seq_len=None):
    n = h2.shape[0]
    tm = _row_tile(n, 512)
    row = lambda w: pl.BlockSpec((tm, w), lambda i: (i, 0))
    nq = wq.shape[1]
    out_specs = [row(KV_WIDTH)] * N_GROUPS + [row(nq)]
    out_shape = [jax.ShapeDtypeStruct((n, KV_WIDTH), F32)] * N_GROUPS + [jax.ShapeDtypeStruct((n, nq), F32)]
    if seq_len is not None:
        assert seq_len % tm == 0
        per_seq = seq_len // tm
        out_specs.append(pl.BlockSpec((1, KV_WIDTH, tm), lambda i: (i // per_seq, 0, i % per_seq)))
        out_shape.append(jax.ShapeDtypeStruct((n // seq_len, KV_WIDTH, seq_len), F32))
    return pl.pallas_call(
        _qkv_body,
        grid=(n // tm,),
        in_specs=[row(D_MODEL), _const_spec(gkv.shape), _const_spec(ng.shape), _const_spec(wkv.shape),
                  _const_spec(wq.shape)],
        out_specs=out_specs,
        out_shape=out_shape,
        compiler_params=_params("parallel"),
        name="qkv",
    )(h2, gkv, ng, wkv, wq)


def _attn_prompt_body(q00, q01, q10, q11, q20, q21, k0, k1, k2, v0, v1, v2, bias_rest_ref, bias_first_ref,
                      o_ref, o_scr, lse_scr, k_scr, v_scr):
    t_len = k0.shape[1]
    par = pl.program_id(1) % 2
    lane = lax.broadcasted_iota(jnp.int32, (1, LANES), 1)
    low = lane < HEAD_DIM
    same = (lane >= HEAD_DIM).astype(jnp.int32) == par

    def strided_rows(start, d):
        if d > 1:
            return pl.ds(start, QB, stride=d)
        return pl.ds(start if isinstance(start, int) else pl.multiple_of(start, QB), QB)

    def block_rows(blk, n=1):
        return pl.ds(pl.multiple_of(blk * QB, QB), n * QB)

    def prepare(g, k_ref, v_ref, d, nblk, blk):
        rho = blk // nblk
        rows = strided_rows(rho + d * QB * (blk - rho * nblk), d)
        dst = block_rows(blk)
        for src_ref, dst_scr in ((k_ref, k_scr), (v_ref, v_scr)):
            own = jnp.where(same, src_ref[0, rows, :], 0.0)
            dst_scr[par, dst, :] = own.astype(BF)
            dst_scr[1 - par, dst, :] = pltpu.roll(own, HEAD_DIM, axis=1).astype(BF)

    def attend(g, q_refs, d, nblk, blk, with_prev):
        rho = blk // nblk
        rows_q = strided_rows(rho + d * QB * (blk - rho * nblk), d)
        keys = block_rows(blk - 1, 2) if with_prev else block_rows(blk)
        nk = 2 * QB if with_prev else QB
        q = jnp.concatenate([q_refs[p][0, rows_q, :] for p in range(N_PAIR)], axis=0).astype(BF)
        k_cat = jnp.concatenate([k_scr[0, keys, :], k_scr[1, keys, :]], axis=0)
        bias = bias_rest_ref[g] if with_prev else bias_first_ref[g]
        s = _dot_nt(q, k_cat) + bias
        mx, ps = [], []
        for odd in range(2):
            sh = s[:, odd * nk:(odd + 1) * nk]
            mx.append(jnp.max(sh, axis=-1, keepdims=True))
            ps.append(jnp.exp2(sh - mx[odd]).astype(BF))
        ones_lo = jnp.broadcast_to(jnp.where(low, 1.0, 0.0), (nk, LANES)).astype(BF)
        ones_hi = jnp.broadcast_to(jnp.where(low, 0.0, 1.0), (nk, LANES)).astype(BF)
        rhs = jnp.concatenate([jnp.concatenate([v_scr[0, keys, :], ones_lo], axis=1),
                               jnp.concatenate([v_scr[1, keys, :], ones_hi], axis=1)], axis=0)
        res = _dot(jnp.concatenate(ps, axis=1), rhs)
        m = jnp.where(low, mx[0], mx[1])
        for p in range(N_PAIR):
            part = slice(p * QB, (p + 1) * QB)
            o_g, l_g, m_g = res[part, :LANES], res[part, LANES:], m[part]
            if g > 0:
                slot = N_PAIR * (g - 1) + p
                o_scr[slot, rows_q, :] = o_g / l_g
                lse_scr[slot, rows_q, :] = m_g + jnp.log2(l_g)
            else:
                slots = [N_PAIR * other + p for other in range(N_GROUPS - 1)]
                lses = [lse_scr[s_, rows_q, :] for s_ in slots]
                top = jnp.maximum(jnp.maximum(m_g, lses[0]), lses[1])
                e_own = jnp.exp2(m_g - top)
                es = [jnp.exp2(ll - top) for ll in lses]
                num = e_own * o_g + es[0] * o_scr[slots[0], rows_q, :] + es[1] * o_scr[slots[1], rows_q, :]
                den = e_own * l_g + es[0] + es[1]
                o_ref[0, rows_q, p * LANES:(p + 1) * LANES] = num / den

    groups = (((q00, q01), k0, v0), ((q10, q11), k1, v1), ((q20, q21), k2, v2))
    for g in reversed(range(N_GROUPS)):
        q_refs, k_ref, v_ref = groups[g]
        d = DILATIONS[g]
        nblk = t_len // d // QB

        def prep_step(blk, carry, g=g, k_ref=k_ref, v_ref=v_ref, d=d, nblk=nblk):
            prepare(g, k_ref, v_ref, d, nblk, blk)
            return carry

        def first_step(rho, carry, g=g, q_refs=q_refs, d=d, nblk=nblk):
            attend(g, q_refs, d, nblk, rho * nblk, False)
            return carry

        def rest_step(n, carry, g=g, q_refs=q_refs, d=d, nblk=nblk):
            rho = n // (nblk - 1)
            attend(g, q_refs, d, nblk, n + rho + 1, True)
            return carry

        lax.fori_loop(0, d * nblk, prep_step, 0, unroll=8)
        lax.fori_loop(0, d, first_step, 0, unroll=min(d, 8))
        if nblk > 1:
            n_rest = d * (nblk - 1)
            lax.fori_loop(0, n_rest, rest_step, 0, unroll=n_rest // 2 if n_rest % 2 == 0 else n_rest)


def _attn_prompt(q, kvs, bias_rest, bias_first):
    bsz, t, _ = q.shape
    assert t % (QB * DILATIONS[-1]) == 0
    wq = Q_PER_KV * HEAD_DIM
    heads_per_block = LANES // HEAD_DIM
    q_specs = [pl.BlockSpec((1, t, LANES), lambda b, h, g=g, p=p: (b, 0, (g * KV_HEADS + h) * N_PAIR + p))
               for g in range(N_GROUPS) for p in range(N_PAIR)]
    k_spec = pl.BlockSpec((1, t, LANES), lambda b, h: (b, 0, h // heads_per_block))
    v_spec = pl.BlockSpec((1, t, LANES), lambda b, h: (b, 0, KV_HEADS // heads_per_block + h // heads_per_block))
    bias_specs = [pl.BlockSpec((N_GROUPS, N_PAIR * QB, tab.shape[-1]), lambda b, h: (0, h, 0))
                  for tab in (bias_rest, bias_first)]
    slots = (N_GROUPS - 1) * N_PAIR
    return pl.pallas_call(
        _attn_prompt_body,
        grid=(bsz, KV_HEADS),
        in_specs=q_specs + [k_spec] * N_GROUPS + [v_spec] * N_GROUPS + bias_specs,
        out_specs=pl.BlockSpec((1, t, wq), lambda b, h: (b, 0, h)),
        out_shape=jax.ShapeDtypeStruct((bsz, t, KV_HEADS * wq), F32),
        scratch_shapes=[pltpu.VMEM((slots, t, LANES), F32)] * 2 + [pltpu.VMEM((2, t, LANES), BF)] * 2,
        compiler_params=_params("parallel", "arbitrary"),
        name="attn_prompt",
    )(*([q] * len(q_specs)), *kvs, *kvs, bias_rest, bias_first)


def _merge_groups(o_parts, lse_parts):
    mx = jnp.maximum(jnp.maximum(lse_parts[0], lse_parts[1]), lse_parts[2])
    es = [jnp.exp2(l - mx) for l in lse_parts]
    num = es[0] * o_parts[0] + es[1] * o_parts[1] + es[2] * o_parts[2]
    return num / (es[0] + es[1] + es[2])


def _attn_sample_body(qp_ref, n0, n1, n2, c0, c1, c2, bc0, bc1, bc2, bn_ref, o_ref, w0, w1, w2):
    n_new = n0.shape[1]
    half = KV_HEADS * HEAD_DIM
    lane = lax.broadcasted_iota(jnp.int32, (1, half), 1)
    tail_lane = lax.broadcasted_iota(jnp.int32, (1, LANES), 1)
    o_parts, lse_parts = [], []
    for g, (new_ref, c_ref, bc_ref, w_ref) in enumerate(((n0, c0, bc0, w0), (n1, c1, bc1, w1), (n2, c2, bc2, w2))):
        w = c_ref.shape[2]
        new = new_ref[0]
        newt = jnp.concatenate([new, jnp.zeros((LANES - n_new, 2 * half), F32)], axis=0).T
        ct = c_ref[0]
        shifted = pltpu.roll(ct, w - n_new, axis=1)
        tail = jnp.where(tail_lane >= LANES - n_new, pltpu.roll(newt, LANES - n_new, axis=1), shifted[:, w - LANES:])
        w_ref[0] = tail if w == LANES else jnp.concatenate([shifted[:, :w - LANES], tail], axis=1)
        kc, vc = ct[:half].astype(BF), ct[half:].astype(BF)
        kn, vn = newt[:half].astype(BF), newt[half:].astype(BF)
        o_g = jnp.zeros((Q_PER_KV * n_new, half), F32)
        l_g = jnp.zeros((Q_PER_KV * n_new, half), F32)
        for h in range(KV_HEADS):
            qp = qp_ref[0, g, h]
            s_c = _dot(qp, kc) + bc_ref[h]
            s_n = _dot(qp, kn) + bn_ref[g, h]
            m = jnp.maximum(jnp.max(s_c, axis=-1, keepdims=True), jnp.max(s_n, axis=-1, keepdims=True))
            p_c = jnp.exp2(s_c - m)
            p_n = jnp.exp2(s_n - m)
            l = jnp.sum(p_c, axis=-1, keepdims=True) + jnp.sum(p_n, axis=-1, keepdims=True)
            o = (_dot_nt(p_c.astype(BF), vc) + _dot_nt(p_n.astype(BF), vn)) / l
            mine = (lane >= h * HEAD_DIM) & (lane < (h + 1) * HEAD_DIM)
            o_g = jnp.where(mine, o, o_g)
            l_g = jnp.where(mine, m + jnp.log2(l), l_g)
        o_parts.append(o_g)
        lse_parts.append(l_g)
    o_ref[0] = _merge_groups(o_parts, lse_parts)


def _attn_sample(qpad, new_kvs, caches, bias_c, bias_n):
    bsz = qpad.shape[0]
    n_new = new_kvs[0].shape[1]
    rows = Q_PER_KV * n_new
    half = KV_HEADS * HEAD_DIM
    per_b = lambda shape: pl.BlockSpec((1,) + shape[1:], lambda b: (b,) + (0,) * (len(shape) - 1))
    cache_specs = [per_b(c.shape) for c in caches]
    return pl.pallas_call(
        _attn_sample_body,
        grid=(bsz,),
        in_specs=[per_b(qpad.shape)] + [per_b(a.shape) for a in new_kvs] + cache_specs
                 + [_const_spec(b.shape) for b in bias_c] + [_const_spec(bias_n.shape)],
        out_specs=[per_b((bsz, rows, half))] + cache_specs,
        out_shape=[jax.ShapeDtypeStruct((bsz, rows, half), F32)]
                  + [jax.ShapeDtypeStruct(c.shape, F32) for c in caches],
        compiler_params=_params("parallel"),
        name="attn_sample",
    )(qpad, *new_kvs, *caches, *bias_c, bias_n)


def _t5_buckets(dist):
    d = np.asarray(dist)
    large = MAX_EXACT + (np.log(np.maximum(d, 1) / MAX_EXACT) / np.log(MAX_DISTANCE / MAX_EXACT)
                         * (N_BUCKETS - MAX_EXACT)).astype(np.int64)
    large = np.minimum(large, N_BUCKETS - 1)
    return np.where(d < MAX_EXACT, d, large).astype(np.int32)


def _group_bias(rel_bias, g):
    bk = _t5_buckets(DILATIONS[g] * np.arange(N_KEYS))
    return rel_bias[bk][:, g * HEADS_PER_GROUP:(g + 1) * HEADS_PER_GROUP].T.astype(F32) * LOG2E


def _prompt_bias_tables(rel_bias):
    period = 2 * QB
    rest, first = [], []
    for g in range(N_GROUPS):
        bv = _group_bias(rel_bias, g)
        row0 = jnp.concatenate([bv[:, ::-1], jnp.full((HEADS_PER_GROUP, period - N_KEYS), NEG, F32)], axis=1)
        wrap = jnp.concatenate([row0, row0, row0[:, :1]], axis=1)
        skew = jnp.broadcast_to(wrap[:, None, :], (HEADS_PER_GROUP, QB, 2 * period + 1))
        skew = skew.reshape(HEADS_PER_GROUP, -1)[:, :QB * 2 * period].reshape(HEADS_PER_GROUP, QB, 2 * period)
        tab = skew[:, :, period:]
        for out, part in ((rest, tab), (first, tab[:, :, QB:])):
            nk = part.shape[-1]
            pairs = part.reshape(HEADS_PER_GROUP // 2, 2, QB, nk).transpose(0, 2, 1, 3)
            out.append(pairs.reshape(HEADS_PER_GROUP // 2 * QB, 2 * nk))
    return jnp.stack(rest, axis=0), jnp.stack(first, axis=0)


def _sample_bias_tables(rel_bias, n_new):
    tabs_c, tabs_n = [], []
    for g in range(N_GROUPS):
        w, d = WINDOWS[g], DILATIONS[g]
        bv = _group_bias(rel_bias, g)
        fill = jnp.full(bv.shape, NEG, F32)
        dil = jnp.stack([bv] + [fill] * (d - 1), axis=-1).reshape(HEADS_PER_GROUP, N_KEYS * d)[:, :w + 1]
        ext = jnp.concatenate([jnp.full((HEADS_PER_GROUP, LANES - 1), NEG, F32), dil,
                               jnp.full((HEADS_PER_GROUP, n_new - 1), NEG, F32)], axis=1)
        rev = ext[:, ::-1]
        rows = [rev[:, n_new - 1 - r:n_new - 1 - r + w + LANES] for r in range(n_new)]
        tab = jnp.stack(rows, axis=1).reshape(KV_HEADS, Q_PER_KV * n_new, w + LANES)
        tabs_c.append(tab[:, :, :w])
        tabs_n.append(tab[:, :, w:])
    return tabs_c, jnp.stack(tabs_n, axis=0)


def _prep_weights(w_in_a, w_a2, b_a, w_o_a, w_kv, w_q_b, w_o_b, w_gate_up, w_down):
    n_main = 2 * GLA_DK + 2 * GLA_DV
    w_in = w_in_a[0]
    wa = jnp.pad(w_in[:, n_main:], ((0, 0), (0, LANES - GATE_RANK)))
    wa2 = jnp.pad(w_a2[0], ((0, LANES - GATE_RANK), (0, 0)))
    return dict(
        wm=w_in[:, :n_main].astype(BF), wa=wa.astype(BF), wa2=wa2.astype(BF), ba=b_a[0][None, :],
        wo_a=w_o_a[0].astype(BF), wkv=w_kv.astype(BF),
        wq=(w_q_b[0] * (HEAD_DIM ** -0.5 * LOG2E)).astype(BF),
        wo_b=w_o_b[0].astype(BF),
        wgu=[w_gate_up[l].astype(BF) for l in range(2)], wd=[w_down[l].astype(BF) for l in range(2)])


def _from_positions_last(a):
    return jnp.transpose(a.reshape(a.shape[0], 2, KV_HEADS, HEAD_DIM, a.shape[-1]), (0, 4, 1, 2, 3))


def _layer0(x, s0, norm_g, g_onorm, wts):
    bsz, t, _ = x.shape
    x2 = x.reshape(bsz * t, D_MODEL)
    q, k, v, r, g = _gla_in(x2, norm_g[0], wts["wm"], wts["wa"], wts["wa2"], wts["ba"])
    sh = lambda a: a.reshape(bsz, t, a.shape[-1])
    o, st = _gla(sh(q), sh(k), sh(v), sh(g), s0)
    h = _mix_ffn([o.reshape(bsz * t, GLA_DV), r], x2, g_onorm, wts["wo_a"], norm_g[0], wts["wgu"][0],
                 wts["wd"][0])
    return h, st


def kernel(x_prompt, x_sample, state_gla, cache_win1, cache_win2, cache_win3, norm_g, w_in_a, w_a2, b_a,
           g_onorm, w_o_a, g_kv, w_kv, w_q_b, w_o_b, rel_bias, w_gate_up, w_down):
    wts = _prep_weights(w_in_a, w_a2, b_a, w_o_a, w_kv, w_q_b, w_o_b, w_gate_up, w_down)
    gkv = g_kv[None, :]

    bp, tp, _ = x_prompt.shape
    s0p = jnp.zeros((bp, GLA_HEADS, DKH, DVH), F32)
    h_p, gla_p = _layer0(x_prompt, s0p, norm_g, g_onorm, wts)
    *kv_p, q_p, kvt_last = _qkv(h_p, gkv, norm_g[1], wts["wkv"], wts["wq"], seq_len=tp)
    kv_p = [a.reshape(bp, tp, KV_WIDTH) for a in kv_p]
    o_p = _attn_prompt(q_p.reshape(bp, tp, -1), kv_p, *_prompt_bias_tables(rel_bias))
    y_p = _mix_ffn([o_p.reshape(bp * tp, D_MODEL)], h_p, None, wts["wo_b"], norm_g[1], wts["wgu"][1],
                   wts["wd"][1])
    win_p = [kv_p[g][:, tp - min(WINDOWS[g], tp):].reshape(bp, -1, 2, KV_HEADS, HEAD_DIM)
             for g in range(N_GROUPS - 1)]
    assert WINDOWS[-1] >= tp
    win_p.append(_from_positions_last(kvt_last))

    bs, ts, _ = x_sample.shape
    h_s, gla_s = _layer0(x_sample, state_gla[0], norm_g, g_onorm, wts)
    *kv_s, q_s = _qkv(h_s, gkv, norm_g[1], wts["wkv"], wts["wq"])
    q6 = q_s.astype(BF).reshape(bs, ts, N_GROUPS, KV_HEADS, Q_PER_KV, HEAD_DIM).transpose(0, 2, 3, 4, 1, 5)
    q6 = q6.reshape(bs, N_GROUPS, KV_HEADS, Q_PER_KV * ts, HEAD_DIM)
    qpad = jnp.stack([jnp.pad(q6[:, :, h], ((0, 0), (0, 0), (0, 0), (h * HEAD_DIM, (KV_HEADS - 1 - h) * HEAD_DIM)))
                      for h in range(KV_HEADS)], axis=2)
    caches = [jnp.transpose(c, (0, 2, 3, 4, 1)).reshape(bs, KV_WIDTH, c.shape[1])
              for c in (cache_win1, cache_win2, cache_win3)]
    bias_c, bias_n = _sample_bias_tables(rel_bias, ts)
    o_s, *win_s = _attn_sample(qpad, [a.reshape(bs, ts, KV_WIDTH) for a in kv_s], caches, bias_c, bias_n)
    o_s = o_s.reshape(bs, Q_PER_KV, ts, KV_HEADS, HEAD_DIM).transpose(0, 2, 3, 1, 4).reshape(bs * ts, D_MODEL)
    y_s = _mix_ffn([o_s], h_s, None, wts["wo_b"], norm_g[1], wts["wgu"][1], wts["wd"][1])
    win_s = [_from_positions_last(w) for w in win_s]

    return (y_p.reshape(bp, tp, D_MODEL), y_s.reshape(bs, ts, D_MODEL), gla_p[None], win_p[0], win_p[1],
            win_p[2], gla_s[None], win_s[0], win_s[1], win_s[2])
```

```python
import functools

import numpy as np
import jax
import jax.numpy as jnp
from jax import lax
from jax.experimental import pallas as pl
from jax.experimental.pallas import tpu as pltpu

BF = jnp.bfloat16
F32 = jnp.float32

D_MODEL = 1024
GLA_HEADS = 4
GLA_DK = 512
GLA_DV = 1024
DKH = GLA_DK // GLA_HEADS
DVH = GLA_DV // GLA_HEADS
GATE_RANK = 16
GATE_TAU = 16.0
GLA_CHUNK = 64
GLA_SUB = 16
GLA_UNROLL = 4
WINDOWS = (128, 512, 2048)
DILATIONS = (1, 4, 16)
N_GROUPS = 3
HEAD_DIM = 64
HEADS_PER_GROUP = 16
KV_HEADS = 4
Q_PER_KV = 4
N_PAIR = Q_PER_KV // 2
N_KEYS = 129
N_BUCKETS = 32
MAX_EXACT = 16
MAX_DISTANCE = 2048
D_FF = 2816
FF_SPLITS = (0, 1536, D_FF)
EPS = 1e-6
NEG = -1e30
LOG2E = 1.4426950408889634
QB = 128
LANES = 128
ROW_ALIGN = 16
DEINT = 4
KV_WIDTH = 2 * KV_HEADS * HEAD_DIM
VMEM_LIMIT_BYTES = 56 * 1024 * 1024


def _dot(a, b):
    return jnp.dot(a, b, preferred_element_type=F32)


def _dot_nt(a, b):
    return lax.dot_general(a, b, (((1,), (1,)), ((), ())), preferred_element_type=F32)


def _rms(x, g):
    return x * lax.rsqrt(jnp.mean(x * x, axis=-1, keepdims=True) + EPS) * g


def _sigmoid(x):
    return 1.0 / (1.0 + jnp.exp(-x))


def _const_spec(shape):
    nd = len(shape)
    return pl.BlockSpec(shape, lambda *_: (0,) * nd, pipeline_mode=pl.Buffered(1))


def _params(*sem):
    return pltpu.CompilerParams(dimension_semantics=sem, vmem_limit_bytes=VMEM_LIMIT_BYTES)


def _row_tile(n, want):
    tm = min(n, want)
    assert n % tm == 0
    return tm


def _row_halves(tm):
    n_sub = 2 if tm % (2 * ROW_ALIGN) == 0 else 1
    return [slice(i * tm // n_sub, (i + 1) * tm // n_sub) for i in range(n_sub)]


def _gla_in_body(x_ref, ng_ref, wm_ref, wa_ref, wa2_ref, ba_ref, q_ref, k_ref, v_ref, r_ref, g_ref):
    subs = _row_halves(x_ref.shape[0])
    xn = [_rms(x_ref[rows, :], ng_ref[0:1, :]).astype(BF) for rows in subs]
    a = [_dot(xi, wa_ref[...]).astype(BF) for xi in xn]
    z = [_dot(ai, wa2_ref[...]) + ba_ref[...] for ai in a]
    for rows, zi in zip(subs, z):
        g_ref[rows, :] = (jnp.minimum(zi, 0.0) - jnp.log(1.0 + jnp.exp(-jnp.abs(zi)))) * (1.0 / GATE_TAU)
    for rows, xi in zip(subs, xn):
        q_ref[rows, :] = _dot(xi, wm_ref[:, 0:GLA_DK]) * (DKH ** -0.5)
        k_ref[rows, :] = _dot(xi, wm_ref[:, GLA_DK:2 * GLA_DK])
        v_ref[rows, :] = _dot(xi, wm_ref[:, 2 * GLA_DK:2 * GLA_DK + GLA_DV]).astype(v_ref.dtype)
        r_ref[rows, :] = _dot(xi, wm_ref[:, 2 * GLA_DK + GLA_DV:2 * GLA_DK + 2 * GLA_DV])


def _gla_in(x2, ng, wm, wa, wa2, ba, v_dtype):
    n = x2.shape[0]
    tm = _row_tile(n, 512)
    row = lambda w: pl.BlockSpec((tm, w), lambda i: (i, 0))
    return pl.pallas_call(
        _gla_in_body,
        grid=(n // tm,),
        in_specs=[row(D_MODEL), _const_spec(ng.shape), _const_spec(wm.shape), _const_spec(wa.shape),
                  _const_spec(wa2.shape), _const_spec(ba.shape)],
        out_specs=[row(GLA_DK), row(GLA_DK), row(GLA_DV), row(GLA_DV), row(GLA_DK)],
        out_shape=[jax.ShapeDtypeStruct((n, w), dt) for w, dt in
                   ((GLA_DK, F32), (GLA_DK, F32), (GLA_DV, v_dtype), (GLA_DV, F32), (GLA_DK, F32))],
        compiler_params=_params("parallel"),
        name="gla_in",
    )(x2, ng, wm, wa, wa2, ba)


def _gla_body(c_real, q_ref, k_ref, v_ref, g_ref, s0_ref, o_ref, st_ref, s_scr):
    t = pl.program_id(1)
    c = GLA_CHUNK
    nsb = c // GLA_SUB
    n_chunks = q_ref.shape[1] // c_real

    @pl.when(t == 0)
    def _():
        s_scr[...] = s0_ref[0]

    gc = min(n_chunks, GLA_UNROLL)
    assert n_chunks % gc == 0
    ri = lax.broadcasted_iota(jnp.int32, (gc * c, gc * c), 0)
    ci = lax.broadcasted_iota(jnp.int32, (gc * c, gc * c), 1)
    tril_bf = jnp.where((ri >= ci) & (ri // c == ci // c), 1.0, 0.0).astype(BF)
    ri2 = lax.broadcasted_iota(jnp.int32, (c, LANES), 0)
    ci2 = lax.broadcasted_iota(jnp.int32, (c, LANES), 1)
    causal = ri2 >= ci2

    def pad_rows(a, rows):
        if a.shape[0] == rows:
            return a
        return jnp.concatenate([a, jnp.zeros((rows - a.shape[0], a.shape[1]), a.dtype)], axis=0)

    heads = range(GLA_HEADS)
    ks = [slice(h * DKH, (h + 1) * DKH) for h in heads]
    vs = [slice(h * DVH, (h + 1) * DVH) for h in heads]
    units = [(j, h) for j in range(gc) for h in heads]

    def group(idx, carry):
        span = gc * c_real
        grows = pl.ds(pl.multiple_of(idx * span, span), span)
        rows = [pl.ds(pl.multiple_of(idx * span + j * c_real, c_real), c_real) for j in range(gc)]
        g_all = pad_rows(g_ref[0, grows, :], gc * c)
        g_hi = g_all.astype(BF)
        g_lo = (g_all - g_hi.astype(F32)).astype(BF)
        b_all = _dot(tril_bf, g_hi) + _dot(tril_bf, g_lo)
        b = {(j, h): b_all[j * c:(j + 1) * c, ks[h]] for j, h in units}
        qh = {(j, h): pad_rows(q_ref[0, rows[j], ks[h]], c) for j, h in units}
        kh = {(j, h): pad_rows(k_ref[0, rows[j], ks[h]], c) for j, h in units}
        vh = {(j, h): pad_rows(v_ref[0, rows[j], vs[h]], LANES) for j, h in units}
        b_last = {u: b[u][c - 1:c, :] for u in units}
        scores = {}
        for u in units:
            qparts, kparts = [], []
            for sbi in range(nsb):
                lo, hi = sbi * GLA_SUB, (sbi + 1) * GLA_SUB
                ref_row = b[u][lo:lo + 1, :]
                qj = (qh[u][lo:] * jnp.exp(b[u][lo:] - ref_row)).astype(BF)
                qparts.append(jnp.concatenate([jnp.zeros((lo, DKH), BF), qj], axis=0) if lo else qj)
                kj = (kh[u][lo:hi] * jnp.exp(ref_row - b[u][lo:hi])).astype(BF)
                pieces = []
                if lo > 0:
                    pieces.append(jnp.zeros((lo, DKH), BF))
                pieces.append(kj)
                pieces.append(jnp.zeros((LANES - hi, DKH), BF))
                kparts.append(jnp.concatenate(pieces, axis=0))
            qcat = jnp.concatenate(qparts, axis=1)
            kcat = jnp.concatenate(kparts, axis=1)
            scores[u] = _dot_nt(qcat, kcat)
        upd, decay = {}, {}
        for u in units:
            k2 = pad_rows(kh[u] * jnp.exp(b_last[u] - b[u]), LANES)
            upd[u] = _dot(k2.T.astype(BF), vh[u].astype(BF))
            col = jnp.broadcast_to(jnp.exp(b_last[u]), (LANES, DKH)).T
            decay[u] = jnp.concatenate([col] * (DVH // LANES), axis=1)
        lhs = {u: jnp.concatenate([(qh[u] * jnp.exp(b[u])).astype(BF),
                                   jnp.where(causal, scores[u], 0.0).astype(BF)], axis=1) for u in units}
        v_bf = {u: vh[u].astype(BF) for u in units}
        st = [s_scr[h] for h in heads]
        for j in range(gc):
            for h in heads:
                o = _dot(lhs[j, h], jnp.concatenate([st[h].astype(BF), v_bf[j, h]], axis=0))
                o_ref[0, rows[j], vs[h]] = o[0:c_real]
            st = [st[h] * decay[j, h] + upd[j, h] for h in heads]
        for h in heads:
            s_scr[h] = st[h]
        return carry

    lax.fori_loop(0, n_chunks // gc, group, 0)

    @pl.when(t == pl.num_programs(1) - 1)
    def _():
        st_ref[0] = s_scr[...]


def _gla(q, k, v, g, s0):
    bsz, t, _ = q.shape
    c_real = min(t, GLA_CHUNK)
    tc = min(t, 512)
    assert t % tc == 0 and tc % c_real == 0
    blk = lambda w: pl.BlockSpec((1, tc, w), lambda b, i: (b, i, 0))
    st_spec = pl.BlockSpec((1, GLA_HEADS, DKH, DVH), lambda b, i: (b, 0, 0, 0))
    return pl.pallas_call(
        functools.partial(_gla_body, c_real),
        grid=(bsz, t // tc),
        in_specs=[blk(GLA_DK), blk(GLA_DK), blk(GLA_DV), blk(GLA_DK), st_spec],
        out_specs=[blk(GLA_DV), st_spec],
        out_shape=[jax.ShapeDtypeStruct((bsz, t, GLA_DV), F32),
                   jax.ShapeDtypeStruct((bsz, GLA_HEADS, DKH, DVH), F32)],
        scratch_shapes=[pltpu.VMEM((GLA_HEADS, DKH, DVH), F32)],
        compiler_params=_params("parallel", "arbitrary"),
        name="gla",
    )(q, k, v, g, s0)


def _mix_ffn_body(gated, *refs):
    if gated:
        o_ref, r_ref, x_ref, gon_ref, wo_ref, ng_ref, wgu_ref, wd_ref, out_ref = refs
    else:
        m_ref, x_ref, wo_ref, ng_ref, wgu_ref, wd_ref, out_ref = refs
    subs = _row_halves(x_ref.shape[0])
    n_sub = len(subs)
    if gated:
        def mixed(rows):
            o = o_ref[rows, :]
            r = r_ref[rows, :]
            on = jnp.concatenate(
                [_rms(o[:, h * DVH:(h + 1) * DVH], gon_ref[...]) for h in range(GLA_HEADS)], axis=1)
            return on * (r * _sigmoid(r))
        m = [mixed(rows) for rows in subs]
    else:
        m = [m_ref[rows, :] for rows in subs]
    y = [_dot(mi.astype(BF), wo_ref[...]) for mi in m]
    h1 = [x_ref[rows, :] + _rms(yi, ng_ref[1:2, :]) for rows, yi in zip(subs, y)]
    u = [_rms(hi, ng_ref[2:3, :]).astype(BF) for hi in h1]
    f = [None] * n_sub
    for lo, hi in zip(FF_SPLITS[:-1], FF_SPLITS[1:]):
        gate = [_dot(ui, wgu_ref[:, lo:hi]) for ui in u]
        up = [_dot(ui, wgu_ref[:, D_FF + lo:D_FF + hi]) for ui in u]
        part = [_dot((gi * _sigmoid(gi) * pi).astype(BF), wd_ref[lo:hi, :]) for gi, pi in zip(gate, up)]
        f = [pi if fi is None else fi + pi for fi, pi in zip(f, part)]
    for rows, hi, fi in zip(subs, h1, f):
        out_ref[rows, :] = hi + _rms(fi, ng_ref[3:4, :])


def _mix_ffn(mix_inputs, x2, gon, wo, ng, wgu, wd):
    n = x2.shape[0]
    tm = _row_tile(n, 512)
    row = pl.BlockSpec((tm, D_MODEL), lambda i: (i, 0))
    gated = gon is not None
    args = list(mix_inputs) + [x2] + ([gon] if gated else []) + [wo, ng, wgu, wd]
    in_specs = [row] * (len(mix_inputs) + 1) + [_const_spec(a.shape) for a in args[len(mix_inputs) + 1:]]
    return pl.pallas_call(
        functools.partial(_mix_ffn_body, gated),
        grid=(n // tm,),
        in_specs=in_specs,
        out_specs=row,
        out_shape=jax.ShapeDtypeStruct((n, D_MODEL), F32),
        compiler_params=_params("parallel"),
        name="mix_ffn_gated" if gated else "mix_ffn",
    )(*args)


def _normed_halves(h_ref, gkv_ref, ng_ref):
    subs = _row_halves(h_ref.shape[0])
    hn = []
    for rows in subs:
        h = h_ref[rows, :]
        hn.append(h * lax.rsqrt(jnp.mean(h * h, axis=-1, keepdims=True) + EPS))
    hkv = [(hi * gkv_ref[...]).astype(BF) for hi in hn]
    hq = [(hi * ng_ref[0:1, :]).astype(BF) for hi in hn]
    return subs, hkv, hq


def _qkv_body(h_ref, gkv_ref, ng_ref, wkv_ref, wq_ref, kv0_ref, kv1_ref, kv2_ref, q_ref):
    subs, hkv, hq = _normed_halves(h_ref, gkv_ref, ng_ref)
    for rows, hi in zip(subs, hkv):
        for g, kv_ref in enumerate((kv0_ref, kv1_ref, kv2_ref)):
            kv_ref[rows, :] = _dot(hi, wkv_ref[:, g * KV_WIDTH:(g + 1) * KV_WIDTH])
    for rows, hi in zip(subs, hq):
        q_ref[rows, :] = _dot(hi, wq_ref[...])


def _qkv_seq_body(h_ref, gkv_ref, ng_ref, wkv_ref, wq_ref, kv0_ref, kv1_ref, q_ref, kvt_ref, qm_ref, kvm_ref,
                  stage_scr):
    last = N_GROUPS - 1
    wq_g = HEADS_PER_GROUP * HEAD_DIM
    subs, hkv, hq = _normed_halves(h_ref, gkv_ref, ng_ref)
    for rows, hi in zip(subs, hkv):
        for g, kv_ref in enumerate((kv0_ref, kv1_ref)):
            kv_ref[rows, :] = _dot(hi, wkv_ref[:, g * KV_WIDTH:(g + 1) * KV_WIDTH])
        kv = _dot(hi, wkv_ref[:, last * KV_WIDTH:])
        kvt_ref[0, :, rows] = kv.T
        for c in range(KV_WIDTH // LANES):
            stage_scr[wq_g // LANES + c, rows, :] = kv[:, c * LANES:(c + 1) * LANES]
    for rows, hi in zip(subs, hq):
        q_ref[rows, :] = _dot(hi, wq_ref[:, :last * wq_g])
        q_last = _dot(hi, wq_ref[:, last * wq_g:])
        for c in range(wq_g // LANES):
            stage_scr[c, rows, :] = q_last[:, c * LANES:(c + 1) * LANES]
    per_class = h_ref.shape[0] // DEINT
    for a in range(DEINT):
        picked = pl.ds(a, per_class, stride=DEINT)
        for c in range(wq_g // LANES):
            qm_ref[0, a, :, c * LANES:(c + 1) * LANES] = stage_scr[c, picked, :]
        for c in range(KV_WIDTH // LANES):
            kvm_ref[0, a, :, c * LANES:(c + 1) * LANES] = stage_scr[wq_g // LANES + c, picked, :]


def _qkv(h2, gkv, ng, wkv, wq):
    n = h2.shape[0]
    tm = _row_tile(n, 512)
    row = lambda w: pl.BlockSpec((tm, w), lambda i: (i, 0))
    nq = wq.shape[1]
    return pl.pallas_call(
        _qkv_body,
        grid=(n // tm,),
        in_specs=[row(D_MODEL), _const_spec(gkv.shape), _const_spec(ng.shape), _const_spec(wkv.shape),
                  _const_spec(wq.shape)],
        out_specs=[row(KV_WIDTH)] * N_GROUPS + [row(nq)],
        out_shape=[jax.ShapeDtypeStruct((n, KV_WIDTH), F32)] * N_GROUPS + [jax.ShapeDtypeStruct((n, nq), F32)],
        compiler_params=_params("parallel"),
        name="qkv",
    )(h2, gkv, ng, wkv, wq)


def _qkv_seq(h2, gkv, ng, wkv, wq, seq_len):
    n = h2.shape[0]
    tm = _row_tile(n, 512)
    assert seq_len % tm == 0 and tm % (DEINT * 8) == 0
    bsz, per_seq = n // seq_len, seq_len // tm
    wq_g = HEADS_PER_GROUP * HEAD_DIM
    row = lambda w: pl.BlockSpec((tm, w), lambda i: (i, 0))
    split = lambda w: pl.BlockSpec((1, DEINT, tm // DEINT, w), lambda i: (i // per_seq, 0, i % per_seq, 0))
    return pl.pallas_call(
        _qkv_seq_body,
        grid=(n // tm,),
        in_specs=[row(D_MODEL), _const_spec(gkv.shape), _const_spec(ng.shape), _const_spec(wkv.shape),
                  _const_spec(wq.shape)],
        out_specs=[row(KV_WIDTH), row(KV_WIDTH), row((N_GROUPS - 1) * wq_g),
                   pl.BlockSpec((1, KV_WIDTH, tm), lambda i: (i // per_seq, 0, i % per_seq)),
                   split(wq_g), split(KV_WIDTH)],
        out_shape=[jax.ShapeDtypeStruct((n, KV_WIDTH), F32), jax.ShapeDtypeStruct((n, KV_WIDTH), F32),
                   jax.ShapeDtypeStruct((n, (N_GROUPS - 1) * wq_g), F32),
                   jax.ShapeDtypeStruct((bsz, KV_WIDTH, seq_len), F32),
                   jax.ShapeDtypeStruct((bsz, DEINT, seq_len // DEINT, wq_g), F32),
                   jax.ShapeDtypeStruct((bsz, DEINT, seq_len // DEINT, KV_WIDTH), F32)],
        scratch_shapes=[pltpu.VMEM(((wq_g + KV_WIDTH) // LANES, tm, LANES), F32)],
        compiler_params=_params("parallel"),
        name="qkv_seq",
    )(h2, gkv, ng, wkv, wq)


def _attn_prompt_body(q00, q01, q10, q11, q20, q21, k0, k1, k2, v0, v1, v2, bias_rest_ref, bias_first_ref,
                      o_ref, o_scr, lse_scr, k_scr, v_scr):
    t_len = k0.shape[1]
    par = pl.program_id(1) % 2
    lane = lax.broadcasted_iota(jnp.int32, (1, LANES), 1)
    low = lane < HEAD_DIM
    same = (lane >= HEAD_DIM).astype(jnp.int32) == par

    def strided_rows(start, d):
        if d > 1:
            return pl.ds(start, QB, stride=d)
        return pl.ds(start if isinstance(start, int) else pl.multiple_of(start, QB), QB)

    def block_rows(blk, n=1):
        return pl.ds(pl.multiple_of(blk * QB, QB), n * QB)

    def load_block(g, ref, d, nblk, blk):
        rho = blk // nblk
        i = blk - rho * nblk
        if g == N_GROUPS - 1:
            inner = d // DEINT
            return ref[0, rho % DEINT, pl.ds(rho // DEINT + inner * QB * i, QB, stride=inner), :]
        return ref[0, strided_rows(rho + d * QB * i, d), :]

    def prepare(g, k_ref, v_ref, d, nblk, blk):
        dst = block_rows(blk)
        for src_ref, dst_scr in ((k_ref, k_scr), (v_ref, v_scr)):
            own = jnp.where(same, load_block(g, src_ref, d, nblk, blk), 0.0)
            dst_scr[par, dst, :] = own.astype(BF)
            dst_scr[1 - par, dst, :] = pltpu.roll(own, HEAD_DIM, axis=1).astype(BF)

    def attend(g, q_refs, d, nblk, blk, with_prev):
        rho = blk // nblk
        rows_q = strided_rows(rho + d * QB * (blk - rho * nblk), d)
        keys = block_rows(blk - 1, 2) if with_prev else block_rows(blk)
        nk = 2 * QB if with_prev else QB
        q = jnp.concatenate([load_block(g, q_refs[p], d, nblk, blk) for p in range(N_PAIR)],
                            axis=0).astype(BF)
        k_cat = jnp.concatenate([k_scr[0, keys, :], k_scr[1, keys, :]], axis=0)
        bias = bias_rest_ref[g] if with_prev else bias_first_ref[g]
        s = _dot_nt(q, k_cat) + bias
        mx, ps = [], []
        for odd in range(2):
            sh = s[:, odd * nk:(odd + 1) * nk]
            mx.append(jnp.max(sh, axis=-1, keepdims=True))
            ps.append(jnp.exp2(sh - mx[odd]).astype(BF))
        ones_lo = jnp.broadcast_to(jnp.where(low, 1.0, 0.0), (nk, LANES)).astype(BF)
        ones_hi = jnp.broadcast_to(jnp.where(low, 0.0, 1.0), (nk, LANES)).astype(BF)
        rhs = jnp.concatenate([jnp.concatenate([v_scr[0, keys, :], ones_lo], axis=1),
                               jnp.concatenate([v_scr[1, keys, :], ones_hi], axis=1)], axis=0)
        res = _dot(jnp.concatenate(ps, axis=1), rhs)
        m = jnp.where(low, mx[0], mx[1])
        for p in range(N_PAIR):
            part = slice(p * QB, (p + 1) * QB)
            o_g, l_g, m_g = res[part, :LANES], res[part, LANES:], m[part]
            if g > 0:
                slot = N_PAIR * (g - 1) + p
                o_scr[slot, rows_q, :] = o_g / l_g
                lse_scr[slot, rows_q, :] = m_g + jnp.log2(l_g)
            else:
                slots = [N_PAIR * other + p for other in range(N_GROUPS - 1)]
                lses = [lse_scr[s_, rows_q, :] for s_ in slots]
                top = jnp.maximum(jnp.maximum(m_g, lses[0]), lses[1])
                e_own = jnp.exp2(m_g - top)
                es = [jnp.exp2(ll - top) for ll in lses]
                num = e_own * o_g + es[0] * o_scr[slots[0], rows_q, :] + es[1] * o_scr[slots[1], rows_q, :]
                den = e_own * l_g + es[0] + es[1]
                o_ref[0, rows_q, p * LANES:(p + 1) * LANES] = num / den

    groups = (((q00, q01), k0, v0), ((q10, q11), k1, v1), ((q20, q21), k2, v2))
    for g in reversed(range(N_GROUPS)):
        q_refs, k_ref, v_ref = groups[g]
        d = DILATIONS[g]
        nblk = t_len // d // QB

        def prep_step(blk, carry, g=g, k_ref=k_ref, v_ref=v_ref, d=d, nblk=nblk):
            prepare(g, k_ref, v_ref, d, nblk, blk)
            return carry

        def first_step(rho, carry, g=g, q_refs=q_refs, d=d, nblk=nblk):
            attend(g, q_refs, d, nblk, rho * nblk, False)
            return carry

        def rest_step(n, carry, g=g, q_refs=q_refs, d=d, nblk=nblk):
            rho = n // (nblk - 1)
            attend(g, q_refs, d, nblk, n + rho + 1, True)
            return carry

        lax.fori_loop(0, d * nblk, prep_step, 0, unroll=8)
        lax.fori_loop(0, d, first_step, 0, unroll=min(d, 8))
        if nblk > 1:
            n_rest = d * (nblk - 1)
            lax.fori_loop(0, n_rest, rest_step, 0, unroll=n_rest // 2 if n_rest % 2 == 0 else n_rest)


def _attn_prompt(q, kvs, qm, kvm, bias_rest, bias_first):
    bsz, t, _ = q.shape
    assert t % (QB * DILATIONS[-1]) == 0
    wq = Q_PER_KV * HEAD_DIM
    heads_per_block = LANES // HEAD_DIM
    k_col = lambda h: h // heads_per_block
    v_col = lambda h: KV_HEADS // heads_per_block + h // heads_per_block
    q_specs = [pl.BlockSpec((1, t, LANES), lambda b, h, g=g, p=p: (b, 0, (g * KV_HEADS + h) * N_PAIR + p))
               for g in range(N_GROUPS - 1) for p in range(N_PAIR)]
    q_specs += [pl.BlockSpec((1, DEINT, t // DEINT, LANES), lambda b, h, p=p: (b, 0, 0, h * N_PAIR + p))
                for p in range(N_PAIR)]
    k_specs = [pl.BlockSpec((1, t, LANES), lambda b, h: (b, 0, k_col(h)))] * (N_GROUPS - 1)
    k_specs.append(pl.BlockSpec((1, DEINT, t // DEINT, LANES), lambda b, h: (b, 0, 0, k_col(h))))
    v_specs = [pl.BlockSpec((1, t, LANES), lambda b, h: (b, 0, v_col(h)))] * (N_GROUPS - 1)
    v_specs.append(pl.BlockSpec((1, DEINT, t // DEINT, LANES), lambda b, h: (b, 0, 0, v_col(h))))
    bias_specs = [pl.BlockSpec((N_GROUPS, N_PAIR * QB, tab.shape[-1]), lambda b, h: (0, h, 0))
                  for tab in (bias_rest, bias_first)]
    slots = (N_GROUPS - 1) * N_PAIR
    q_args = [q] * (N_PAIR * (N_GROUPS - 1)) + [qm] * N_PAIR
    return pl.pallas_call(
        _attn_prompt_body,
        grid=(bsz, KV_HEADS),
        in_specs=q_specs + k_specs + v_specs + bias_specs,
        out_specs=pl.BlockSpec((1, t, wq), lambda b, h: (b, 0, h)),
        out_shape=jax.ShapeDtypeStruct((bsz, t, KV_HEADS * wq), F32),
        scratch_shapes=[pltpu.VMEM((slots, t, LANES), F32)] * 2 + [pltpu.VMEM((2, t, LANES), BF)] * 2,
        compiler_params=_params("parallel", "arbitrary"),
        name="attn_prompt",
    )(*q_args, *kvs, kvm, *kvs, kvm, bias_rest, bias_first)


def _merge_groups(o_parts, lse_parts):
    mx = jnp.maximum(jnp.maximum(lse_parts[0], lse_parts[1]), lse_parts[2])
    es = [jnp.exp2(l - mx) for l in lse_parts]
    num = es[0] * o_parts[0] + es[1] * o_parts[1] + es[2] * o_parts[2]
    return num / (es[0] + es[1] + es[2])


def _attn_sample_body(qp_ref, n0, n1, n2, c0, c1, c2, bc0, bc1, bc2, bn_ref, o_ref, w0, w1, w2):
    n_new = n0.shape[1]
    half = KV_HEADS * HEAD_DIM
    lane = lax.broadcasted_iota(jnp.int32, (1, half), 1)
    tail_lane = lax.broadcasted_iota(jnp.int32, (1, LANES), 1)
    o_parts, lse_parts = [], []
    for g, (new_ref, c_ref, bc_ref, w_ref) in enumerate(((n0, c0, bc0, w0), (n1, c1, bc1, w1), (n2, c2, bc2, w2))):
        w = c_ref.shape[2]
        new = new_ref[0]
        newt = jnp.concatenate([new, jnp.zeros((LANES - n_new, 2 * half), F32)], axis=0).T
        ct = c_ref[0]
        shifted = pltpu.roll(ct, w - n_new, axis=1)
        tail = jnp.where(tail_lane >= LANES - n_new, pltpu.roll(newt, LANES - n_new, axis=1), shifted[:, w - LANES:])
        w_ref[0] = tail if w == LANES else jnp.concatenate([shifted[:, :w - LANES], tail], axis=1)
        kc, vc = ct[:half].astype(BF), ct[half:].astype(BF)
        kn, vn = newt[:half].astype(BF), newt[half:].astype(BF)
        o_g = jnp.zeros((Q_PER_KV * n_new, half), F32)
        l_g = jnp.zeros((Q_PER_KV * n_new, half), F32)
        for h in range(KV_HEADS):
            qp = qp_ref[0, g, h]
            s_c = _dot(qp, kc) + bc_ref[h]
            s_n = _dot(qp, kn) + bn_ref[g, h]
            m = jnp.maximum(jnp.max(s_c, axis=-1, keepdims=True), jnp.max(s_n, axis=-1, keepdims=True))
            p_c = jnp.exp2(s_c - m)
            p_n = jnp.exp2(s_n - m)
            l = jnp.sum(p_c, axis=-1, keepdims=True) + jnp.sum(p_n, axis=-1, keepdims=True)
            o = (_dot_nt(p_c.astype(BF), vc) + _dot_nt(p_n.astype(BF), vn)) / l
            mine = (lane >= h * HEAD_DIM) & (lane < (h + 1) * HEAD_DIM)
            o_g = jnp.where(mine, o, o_g)
            l_g = jnp.where(mine, m + jnp.log2(l), l_g)
        o_parts.append(o_g)
        lse_parts.append(l_g)
    o_ref[0] = _merge_groups(o_parts, lse_parts)


def _attn_sample(qpad, new_kvs, caches, bias_c, bias_n):
    bsz = qpad.shape[0]
    n_new = new_kvs[0].shape[1]
    rows = Q_PER_KV * n_new
    half = KV_HEADS * HEAD_DIM
    per_b = lambda shape: pl.BlockSpec((1,) + shape[1:], lambda b: (b,) + (0,) * (len(shape) - 1))
    cache_specs = [per_b(c.shape) for c in caches]
    return pl.pallas_call(
        _attn_sample_body,
        grid=(bsz,),
        in_specs=[per_b(qpad.shape)] + [per_b(a.shape) for a in new_kvs] + cache_specs
                 + [_const_spec(b.shape) for b in bias_c] + [_const_spec(bias_n.shape)],
        out_specs=[per_b((bsz, rows, half))] + cache_specs,
        out_shape=[jax.ShapeDtypeStruct((bsz, rows, half), F32)]
                  + [jax.ShapeDtypeStruct(c.shape, F32) for c in caches],
        compiler_params=_params("parallel"),
        name="attn_sample",
    )(qpad, *new_kvs, *caches, *bias_c, bias_n)


def _t5_buckets(dist):
    d = np.asarray(dist)
    large = MAX_EXACT + (np.log(np.maximum(d, 1) / MAX_EXACT) / np.log(MAX_DISTANCE / MAX_EXACT)
                         * (N_BUCKETS - MAX_EXACT)).astype(np.int64)
    large = np.minimum(large, N_BUCKETS - 1)
    return np.where(d < MAX_EXACT, d, large).astype(np.int32)


def _group_bias(rel_bias, g):
    bk = _t5_buckets(DILATIONS[g] * np.arange(N_KEYS))
    return rel_bias[bk][:, g * HEADS_PER_GROUP:(g + 1) * HEADS_PER_GROUP].T.astype(F32) * LOG2E


def _prompt_bias_tables(rel_bias):
    period = 2 * QB
    rest, first = [], []
    for g in range(N_GROUPS):
        bv = _group_bias(rel_bias, g)
        row0 = jnp.concatenate([bv[:, ::-1], jnp.full((HEADS_PER_GROUP, period - N_KEYS), NEG, F32)], axis=1)
        wrap = jnp.concatenate([row0, row0, row0[:, :1]], axis=1)
        skew = jnp.broadcast_to(wrap[:, None, :], (HEADS_PER_GROUP, QB, 2 * period + 1))
        skew = skew.reshape(HEADS_PER_GROUP, -1)[:, :QB * 2 * period].reshape(HEADS_PER_GROUP, QB, 2 * period)
        tab = skew[:, :, period:]
        for out, part in ((rest, tab), (first, tab[:, :, QB:])):
            nk = part.shape[-1]
            pairs = part.reshape(HEADS_PER_GROUP // 2, 2, QB, nk).transpose(0, 2, 1, 3)
            out.append(pairs.reshape(HEADS_PER_GROUP // 2 * QB, 2 * nk))
    return jnp.stack(rest, axis=0), jnp.stack(first, axis=0)


def _sample_bias_tables(rel_bias, n_new):
    tabs_c, tabs_n = [], []
    for g in range(N_GROUPS):
        w, d = WINDOWS[g], DILATIONS[g]
        bv = _group_bias(rel_bias, g)
        fill = jnp.full(bv.shape, NEG, F32)
        dil = jnp.stack([bv] + [fill] * (d - 1), axis=-1).reshape(HEADS_PER_GROUP, N_KEYS * d)[:, :w + 1]
        ext = jnp.concatenate([jnp.full((HEADS_PER_GROUP, LANES - 1), NEG, F32), dil,
                               jnp.full((HEADS_PER_GROUP, n_new - 1), NEG, F32)], axis=1)
        rev = ext[:, ::-1]
        rows = [rev[:, n_new - 1 - r:n_new - 1 - r + w + LANES] for r in range(n_new)]
        tab = jnp.stack(rows, axis=1).reshape(KV_HEADS, Q_PER_KV * n_new, w + LANES)
        tabs_c.append(tab[:, :, :w])
        tabs_n.append(tab[:, :, w:])
    return tabs_c, jnp.stack(tabs_n, axis=0)


def _prep_weights(w_in_a, w_a2, b_a, w_o_a, w_kv, w_q_b, w_o_b, w_gate_up, w_down):
    n_main = 2 * GLA_DK + 2 * GLA_DV
    w_in = w_in_a[0]
    wa = jnp.pad(w_in[:, n_main:], ((0, 0), (0, LANES - GATE_RANK)))
    wa2 = jnp.pad(w_a2[0], ((0, LANES - GATE_RANK), (0, 0)))
    return dict(
        wm=w_in.astype(BF), wa=wa.astype(BF), wa2=wa2.astype(BF), ba=b_a[0][None, :],
        wo_a=w_o_a[0].astype(BF), wkv=w_kv.astype(BF),
        wq=(w_q_b[0] * (HEAD_DIM ** -0.5 * LOG2E)).astype(BF),
        wo_b=w_o_b[0].astype(BF),
        wgu=[w_gate_up[l].astype(BF) for l in range(2)], wd=[w_down[l].astype(BF) for l in range(2)])


def _from_positions_last(a):
    return jnp.transpose(a.reshape(a.shape[0], 2, KV_HEADS, HEAD_DIM, a.shape[-1]), (0, 4, 1, 2, 3))


def _layer0(x, s0, norm_g, g_onorm, wts):
    bsz, t, _ = x.shape
    x2 = x.reshape(bsz * t, D_MODEL)
    v_dtype = BF if t % GLA_CHUNK == 0 else F32
    q, k, v, r, g = _gla_in(x2, norm_g[0], wts["wm"], wts["wa"], wts["wa2"], wts["ba"], v_dtype)
    sh = lambda a: a.reshape(bsz, t, a.shape[-1])
    o, st = _gla(sh(q), sh(k), sh(v), sh(g), s0)
    h = _mix_ffn([o.reshape(bsz * t, GLA_DV), r], x2, g_onorm, wts["wo_a"], norm_g[0], wts["wgu"][0],
                 wts["wd"][0])
    return h, st


def kernel(x_prompt, x_sample, state_gla, cache_win1, cache_win2, cache_win3, norm_g, w_in_a, w_a2, b_a,
           g_onorm, w_o_a, g_kv, w_kv, w_q_b, w_o_b, rel_bias, w_gate_up, w_down):
    wts = _prep_weights(w_in_a, w_a2, b_a, w_o_a, w_kv, w_q_b, w_o_b, w_gate_up, w_down)
    gkv = g_kv[None, :]

    bp, tp, _ = x_prompt.shape
    s0p = jnp.zeros((bp, GLA_HEADS, DKH, DVH), F32)
    h_p, gla_p = _layer0(x_prompt, s0p, norm_g, g_onorm, wts)
    *kv_p, q_p, kvt_last, qm_p, kvm_p = _qkv_seq(h_p, gkv, norm_g[1], wts["wkv"], wts["wq"], tp)
    kv_p = [a.reshape(bp, tp, KV_WIDTH) for a in kv_p]
    o_p = _attn_prompt(q_p.reshape(bp, tp, -1), kv_p, qm_p, kvm_p, *_prompt_bias_tables(rel_bias))
    y_p = _mix_ffn([o_p.reshape(bp * tp, D_MODEL)], h_p, None, wts["wo_b"], norm_g[1], wts["wgu"][1],
                   wts["wd"][1])
    win_p = [kv_p[g][:, tp - min(WINDOWS[g], tp):].reshape(bp, -1, 2, KV_HEADS, HEAD_DIM)
             for g in range(N_GROUPS - 1)]
    assert WINDOWS[-1] >= tp
    win_p.append(_from_positions_last(kvt_last))

    bs, ts, _ = x_sample.shape
    h_s, gla_s = _layer0(x_sample, state_gla[0], norm_g, g_onorm, wts)
    *kv_s, q_s = _qkv(h_s, gkv, norm_g[1], wts["wkv"], wts["wq"])
    q6 = q_s.astype(BF).reshape(bs, ts, N_GROUPS, KV_HEADS, Q_PER_KV, HEAD_DIM).transpose(0, 2, 3, 4, 1, 5)
    q6 = q6.reshape(bs, N_GROUPS, KV_HEADS, Q_PER_KV * ts, HEAD_DIM)
    qpad = jnp.stack([jnp.pad(q6[:, :, h], ((0, 0), (0, 0), (0, 0), (h * HEAD_DIM, (KV_HEADS - 1 - h) * HEAD_DIM)))
                      for h in range(KV_HEADS)], axis=2)
    caches = [jnp.transpose(c, (0, 2, 3, 4, 1)).reshape(bs, KV_WIDTH, c.shape[1])
              for c in (cache_win1, cache_win2, cache_win3)]
    bias_c, bias_n = _sample_bias_tables(rel_bias, ts)
    o_s, *win_s = _attn_sample(qpad, [a.reshape(bs, ts, KV_WIDTH) for a in kv_s], caches, bias_c, bias_n)
    o_s = o_s.reshape(bs, Q_PER_KV, ts, KV_HEADS, HEAD_DIM).transpose(0, 2, 3, 1, 4).reshape(bs * ts, D_MODEL)
    y_s = _mix_ffn([o_s], h_s, None, wts["wo_b"], norm_g[1], wts["wgu"][1], wts["wd"][1])
    win_s = [_from_positions_last(w) for w in win_s]

    return (y_p.reshape(bp, tp, D_MODEL), y_s.reshape(bs, ts, D_MODEL), gla_p[None], win_p[0], win_p[1],
            win_p[2], gla_s[None], win_s[0], win_s[1], win_s[2])
```

```python
import functools

import numpy as np
import jax
import jax.numpy as jnp
from jax import lax
from jax.experimental import pallas as pl
from jax.experimental.pallas import tpu as pltpu

BF = jnp.bfloat16
F32 = jnp.float32

D_MODEL = 1024
GLA_HEADS = 4
GLA_DK = 512
GLA_DV = 1024
DKH = GLA_DK // GLA_HEADS
DVH = GLA_DV // GLA_HEADS
GATE_RANK = 16
GATE_TAU = 16.0
GLA_CHUNK = 64
GLA_SUB = 16
GLA_UNROLL = 4
GLA_TIME_BLOCK = 1024
WINDOWS = (128, 512, 2048)
DILATIONS = (1, 4, 16)
N_GROUPS = 3
HEAD_DIM = 64
HEADS_PER_GROUP = 16
KV_HEADS = 4
Q_PER_KV = 4
N_PAIR = Q_PER_KV // 2
N_KEYS = 129
N_BUCKETS = 32
MAX_EXACT = 16
MAX_DISTANCE = 2048
D_FF = 2816
FF_SPLITS = (0, 1536, D_FF)
EPS = 1e-6
NEG = -1e30
LOG2E = 1.4426950408889634
QB = 128
LANES = 128
ROW_ALIGN = 16
DEINT = 4
KV_WIDTH = 2 * KV_HEADS * HEAD_DIM
VMEM_LIMIT_BYTES = 56 * 1024 * 1024


def _dot(a, b):
    return jnp.dot(a, b, preferred_element_type=F32)


def _dot_nt(a, b):
    return lax.dot_general(a, b, (((1,), (1,)), ((), ())), preferred_element_type=F32)


def _rms(x, g):
    return x * lax.rsqrt(jnp.mean(x * x, axis=-1, keepdims=True) + EPS) * g


def _sigmoid(x):
    return 1.0 / (1.0 + jnp.exp(-x))


def _const_spec(shape):
    nd = len(shape)
    return pl.BlockSpec(shape, lambda *_: (0,) * nd, pipeline_mode=pl.Buffered(1))


def _params(*sem):
    return pltpu.CompilerParams(dimension_semantics=sem, vmem_limit_bytes=VMEM_LIMIT_BYTES)


def _row_tile(n, want):
    tm = min(n, want)
    assert n % tm == 0
    return tm


def _row_halves(tm):
    n_sub = 2 if tm % (2 * ROW_ALIGN) == 0 else 1
    return [slice(i * tm // n_sub, (i + 1) * tm // n_sub) for i in range(n_sub)]


def _gla_in_body(x_ref, ng_ref, wm_ref, wa_ref, wa2_ref, ba_ref, q_ref, k_ref, v_ref, r_ref, g_ref):
    subs = _row_halves(x_ref.shape[0])
    xn = [_rms(x_ref[rows, :], ng_ref[0:1, :]).astype(BF) for rows in subs]
    a = [_dot(xi, wa_ref[...]).astype(BF) for xi in xn]
    z = [_dot(ai, wa2_ref[...]) + ba_ref[...] for ai in a]
    for rows, zi in zip(subs, z):
        g_ref[rows, :] = (jnp.minimum(zi, 0.0) - jnp.log(1.0 + jnp.exp(-jnp.abs(zi)))) * (1.0 / GATE_TAU)
    for rows, xi in zip(subs, xn):
        q_ref[rows, :] = _dot(xi, wm_ref[:, 0:GLA_DK]) * (DKH ** -0.5)
        k_ref[rows, :] = _dot(xi, wm_ref[:, GLA_DK:2 * GLA_DK])
        v_ref[rows, :] = _dot(xi, wm_ref[:, 2 * GLA_DK:2 * GLA_DK + GLA_DV]).astype(v_ref.dtype)
        r_ref[rows, :] = _dot(xi, wm_ref[:, 2 * GLA_DK + GLA_DV:2 * GLA_DK + 2 * GLA_DV])


def _gla_in(x2, ng, wm, wa, wa2, ba, v_dtype):
    n = x2.shape[0]
    tm = _row_tile(n, 512)
    row = lambda w: pl.BlockSpec((tm, w), lambda i: (i, 0))
    return pl.pallas_call(
        _gla_in_body,
        grid=(n // tm,),
        in_specs=[row(D_MODEL), _const_spec(ng.shape), _const_spec(wm.shape), _const_spec(wa.shape),
                  _const_spec(wa2.shape), _const_spec(ba.shape)],
        out_specs=[row(GLA_DK), row(GLA_DK), row(GLA_DV), row(GLA_DV), row(GLA_DK)],
        out_shape=[jax.ShapeDtypeStruct((n, w), dt) for w, dt in
                   ((GLA_DK, F32), (GLA_DK, F32), (GLA_DV, v_dtype), (GLA_DV, F32), (GLA_DK, F32))],
        compiler_params=_params("parallel"),
        name="gla_in",
    )(x2, ng, wm, wa, wa2, ba)


def _gla_body(c_real, q_ref, k_ref, v_ref, g_ref, s0_ref, o_ref, st_ref, s_scr):
    t = pl.program_id(1)
    c = min(GLA_CHUNK, max(c_real, GLA_SUB))
    nsb = c // GLA_SUB
    n_chunks = q_ref.shape[1] // c_real

    @pl.when(t == 0)
    def _():
        s_scr[...] = s0_ref[0]

    gc = min(n_chunks, GLA_UNROLL)
    assert n_chunks % gc == 0
    ri = lax.broadcasted_iota(jnp.int32, (gc * c, gc * c), 0)
    ci = lax.broadcasted_iota(jnp.int32, (gc * c, gc * c), 1)
    tril_bf = jnp.where((ri >= ci) & (ri // c == ci // c), 1.0, 0.0).astype(BF)
    ri2 = lax.broadcasted_iota(jnp.int32, (c, LANES), 0)
    ci2 = lax.broadcasted_iota(jnp.int32, (c, LANES), 1)
    causal = ri2 >= ci2

    def pad_rows(a, rows):
        if a.shape[0] == rows:
            return a
        return jnp.concatenate([a, jnp.zeros((rows - a.shape[0], a.shape[1]), a.dtype)], axis=0)

    heads = range(GLA_HEADS)
    ks = [slice(h * DKH, (h + 1) * DKH) for h in heads]
    vs = [slice(h * DVH, (h + 1) * DVH) for h in heads]
    units = [(j, h) for j in range(gc) for h in heads]

    def group(idx, carry):
        span = gc * c_real
        grows = pl.ds(pl.multiple_of(idx * span, span), span)
        rows = [pl.ds(pl.multiple_of(idx * span + j * c_real, c_real), c_real) for j in range(gc)]
        g_all = pad_rows(g_ref[0, grows, :], gc * c)
        g_hi = g_all.astype(BF)
        g_lo = (g_all - g_hi.astype(F32)).astype(BF)
        b_all = _dot(tril_bf, g_hi) + _dot(tril_bf, g_lo)
        b = {(j, h): b_all[j * c:(j + 1) * c, ks[h]] for j, h in units}
        qh = {(j, h): pad_rows(q_ref[0, rows[j], ks[h]], c) for j, h in units}
        kh = {(j, h): pad_rows(k_ref[0, rows[j], ks[h]], c) for j, h in units}
        vh = {(j, h): pad_rows(v_ref[0, rows[j], vs[h]], LANES) for j, h in units}
        b_last = {u: b[u][c - 1:c, :] for u in units}
        scores = {}
        for u in units:
            qparts, kparts = [], []
            for sbi in range(nsb):
                lo, hi = sbi * GLA_SUB, (sbi + 1) * GLA_SUB
                ref_row = b[u][lo:lo + 1, :]
                qj = (qh[u][lo:] * jnp.exp(b[u][lo:] - ref_row)).astype(BF)
                qparts.append(jnp.concatenate([jnp.zeros((lo, DKH), BF), qj], axis=0) if lo else qj)
                kj = (kh[u][lo:hi] * jnp.exp(ref_row - b[u][lo:hi])).astype(BF)
                pieces = []
                if lo > 0:
                    pieces.append(jnp.zeros((lo, DKH), BF))
                pieces.append(kj)
                pieces.append(jnp.zeros((LANES - hi, DKH), BF))
                kparts.append(jnp.concatenate(pieces, axis=0))
            qcat = jnp.concatenate(qparts, axis=1)
            kcat = jnp.concatenate(kparts, axis=1)
            scores[u] = _dot_nt(qcat, kcat)
        upd, decay = {}, {}
        for u in units:
            k2 = pad_rows(kh[u] * jnp.exp(b_last[u] - b[u]), LANES)
            upd[u] = _dot(k2.T.astype(BF), vh[u].astype(BF))
            col = jnp.broadcast_to(jnp.exp(b_last[u]), (LANES, DKH)).T
            decay[u] = jnp.concatenate([col] * (DVH // LANES), axis=1)
        lhs = {u: jnp.concatenate([(qh[u] * jnp.exp(b[u])).astype(BF),
                                   jnp.where(causal, scores[u], 0.0).astype(BF)], axis=1) for u in units}
        v_bf = {u: vh[u].astype(BF) for u in units}
        st = [s_scr[h] for h in heads]
        for j in range(gc):
            for h in heads:
                o = _dot(lhs[j, h], jnp.concatenate([st[h].astype(BF), v_bf[j, h]], axis=0))
                o_ref[0, rows[j], vs[h]] = o[0:c_real]
            st = [st[h] * decay[j, h] + upd[j, h] for h in heads]
        for h in heads:
            s_scr[h] = st[h]
        return carry

    lax.fori_loop(0, n_chunks // gc, group, 0)

    @pl.when(t == pl.num_programs(1) - 1)
    def _():
        st_ref[0] = s_scr[...]


def _gla(q, k, v, g, s0):
    bsz, t, _ = q.shape
    c_real = min(t, GLA_CHUNK)
    tc = min(t, GLA_TIME_BLOCK)
    assert t % tc == 0 and tc % c_real == 0
    blk = lambda w: pl.BlockSpec((1, tc, w), lambda b, i: (b, i, 0))
    st_spec = pl.BlockSpec((1, GLA_HEADS, DKH, DVH), lambda b, i: (b, 0, 0, 0))
    return pl.pallas_call(
        functools.partial(_gla_body, c_real),
        grid=(bsz, t // tc),
        in_specs=[blk(GLA_DK), blk(GLA_DK), blk(GLA_DV), blk(GLA_DK), st_spec],
        out_specs=[blk(GLA_DV), st_spec],
        out_shape=[jax.ShapeDtypeStruct((bsz, t, GLA_DV), F32),
                   jax.ShapeDtypeStruct((bsz, GLA_HEADS, DKH, DVH), F32)],
        scratch_shapes=[pltpu.VMEM((GLA_HEADS, DKH, DVH), F32)],
        compiler_params=_params("parallel", "arbitrary"),
        name="gla",
    )(q, k, v, g, s0)


def _mix_ffn_body(gated, *refs):
    if gated:
        o_ref, r_ref, x_ref, gon_ref, wo_ref, ng_ref, wgu_ref, wd_ref, out_ref = refs
    else:
        m_ref, x_ref, wo_ref, ng_ref, wgu_ref, wd_ref, out_ref = refs
    subs = _row_halves(x_ref.shape[0])
    n_sub = len(subs)
    if gated:
        def mixed(rows):
            o = o_ref[rows, :]
            r = r_ref[rows, :]
            on = jnp.concatenate(
                [_rms(o[:, h * DVH:(h + 1) * DVH], gon_ref[...]) for h in range(GLA_HEADS)], axis=1)
            return on * (r * _sigmoid(r))
        m = [mixed(rows) for rows in subs]
    else:
        m = [m_ref[rows, :] for rows in subs]
    y = [_dot(mi.astype(BF), wo_ref[...]) for mi in m]
    h1 = [x_ref[rows, :] + _rms(yi, ng_ref[1:2, :]) for rows, yi in zip(subs, y)]
    u = [_rms(hi, ng_ref[2:3, :]).astype(BF) for hi in h1]
    f = [None] * n_sub
    for lo, hi in zip(FF_SPLITS[:-1], FF_SPLITS[1:]):
        gate = [_dot(ui, wgu_ref[:, lo:hi]) for ui in u]
        up = [_dot(ui, wgu_ref[:, D_FF + lo:D_FF + hi]) for ui in u]
        part = [_dot((gi * _sigmoid(gi) * pi).astype(BF), wd_ref[lo:hi, :]) for gi, pi in zip(gate, up)]
        f = [pi if fi is None else fi + pi for fi, pi in zip(f, part)]
    for rows, hi, fi in zip(subs, h1, f):
        out_ref[rows, :] = hi + _rms(fi, ng_ref[3:4, :])


def _mix_ffn(mix_inputs, x2, gon, wo, ng, wgu, wd):
    n = x2.shape[0]
    tm = _row_tile(n, 512)
    row = pl.BlockSpec((tm, D_MODEL), lambda i: (i, 0))
    gated = gon is not None
    args = list(mix_inputs) + [x2] + ([gon] if gated else []) + [wo, ng, wgu, wd]
    in_specs = [row] * (len(mix_inputs) + 1) + [_const_spec(a.shape) for a in args[len(mix_inputs) + 1:]]
    return pl.pallas_call(
        functools.partial(_mix_ffn_body, gated),
        grid=(n // tm,),
        in_specs=in_specs,
        out_specs=row,
        out_shape=jax.ShapeDtypeStruct((n, D_MODEL), F32),
        compiler_params=_params("parallel"),
        name="mix_ffn_gated" if gated else "mix_ffn",
    )(*args)


def _normed_halves(h_ref, gkv_ref, ng_ref):
    subs = _row_halves(h_ref.shape[0])
    hn = []
    for rows in subs:
        h = h_ref[rows, :]
        hn.append(h * lax.rsqrt(jnp.mean(h * h, axis=-1, keepdims=True) + EPS))
    hkv = [(hi * gkv_ref[...]).astype(BF) for hi in hn]
    hq = [(hi * ng_ref[0:1, :]).astype(BF) for hi in hn]
    return subs, hkv, hq


def _qkv_body(h_ref, gkv_ref, ng_ref, wkv_ref, wq_ref, kv0_ref, kv1_ref, kv2_ref, q_ref):
    subs, hkv, hq = _normed_halves(h_ref, gkv_ref, ng_ref)
    for rows, hi in zip(subs, hkv):
        for g, kv_ref in enumerate((kv0_ref, kv1_ref, kv2_ref)):
            kv_ref[rows, :] = _dot(hi, wkv_ref[:, g * KV_WIDTH:(g + 1) * KV_WIDTH])
    for rows, hi in zip(subs, hq):
        q_ref[rows, :] = _dot(hi, wq_ref[...])


def _qkv_seq_body(h_ref, gkv_ref, ng_ref, wkv_ref, wq_ref, kv0_ref, kv1_ref, q_ref, kvt_ref, qm_ref, kvm_ref,
                  stage_scr):
    last = N_GROUPS - 1
    wq_g = HEADS_PER_GROUP * HEAD_DIM
    subs, hkv, hq = _normed_halves(h_ref, gkv_ref, ng_ref)
    for rows, hi in zip(subs, hkv):
        for g, kv_ref in enumerate((kv0_ref, kv1_ref)):
            kv_ref[rows, :] = _dot(hi, wkv_ref[:, g * KV_WIDTH:(g + 1) * KV_WIDTH])
        kv = _dot(hi, wkv_ref[:, last * KV_WIDTH:])
        kvt_ref[0, :, rows] = kv.T
        for c in range(KV_WIDTH // LANES):
            stage_scr[wq_g // LANES + c, rows, :] = kv[:, c * LANES:(c + 1) * LANES]
    for rows, hi in zip(subs, hq):
        q_ref[rows, :] = _dot(hi, wq_ref[:, :last * wq_g])
        q_last = _dot(hi, wq_ref[:, last * wq_g:])
        for c in range(wq_g // LANES):
            stage_scr[c, rows, :] = q_last[:, c * LANES:(c + 1) * LANES]
    per_class = h_ref.shape[0] // DEINT
    for a in range(DEINT):
        picked = pl.ds(a, per_class, stride=DEINT)
        for c in range(wq_g // LANES):
            qm_ref[0, a, :, c * LANES:(c + 1) * LANES] = stage_scr[c, picked, :]
        for c in range(KV_WIDTH // LANES):
            kvm_ref[0, a, :, c * LANES:(c + 1) * LANES] = stage_scr[wq_g // LANES + c, picked, :]


def _qkv(h2, gkv, ng, wkv, wq):
    n = h2.shape[0]
    tm = _row_tile(n, 512)
    row = lambda w: pl.BlockSpec((tm, w), lambda i: (i, 0))
    nq = wq.shape[1]
    return pl.pallas_call(
        _qkv_body,
        grid=(n // tm,),
        in_specs=[row(D_MODEL), _const_spec(gkv.shape), _const_spec(ng.shape), _const_spec(wkv.shape),
                  _const_spec(wq.shape)],
        out_specs=[row(KV_WIDTH)] * N_GROUPS + [row(nq)],
        out_shape=[jax.ShapeDtypeStruct((n, KV_WIDTH), F32)] * N_GROUPS + [jax.ShapeDtypeStruct((n, nq), F32)],
        compiler_params=_params("parallel"),
        name="qkv",
    )(h2, gkv, ng, wkv, wq)


def _qkv_seq(h2, gkv, ng, wkv, wq, seq_len):
    n = h2.shape[0]
    tm = _row_tile(n, 512)
    assert seq_len % tm == 0 and tm % (DEINT * 8) == 0
    bsz, per_seq = n // seq_len, seq_len // tm
    wq_g = HEADS_PER_GROUP * HEAD_DIM
    row = lambda w: pl.BlockSpec((tm, w), lambda i: (i, 0))
    split = lambda w: pl.BlockSpec((1, DEINT, tm // DEINT, w), lambda i: (i // per_seq, 0, i % per_seq, 0))
    return pl.pallas_call(
        _qkv_seq_body,
        grid=(n // tm,),
        in_specs=[row(D_MODEL), _const_spec(gkv.shape), _const_spec(ng.shape), _const_spec(wkv.shape),
                  _const_spec(wq.shape)],
        out_specs=[row(KV_WIDTH), row(KV_WIDTH), row((N_GROUPS - 1) * wq_g),
                   pl.BlockSpec((1, KV_WIDTH, tm), lambda i: (i // per_seq, 0, i % per_seq)),
                   split(wq_g), split(KV_WIDTH)],
        out_shape=[jax.ShapeDtypeStruct((n, KV_WIDTH), F32), jax.ShapeDtypeStruct((n, KV_WIDTH), F32),
                   jax.ShapeDtypeStruct((n, (N_GROUPS - 1) * wq_g), F32),
                   jax.ShapeDtypeStruct((bsz, KV_WIDTH, seq_len), F32),
                   jax.ShapeDtypeStruct((bsz, DEINT, seq_len // DEINT, wq_g), F32),
                   jax.ShapeDtypeStruct((bsz, DEINT, seq_len // DEINT, KV_WIDTH), F32)],
        scratch_shapes=[pltpu.VMEM(((wq_g + KV_WIDTH) // LANES, tm, LANES), F32)],
        compiler_params=_params("parallel"),
        name="qkv_seq",
    )(h2, gkv, ng, wkv, wq)


def _attn_prompt_body(q00, q01, q10, q11, q20, q21, k0, k1, k2, v0, v1, v2, bias_rest_ref, bias_first_ref,
                      o_ref, o_scr, lse_scr, k_scr, v_scr):
    t_len = k0.shape[1]
    par = pl.program_id(1) % 2
    lane = lax.broadcasted_iota(jnp.int32, (1, LANES), 1)
    low = lane < HEAD_DIM
    same = (lane >= HEAD_DIM).astype(jnp.int32) == par

    def strided_rows(start, d):
        if d > 1:
            return pl.ds(start, QB, stride=d)
        return pl.ds(start if isinstance(start, int) else pl.multiple_of(start, QB), QB)

    def block_rows(blk, n=1):
        return pl.ds(pl.multiple_of(blk * QB, QB), n * QB)

    def load_block(g, ref, d, nblk, blk):
        rho = blk // nblk
        i = blk - rho * nblk
        if g == N_GROUPS - 1:
            inner = d // DEINT
            return ref[0, rho % DEINT, pl.ds(rho // DEINT + inner * QB * i, QB, stride=inner), :]
        return ref[0, strided_rows(rho + d * QB * i, d), :]

    def prepare(g, k_ref, v_ref, d, nblk, blk):
        dst = block_rows(blk)
        for src_ref, dst_scr in ((k_ref, k_scr), (v_ref, v_scr)):
            own = jnp.where(same, load_block(g, src_ref, d, nblk, blk), 0.0)
            dst_scr[par, dst, :] = own.astype(BF)
            dst_scr[1 - par, dst, :] = pltpu.roll(own, HEAD_DIM, axis=1).astype(BF)

    def attend(g, q_refs, d, nblk, blk, with_prev):
        rho = blk // nblk
        rows_q = strided_rows(rho + d * QB * (blk - rho * nblk), d)
        keys = block_rows(blk - 1, 2) if with_prev else block_rows(blk)
        nk = 2 * QB if with_prev else QB
        q = jnp.concatenate([load_block(g, q_refs[p], d, nblk, blk) for p in range(N_PAIR)],
                            axis=0).astype(BF)
        k_cat = jnp.concatenate([k_scr[0, keys, :], k_scr[1, keys, :]], axis=0)
        bias = bias_rest_ref[g] if with_prev else bias_first_ref[g]
        s = _dot_nt(q, k_cat) + bias
        mx, ps = [], []
        for odd in range(2):
            sh = s[:, odd * nk:(odd + 1) * nk]
            mx.append(jnp.max(sh, axis=-1, keepdims=True))
            ps.append(jnp.exp2(sh - mx[odd]).astype(BF))
        ones_lo = jnp.broadcast_to(jnp.where(low, 1.0, 0.0), (nk, LANES)).astype(BF)
        ones_hi = jnp.broadcast_to(jnp.where(low, 0.0, 1.0), (nk, LANES)).astype(BF)
        rhs = jnp.concatenate([jnp.concatenate([v_scr[0, keys, :], ones_lo], axis=1),
                               jnp.concatenate([v_scr[1, keys, :], ones_hi], axis=1)], axis=0)
        res = _dot(jnp.concatenate(ps, axis=1), rhs)
        m = jnp.where(low, mx[0], mx[1])
        for p in range(N_PAIR):
            part = slice(p * QB, (p + 1) * QB)
            o_g, l_g, m_g = res[part, :LANES], res[part, LANES:], m[part]
            if g > 0:
                slot = N_PAIR * (g - 1) + p
                o_scr[slot, rows_q, :] = o_g / l_g
                lse_scr[slot, rows_q, :] = m_g + jnp.log2(l_g)
            else:
                slots = [N_PAIR * other + p for other in range(N_GROUPS - 1)]
                lses = [lse_scr[s_, rows_q, :] for s_ in slots]
                top = jnp.maximum(jnp.maximum(m_g, lses[0]), lses[1])
                e_own = jnp.exp2(m_g - top)
                es = [jnp.exp2(ll - top) for ll in lses]
                num = e_own * o_g + es[0] * o_scr[slots[0], rows_q, :] + es[1] * o_scr[slots[1], rows_q, :]
                den = e_own * l_g + es[0] + es[1]
                o_ref[0, rows_q, p * LANES:(p + 1) * LANES] = num / den

    groups = (((q00, q01), k0, v0), ((q10, q11), k1, v1), ((q20, q21), k2, v2))
    for g in reversed(range(N_GROUPS)):
        q_refs, k_ref, v_ref = groups[g]
        d = DILATIONS[g]
        nblk = t_len // d // QB

        def prep_step(blk, carry, g=g, k_ref=k_ref, v_ref=v_ref, d=d, nblk=nblk):
            prepare(g, k_ref, v_ref, d, nblk, blk)
            return carry

        def first_step(rho, carry, g=g, q_refs=q_refs, d=d, nblk=nblk):
            attend(g, q_refs, d, nblk, rho * nblk, False)
            return carry

        def rest_step(n, carry, g=g, q_refs=q_refs, d=d, nblk=nblk):
            rho = n // (nblk - 1)
            attend(g, q_refs, d, nblk, n + rho + 1, True)
            return carry

        lax.fori_loop(0, d * nblk, prep_step, 0, unroll=8)
        lax.fori_loop(0, d, first_step, 0, unroll=min(d, 8))
        if nblk > 1:
            n_rest = d * (nblk - 1)
            lax.fori_loop(0, n_rest, rest_step, 0, unroll=n_rest // 2 if n_rest % 2 == 0 else n_rest)


def _attn_prompt(q, kvs, qm, kvm, bias_rest, bias_first):
    bsz, t, _ = q.shape
    assert t % (QB * DILATIONS[-1]) == 0
    wq = Q_PER_KV * HEAD_DIM
    heads_per_block = LANES // HEAD_DIM
    k_col = lambda h: h // heads_per_block
    v_col = lambda h: KV_HEADS // heads_per_block + h // heads_per_block
    q_specs = [pl.BlockSpec((1, t, LANES), lambda b, h, g=g, p=p: (b, 0, (g * KV_HEADS + h) * N_PAIR + p))
               for g in range(N_GROUPS - 1) for p in range(N_PAIR)]
    q_specs += [pl.BlockSpec((1, DEINT, t // DEINT, LANES), lambda b, h, p=p: (b, 0, 0, h * N_PAIR + p))
                for p in range(N_PAIR)]
    k_specs = [pl.BlockSpec((1, t, LANES), lambda b, h: (b, 0, k_col(h)))] * (N_GROUPS - 1)
    k_specs.append(pl.BlockSpec((1, DEINT, t // DEINT, LANES), lambda b, h: (b, 0, 0, k_col(h))))
    v_specs = [pl.BlockSpec((1, t, LANES), lambda b, h: (b, 0, v_col(h)))] * (N_GROUPS - 1)
    v_specs.append(pl.BlockSpec((1, DEINT, t // DEINT, LANES), lambda b, h: (b, 0, 0, v_col(h))))
    bias_specs = [pl.BlockSpec((N_GROUPS, N_PAIR * QB, tab.shape[-1]), lambda b, h: (0, h, 0))
                  for tab in (bias_rest, bias_first)]
    slots = (N_GROUPS - 1) * N_PAIR
    q_args = [q] * (N_PAIR * (N_GROUPS - 1)) + [qm] * N_PAIR
    return pl.pallas_call(
        _attn_prompt_body,
        grid=(bsz, KV_HEADS),
        in_specs=q_specs + k_specs + v_specs + bias_specs,
        out_specs=pl.BlockSpec((1, t, wq), lambda b, h: (b, 0, h)),
        out_shape=jax.ShapeDtypeStruct((bsz, t, KV_HEADS * wq), F32),
        scratch_shapes=[pltpu.VMEM((slots, t, LANES), F32)] * 2 + [pltpu.VMEM((2, t, LANES), BF)] * 2,
        compiler_params=_params("parallel", "arbitrary"),
        name="attn_prompt",
    )(*q_args, *kvs, kvm, *kvs, kvm, bias_rest, bias_first)


def _merge_groups(o_parts, lse_parts):
    mx = jnp.maximum(jnp.maximum(lse_parts[0], lse_parts[1]), lse_parts[2])
    es = [jnp.exp2(l - mx) for l in lse_parts]
    num = es[0] * o_parts[0] + es[1] * o_parts[1] + es[2] * o_parts[2]
    return num / (es[0] + es[1] + es[2])


def _attn_sample_body(qp_ref, n0, n1, n2, c0, c1, c2, bc0, bc1, bc2, bn_ref, o_ref, w0, w1, w2):
    n_new = n0.shape[1]
    half = KV_HEADS * HEAD_DIM
    lane = lax.broadcasted_iota(jnp.int32, (1, half), 1)
    tail_lane = lax.broadcasted_iota(jnp.int32, (1, LANES), 1)
    o_parts, lse_parts = [], []
    for g, (new_ref, c_ref, bc_ref, w_ref) in enumerate(((n0, c0, bc0, w0), (n1, c1, bc1, w1), (n2, c2, bc2, w2))):
        w = c_ref.shape[2]
        new = new_ref[0]
        newt = jnp.concatenate([new, jnp.zeros((LANES - n_new, 2 * half), F32)], axis=0).T
        ct = c_ref[0]
        shifted = pltpu.roll(ct, w - n_new, axis=1)
        tail = jnp.where(tail_lane >= LANES - n_new, pltpu.roll(newt, LANES - n_new, axis=1), shifted[:, w - LANES:])
        w_ref[0] = tail if w == LANES else jnp.concatenate([shifted[:, :w - LANES], tail], axis=1)
        kc, vc = ct[:half].astype(BF), ct[half:].astype(BF)
        kn, vn = newt[:half].astype(BF), newt[half:].astype(BF)
        o_g = jnp.zeros((Q_PER_KV * n_new, half), F32)
        l_g = jnp.zeros((Q_PER_KV * n_new, half), F32)
        for h in range(KV_HEADS):
            qp = qp_ref[0, g, h]
            s_c = _dot(qp, kc) + bc_ref[h]
            s_n = _dot(qp, kn) + bn_ref[g, h]
            m = jnp.maximum(jnp.max(s_c, axis=-1, keepdims=True), jnp.max(s_n, axis=-1, keepdims=True))
            p_c = jnp.exp2(s_c - m)
            p_n = jnp.exp2(s_n - m)
            l = jnp.sum(p_c, axis=-1, keepdims=True) + jnp.sum(p_n, axis=-1, keepdims=True)
            o = (_dot_nt(p_c.astype(BF), vc) + _dot_nt(p_n.astype(BF), vn)) / l
            mine = (lane >= h * HEAD_DIM) & (lane < (h + 1) * HEAD_DIM)
            o_g = jnp.where(mine, o, o_g)
            l_g = jnp.where(mine, m + jnp.log2(l), l_g)
        o_parts.append(o_g)
        lse_parts.append(l_g)
    o_ref[0] = _merge_groups(o_parts, lse_parts)


def _attn_sample(qpad, new_kvs, caches, bias_c, bias_n):
    bsz = qpad.shape[0]
    n_new = new_kvs[0].shape[1]
    rows = Q_PER_KV * n_new
    half = KV_HEADS * HEAD_DIM
    per_b = lambda shape: pl.BlockSpec((1,) + shape[1:], lambda b: (b,) + (0,) * (len(shape) - 1))
    cache_specs = [per_b(c.shape) for c in caches]
    return pl.pallas_call(
        _attn_sample_body,
        grid=(bsz,),
        in_specs=[per_b(qpad.shape)] + [per_b(a.shape) for a in new_kvs] + cache_specs
                 + [_const_spec(b.shape) for b in bias_c] + [_const_spec(bias_n.shape)],
        out_specs=[per_b((bsz, rows, half))] + cache_specs,
        out_shape=[jax.ShapeDtypeStruct((bsz, rows, half), F32)]
                  + [jax.ShapeDtypeStruct(c.shape, F32) for c in caches],
        compiler_params=_params("parallel"),
        name="attn_sample",
    )(qpad, *new_kvs, *caches, *bias_c, bias_n)


def _t5_buckets(dist):
    d = np.asarray(dist)
    large = MAX_EXACT + (np.log(np.maximum(d, 1) / MAX_EXACT) / np.log(MAX_DISTANCE / MAX_EXACT)
                         * (N_BUCKETS - MAX_EXACT)).astype(np.int64)
    large = np.minimum(large, N_BUCKETS - 1)
    return np.where(d < MAX_EXACT, d, large).astype(np.int32)


def _group_bias(rel_bias, g):
    bk = _t5_buckets(DILATIONS[g] * np.arange(N_KEYS))
    return rel_bias[bk][:, g * HEADS_PER_GROUP:(g + 1) * HEADS_PER_GROUP].T.astype(F32) * LOG2E


def _prompt_bias_tables(rel_bias):
    period = 2 * QB
    rest, first = [], []
    for g in range(N_GROUPS):
        bv = _group_bias(rel_bias, g)
        row0 = jnp.concatenate([bv[:, ::-1], jnp.full((HEADS_PER_GROUP, period - N_KEYS), NEG, F32)], axis=1)
        wrap = jnp.concatenate([row0, row0, row0[:, :1]], axis=1)
        skew = jnp.broadcast_to(wrap[:, None, :], (HEADS_PER_GROUP, QB, 2 * period + 1))
        skew = skew.reshape(HEADS_PER_GROUP, -1)[:, :QB * 2 * period].reshape(HEADS_PER_GROUP, QB, 2 * period)
        tab = skew[:, :, period:]
        for out, part in ((rest, tab), (first, tab[:, :, QB:])):
            nk = part.shape[-1]
            pairs = part.reshape(HEADS_PER_GROUP // 2, 2, QB, nk).transpose(0, 2, 1, 3)
            out.append(pairs.reshape(HEADS_PER_GROUP // 2 * QB, 2 * nk))
    return jnp.stack(rest, axis=0), jnp.stack(first, axis=0)


def _sample_bias_tables(rel_bias, n_new):
    tabs_c, tabs_n = [], []
    for g in range(N_GROUPS):
        w, d = WINDOWS[g], DILATIONS[g]
        bv = _group_bias(rel_bias, g)
        fill = jnp.full(bv.shape, NEG, F32)
        dil = jnp.stack([bv] + [fill] * (d - 1), axis=-1).reshape(HEADS_PER_GROUP, N_KEYS * d)[:, :w + 1]
        ext = jnp.concatenate([jnp.full((HEADS_PER_GROUP, LANES - 1), NEG, F32), dil,
                               jnp.full((HEADS_PER_GROUP, n_new - 1), NEG, F32)], axis=1)
        rev = ext[:, ::-1]
        rows = [rev[:, n_new - 1 - r:n_new - 1 - r + w + LANES] for r in range(n_new)]
        tab = jnp.stack(rows, axis=1).reshape(KV_HEADS, Q_PER_KV * n_new, w + LANES)
        tabs_c.append(tab[:, :, :w])
        tabs_n.append(tab[:, :, w:])
    return tabs_c, jnp.stack(tabs_n, axis=0)


def _prep_weights(w_in_a, w_a2, b_a, w_o_a, w_kv, w_q_b, w_o_b, w_gate_up, w_down):
    n_main = 2 * GLA_DK + 2 * GLA_DV
    w_in = w_in_a[0]
    wa = jnp.pad(w_in[:, n_main:], ((0, 0), (0, LANES - GATE_RANK)))
    wa2 = jnp.pad(w_a2[0], ((0, LANES - GATE_RANK), (0, 0)))
    return dict(
        wm=w_in.astype(BF), wa=wa.astype(BF), wa2=wa2.astype(BF), ba=b_a[0][None, :],
        wo_a=w_o_a[0].astype(BF), wkv=w_kv.astype(BF),
        wq=(w_q_b[0] * (HEAD_DIM ** -0.5 * LOG2E)).astype(BF),
        wo_b=w_o_b[0].astype(BF),
        wgu=[w_gate_up[l].astype(BF) for l in range(2)], wd=[w_down[l].astype(BF) for l in range(2)])


def _from_positions_last(a):
    return jnp.transpose(a.reshape(a.shape[0], 2, KV_HEADS, HEAD_DIM, a.shape[-1]), (0, 4, 1, 2, 3))


def _layer0(x, s0, norm_g, g_onorm, wts):
    bsz, t, _ = x.shape
    x2 = x.reshape(bsz * t, D_MODEL)
    v_dtype = BF if t % GLA_CHUNK == 0 else F32
    q, k, v, r, g = _gla_in(x2, norm_g[0], wts["wm"], wts["wa"], wts["wa2"], wts["ba"], v_dtype)
    sh = lambda a: a.reshape(bsz, t, a.shape[-1])
    o, st = _gla(sh(q), sh(k), sh(v), sh(g), s0)
    h = _mix_ffn([o.reshape(bsz * t, GLA_DV), r], x2, g_onorm, wts["wo_a"], norm_g[0], wts["wgu"][0],
                 wts["wd"][0])
    return h, st


def kernel(x_prompt, x_sample, state_gla, cache_win1, cache_win2, cache_win3, norm_g, w_in_a, w_a2, b_a,
           g_onorm, w_o_a, g_kv, w_kv, w_q_b, w_o_b, rel_bias, w_gate_up, w_down):
    wts = _prep_weights(w_in_a, w_a2, b_a, w_o_a, w_kv, w_q_b, w_o_b, w_gate_up, w_down)
    gkv = g_kv[None, :]

    bp, tp, _ = x_prompt.shape
    s0p = jnp.zeros((bp, GLA_HEADS, DKH, DVH), F32)
    h_p, gla_p = _layer0(x_prompt, s0p, norm_g, g_onorm, wts)
    *kv_p, q_p, kvt_last, qm_p, kvm_p = _qkv_seq(h_p, gkv, norm_g[1], wts["wkv"], wts["wq"], tp)
    kv_p = [a.reshape(bp, tp, KV_WIDTH) for a in kv_p]
    o_p = _attn_prompt(q_p.reshape(bp, tp, -1), kv_p, qm_p, kvm_p, *_prompt_bias_tables(rel_bias))
    y_p = _mix_ffn([o_p.reshape(bp * tp, D_MODEL)], h_p, None, wts["wo_b"], norm_g[1], wts["wgu"][1],
                   wts["wd"][1])
    win_p = [kv_p[g][:, tp - min(WINDOWS[g], tp):].reshape(bp, -1, 2, KV_HEADS, HEAD_DIM)
             for g in range(N_GROUPS - 1)]
    assert WINDOWS[-1] >= tp
    win_p.append(_from_positions_last(kvt_last))

    bs, ts, _ = x_sample.shape
    h_s, gla_s = _layer0(x_sample, state_gla[0], norm_g, g_onorm, wts)
    *kv_s, q_s = _qkv(h_s, gkv, norm_g[1], wts["wkv"], wts["wq"])
    q6 = q_s.astype(BF).reshape(bs, ts, N_GROUPS, KV_HEADS, Q_PER_KV, HEAD_DIM).transpose(0, 2, 3, 4, 1, 5)
    q6 = q6.reshape(bs, N_GROUPS, KV_HEADS, Q_PER_KV * ts, HEAD_DIM)
    qpad = jnp.stack([jnp.pad(q6[:, :, h], ((0, 0), (0, 0), (0, 0), (h * HEAD_DIM, (KV_HEADS - 1 - h) * HEAD_DIM)))
                      for h in range(KV_HEADS)], axis=2)
    caches = [jnp.transpose(c, (0, 2, 3, 4, 1)).reshape(bs, KV_WIDTH, c.shape[1])
              for c in (cache_win1, cache_win2, cache_win3)]
    bias_c, bias_n = _sample_bias_tables(rel_bias, ts)
    o_s, *win_s = _attn_sample(qpad, [a.reshape(bs, ts, KV_WIDTH) for a in kv_s], caches, bias_c, bias_n)
    o_s = o_s.reshape(bs, Q_PER_KV, ts, KV_HEADS, HEAD_DIM).transpose(0, 2, 3, 1, 4).reshape(bs * ts, D_MODEL)
    y_s = _mix_ffn([o_s], h_s, None, wts["wo_b"], norm_g[1], wts["wgu"][1], wts["wd"][1])
    win_s = [_from_positions_last(w) for w in win_s]

    return (y_p.reshape(bp, tp, D_MODEL), y_s.reshape(bs, ts, D_MODEL), gla_p[None], win_p[0], win_p[1],
            win_p[2], gla_s[None], win_s[0], win_s[1], win_s[2])
```

```python
import functools

import numpy as np
import jax
import jax.numpy as jnp
from jax import lax
from jax.experimental import pallas as pl
from jax.experimental.pallas import tpu as pltpu

BF = jnp.bfloat16
F32 = jnp.float32

D_MODEL = 1024
GLA_HEADS = 4
GLA_DK = 512
GLA_DV = 1024
DKH = GLA_DK // GLA_HEADS
DVH = GLA_DV // GLA_HEADS
GATE_RANK = 16
GATE_TAU = 16.0
GLA_CHUNK = 64
GLA_SUB = 16
GLA_UNROLL = 4
GLA_TIME_BLOCK = 1024
WINDOWS = (128, 512, 2048)
DILATIONS = (1, 4, 16)
N_GROUPS = 3
HEAD_DIM = 64
HEADS_PER_GROUP = 16
KV_HEADS = 4
Q_PER_KV = 4
N_PAIR = Q_PER_KV // 2
N_KEYS = 129
N_BUCKETS = 32
MAX_EXACT = 16
MAX_DISTANCE = 2048
D_FF = 2816
FF_SPLITS = (0, 1536, D_FF)
EPS = 1e-6
NEG = -1e30
LOG2E = 1.4426950408889634
QB = 128
LANES = 128
ROW_ALIGN = 16
DEINT = 4
KV_WIDTH = 2 * KV_HEADS * HEAD_DIM
VMEM_LIMIT_BYTES = 56 * 1024 * 1024


def _dot(a, b):
    return jnp.dot(a, b, preferred_element_type=F32)


def _dot_nt(a, b):
    return lax.dot_general(a, b, (((1,), (1,)), ((), ())), preferred_element_type=F32)


def _rms(x, g):
    return x * lax.rsqrt(jnp.mean(x * x, axis=-1, keepdims=True) + EPS) * g


def _sigmoid(x):
    return 1.0 / (1.0 + jnp.exp(-x))


def _const_spec(shape):
    nd = len(shape)
    return pl.BlockSpec(shape, lambda *_: (0,) * nd, pipeline_mode=pl.Buffered(1))


def _params(*sem):
    return pltpu.CompilerParams(dimension_semantics=sem, vmem_limit_bytes=VMEM_LIMIT_BYTES)


def _row_tile(n, want):
    tm = min(n, want)
    assert n % tm == 0
    return tm


def _row_halves(tm):
    n_sub = 2 if tm % (2 * ROW_ALIGN) == 0 else 1
    return [slice(i * tm // n_sub, (i + 1) * tm // n_sub) for i in range(n_sub)]


def _gla_in_body(x_ref, ng_ref, wm_ref, wa_ref, wa2_ref, ba_ref, q_ref, k_ref, v_ref, r_ref, g_ref):
    subs = _row_halves(x_ref.shape[0])
    xn = [_rms(x_ref[rows, :], ng_ref[0:1, :]).astype(BF) for rows in subs]
    a = [_dot(xi, wa_ref[...]).astype(BF) for xi in xn]
    z = [_dot(ai, wa2_ref[...]) + ba_ref[...] for ai in a]
    for rows, zi in zip(subs, z):
        g_ref[rows, :] = (jnp.minimum(zi, 0.0) - jnp.log(1.0 + jnp.exp(-jnp.abs(zi)))) * (1.0 / GATE_TAU)
    for rows, xi in zip(subs, xn):
        q_ref[rows, :] = _dot(xi, wm_ref[:, 0:GLA_DK]) * (DKH ** -0.5)
        k_ref[rows, :] = _dot(xi, wm_ref[:, GLA_DK:2 * GLA_DK])
        v_ref[rows, :] = _dot(xi, wm_ref[:, 2 * GLA_DK:2 * GLA_DK + GLA_DV]).astype(v_ref.dtype)
        r_ref[rows, :] = _dot(xi, wm_ref[:, 2 * GLA_DK + GLA_DV:2 * GLA_DK + 2 * GLA_DV])


def _gla_in(x2, ng, wm, wa, wa2, ba, v_dtype):
    n = x2.shape[0]
    tm = _row_tile(n, 512)
    row = lambda w: pl.BlockSpec((tm, w), lambda i: (i, 0))
    return pl.pallas_call(
        _gla_in_body,
        grid=(n // tm,),
        in_specs=[row(D_MODEL), _const_spec(ng.shape), _const_spec(wm.shape), _const_spec(wa.shape),
                  _const_spec(wa2.shape), _const_spec(ba.shape)],
        out_specs=[row(GLA_DK), row(GLA_DK), row(GLA_DV), row(GLA_DV), row(GLA_DK)],
        out_shape=[jax.ShapeDtypeStruct((n, w), dt) for w, dt in
                   ((GLA_DK, F32), (GLA_DK, F32), (GLA_DV, v_dtype), (GLA_DV, F32), (GLA_DK, F32))],
        compiler_params=_params("parallel"),
        name="gla_in",
    )(x2, ng, wm, wa, wa2, ba)


def _gla_body(c_real, q_ref, k_ref, v_ref, g_ref, s0_ref, o_ref, st_ref, s_scr):
    t = pl.program_id(1)
    c = min(GLA_CHUNK, max(c_real, GLA_SUB))
    nsb = c // GLA_SUB
    n_chunks = q_ref.shape[1] // c_real

    @pl.when(t == 0)
    def _():
        s_scr[...] = s0_ref[0]

    gc = min(n_chunks, GLA_UNROLL)
    assert n_chunks % gc == 0
    ri = lax.broadcasted_iota(jnp.int32, (gc * c, gc * c), 0)
    ci = lax.broadcasted_iota(jnp.int32, (gc * c, gc * c), 1)
    tril_bf = jnp.where((ri >= ci) & (ri // c == ci // c), 1.0, 0.0).astype(BF)
    ri2 = lax.broadcasted_iota(jnp.int32, (c, LANES), 0)
    ci2 = lax.broadcasted_iota(jnp.int32, (c, LANES), 1)
    causal = ri2 >= ci2

    def pad_rows(a, rows):
        if a.shape[0] == rows:
            return a
        return jnp.concatenate([a, jnp.zeros((rows - a.shape[0], a.shape[1]), a.dtype)], axis=0)

    heads = range(GLA_HEADS)
    ks = [slice(h * DKH, (h + 1) * DKH) for h in heads]
    vs = [slice(h * DVH, (h + 1) * DVH) for h in heads]
    units = [(j, h) for j in range(gc) for h in heads]

    def group(idx, carry):
        span = gc * c_real
        grows = pl.ds(pl.multiple_of(idx * span, span), span)
        rows = [pl.ds(pl.multiple_of(idx * span + j * c_real, c_real), c_real) for j in range(gc)]
        g_all = pad_rows(g_ref[0, grows, :], gc * c)
        g_hi = g_all.astype(BF)
        g_lo = (g_all - g_hi.astype(F32)).astype(BF)
        b_all = _dot(tril_bf, g_hi) + _dot(tril_bf, g_lo)
        b = {(j, h): b_all[j * c:(j + 1) * c, ks[h]] for j, h in units}
        qh = {(j, h): pad_rows(q_ref[0, rows[j], ks[h]], c) for j, h in units}
        kh = {(j, h): pad_rows(k_ref[0, rows[j], ks[h]], c) for j, h in units}
        vh = {(j, h): pad_rows(v_ref[0, rows[j], vs[h]], LANES) for j, h in units}
        b_last = {u: b[u][c - 1:c, :] for u in units}
        scores = {}
        for u in units:
            qparts, kparts = [], []
            for sbi in range(nsb):
                lo, hi = sbi * GLA_SUB, (sbi + 1) * GLA_SUB
                ref_row = b[u][lo:lo + 1, :]
                qj = (qh[u][lo:] * jnp.exp(b[u][lo:] - ref_row)).astype(BF)
                qparts.append(jnp.concatenate([jnp.zeros((lo, DKH), BF), qj], axis=0) if lo else qj)
                kj = (kh[u][lo:hi] * jnp.exp(ref_row - b[u][lo:hi])).astype(BF)
                pieces = []
                if lo > 0:
                    pieces.append(jnp.zeros((lo, DKH), BF))
                pieces.append(kj)
                pieces.append(jnp.zeros((LANES - hi, DKH), BF))
                kparts.append(jnp.concatenate(pieces, axis=0))
            qcat = jnp.concatenate(qparts, axis=1)
            kcat = jnp.concatenate(kparts, axis=1)
            scores[u] = _dot_nt(qcat, kcat)
        upd, decay = {}, {}
        for u in units:
            k2 = pad_rows(kh[u] * jnp.exp(b_last[u] - b[u]), LANES)
            upd[u] = _dot(k2.T.astype(BF), vh[u].astype(BF))
            col = jnp.broadcast_to(jnp.exp(b_last[u]), (LANES, DKH)).T
            decay[u] = jnp.concatenate([col] * (DVH // LANES), axis=1)
        lhs = {u: jnp.concatenate([(qh[u] * jnp.exp(b[u])).astype(BF),
                                   jnp.where(causal, scores[u], 0.0).astype(BF)], axis=1) for u in units}
        v_bf = {u: vh[u].astype(BF) for u in units}
        st = [s_scr[h] for h in heads]
        for j in range(gc):
            for h in heads:
                o = _dot(lhs[j, h], jnp.concatenate([st[h].astype(BF), v_bf[j, h]], axis=0))
                o_ref[0, rows[j], vs[h]] = o[0:c_real]
            st = [st[h] * decay[j, h] + upd[j, h] for h in heads]
        for h in heads:
            s_scr[h] = st[h]
        return carry

    lax.fori_loop(0, n_chunks // gc, group, 0)

    @pl.when(t == pl.num_programs(1) - 1)
    def _():
        st_ref[0] = s_scr[...]


def _gla(q, k, v, g, s0):
    bsz, t, _ = q.shape
    c_real = min(t, GLA_CHUNK)
    tc = min(t, GLA_TIME_BLOCK)
    assert t % tc == 0 and tc % c_real == 0
    blk = lambda w: pl.BlockSpec((1, tc, w), lambda b, i: (b, i, 0))
    st_spec = pl.BlockSpec((1, GLA_HEADS, DKH, DVH), lambda b, i: (b, 0, 0, 0))
    return pl.pallas_call(
        functools.partial(_gla_body, c_real),
        grid=(bsz, t // tc),
        in_specs=[blk(GLA_DK), blk(GLA_DK), blk(GLA_DV), blk(GLA_DK), st_spec],
        out_specs=[blk(GLA_DV), st_spec],
        out_shape=[jax.ShapeDtypeStruct((bsz, t, GLA_DV), F32),
                   jax.ShapeDtypeStruct((bsz, GLA_HEADS, DKH, DVH), F32)],
        scratch_shapes=[pltpu.VMEM((GLA_HEADS, DKH, DVH), F32)],
        compiler_params=_params("parallel", "arbitrary"),
        name="gla",
    )(q, k, v, g, s0)


def _mix_ffn_body(gated, *refs):
    if gated:
        o_ref, r_ref, x_ref, gon_ref, wo_ref, ng_ref, wgu_ref, wd_ref, out_ref = refs
    else:
        m_ref, x_ref, wo_ref, ng_ref, wgu_ref, wd_ref, out_ref = refs
    subs = _row_halves(x_ref.shape[0])
    n_sub = len(subs)
    if gated:
        def mixed(rows):
            o = o_ref[rows, :]
            r = r_ref[rows, :]
            on = jnp.concatenate(
                [_rms(o[:, h * DVH:(h + 1) * DVH], gon_ref[...]) for h in range(GLA_HEADS)], axis=1)
            return on * (r * _sigmoid(r))
        m = [mixed(rows) for rows in subs]
    else:
        m = [m_ref[rows, :] for rows in subs]
    y = [_dot(mi.astype(BF), wo_ref[...]) for mi in m]
    h1 = [x_ref[rows, :] + _rms(yi, ng_ref[1:2, :]) for rows, yi in zip(subs, y)]
    u = [_rms(hi, ng_ref[2:3, :]).astype(BF) for hi in h1]
    f = [None] * n_sub
    for lo, hi in zip(FF_SPLITS[:-1], FF_SPLITS[1:]):
        gate = [_dot(ui, wgu_ref[:, lo:hi]) for ui in u]
        up = [_dot(ui, wgu_ref[:, D_FF + lo:D_FF + hi]) for ui in u]
        part = [_dot((gi * _sigmoid(gi) * pi).astype(BF), wd_ref[lo:hi, :]) for gi, pi in zip(gate, up)]
        f = [pi if fi is None else fi + pi for fi, pi in zip(f, part)]
    for rows, hi, fi in zip(subs, h1, f):
        out_ref[rows, :] = hi + _rms(fi, ng_ref[3:4, :])


def _mix_ffn(mix_inputs, x2, gon, wo, ng, wgu, wd):
    n = x2.shape[0]
    tm = _row_tile(n, 512)
    row = pl.BlockSpec((tm, D_MODEL), lambda i: (i, 0))
    gated = gon is not None
    args = list(mix_inputs) + [x2] + ([gon] if gated else []) + [wo, ng, wgu, wd]
    in_specs = [row] * (len(mix_inputs) + 1) + [_const_spec(a.shape) for a in args[len(mix_inputs) + 1:]]
    return pl.pallas_call(
        functools.partial(_mix_ffn_body, gated),
        grid=(n // tm,),
        in_specs=in_specs,
        out_specs=row,
        out_shape=jax.ShapeDtypeStruct((n, D_MODEL), F32),
        compiler_params=_params("parallel"),
        name="mix_ffn_gated" if gated else "mix_ffn",
    )(*args)


def _normed_halves(h_ref, gkv_ref, ng_ref):
    subs = _row_halves(h_ref.shape[0])
    hn = []
    for rows in subs:
        h = h_ref[rows, :]
        hn.append(h * lax.rsqrt(jnp.mean(h * h, axis=-1, keepdims=True) + EPS))
    hkv = [(hi * gkv_ref[...]).astype(BF) for hi in hn]
    hq = [(hi * ng_ref[0:1, :]).astype(BF) for hi in hn]
    return subs, hkv, hq


def _qkv_body(h_ref, gkv_ref, ng_ref, wkv_ref, wq_ref, kv0_ref, kv1_ref, kv2_ref, q_ref):
    subs, hkv, hq = _normed_halves(h_ref, gkv_ref, ng_ref)
    for rows, hi in zip(subs, hkv):
        for g, kv_ref in enumerate((kv0_ref, kv1_ref, kv2_ref)):
            kv_ref[rows, :] = _dot(hi, wkv_ref[:, g * KV_WIDTH:(g + 1) * KV_WIDTH])
    for rows, hi in zip(subs, hq):
        q_ref[rows, :] = _dot(hi, wq_ref[...])


def _qkv_seq_body(h_ref, gkv_ref, ng_ref, wkv_ref, wq_ref, kv0_ref, kv1_ref, q_ref, kvt_ref, qm_ref, kvm_ref,
                  stage_scr):
    last = N_GROUPS - 1
    wq_g = HEADS_PER_GROUP * HEAD_DIM
    cols = lambda c: slice(c * LANES, (c + 1) * LANES)
    subs, hkv, hq = _normed_halves(h_ref, gkv_ref, ng_ref)
    for rows, hi in zip(subs, hkv):
        for g, kv_ref in enumerate((kv0_ref, kv1_ref)):
            kv = _dot(hi, wkv_ref[:, g * KV_WIDTH:(g + 1) * KV_WIDTH])
            for c in range(KV_WIDTH // LANES):
                kv_ref[0, c, rows, :] = kv[:, cols(c)]
        kv = _dot(hi, wkv_ref[:, last * KV_WIDTH:])
        kvt_ref[0, :, rows] = kv.T
        for c in range(KV_WIDTH // LANES):
            stage_scr[wq_g // LANES + c, rows, :] = kv[:, cols(c)]
    for rows, hi in zip(subs, hq):
        q = _dot(hi, wq_ref[:, :last * wq_g])
        for c in range(last * wq_g // LANES):
            q_ref[0, c, rows, :] = q[:, cols(c)]
        q_last = _dot(hi, wq_ref[:, last * wq_g:])
        for c in range(wq_g // LANES):
            stage_scr[c, rows, :] = q_last[:, cols(c)]
    per_class = h_ref.shape[0] // DEINT
    for a in range(DEINT):
        picked = pl.ds(a, per_class, stride=DEINT)
        for c in range(wq_g // LANES):
            qm_ref[0, a, c] = stage_scr[c, picked, :]
        for c in range(KV_WIDTH // LANES):
            kvm_ref[0, a, c] = stage_scr[wq_g // LANES + c, picked, :]


def _qkv(h2, gkv, ng, wkv, wq):
    n = h2.shape[0]
    tm = _row_tile(n, 512)
    row = lambda w: pl.BlockSpec((tm, w), lambda i: (i, 0))
    nq = wq.shape[1]
    return pl.pallas_call(
        _qkv_body,
        grid=(n // tm,),
        in_specs=[row(D_MODEL), _const_spec(gkv.shape), _const_spec(ng.shape), _const_spec(wkv.shape),
                  _const_spec(wq.shape)],
        out_specs=[row(KV_WIDTH)] * N_GROUPS + [row(nq)],
        out_shape=[jax.ShapeDtypeStruct((n, KV_WIDTH), F32)] * N_GROUPS + [jax.ShapeDtypeStruct((n, nq), F32)],
        compiler_params=_params("parallel"),
        name="qkv",
    )(h2, gkv, ng, wkv, wq)


def _qkv_seq(h2, gkv, ng, wkv, wq, seq_len):
    n = h2.shape[0]
    tm = _row_tile(n, 512)
    assert seq_len % tm == 0 and tm % (DEINT * 8) == 0
    bsz, per_seq = n // seq_len, seq_len // tm
    wq_g = HEADS_PER_GROUP * HEAD_DIM
    n_kv, n_q = KV_WIDTH // LANES, wq_g // LANES
    row = lambda w: pl.BlockSpec((tm, w), lambda i: (i, 0))
    blocks = lambda nb: pl.BlockSpec((1, nb, tm, LANES), lambda i: (i // per_seq, 0, i % per_seq, 0))
    split = lambda nb: pl.BlockSpec((1, DEINT, nb, tm // DEINT, LANES),
                                    lambda i: (i // per_seq, 0, 0, i % per_seq, 0))
    return pl.pallas_call(
        _qkv_seq_body,
        grid=(n // tm,),
        in_specs=[row(D_MODEL), _const_spec(gkv.shape), _const_spec(ng.shape), _const_spec(wkv.shape),
                  _const_spec(wq.shape)],
        out_specs=[blocks(n_kv), blocks(n_kv), blocks((N_GROUPS - 1) * n_q),
                   pl.BlockSpec((1, KV_WIDTH, tm), lambda i: (i // per_seq, 0, i % per_seq)),
                   split(n_q), split(n_kv)],
        out_shape=[jax.ShapeDtypeStruct((bsz, n_kv, seq_len, LANES), F32),
                   jax.ShapeDtypeStruct((bsz, n_kv, seq_len, LANES), F32),
                   jax.ShapeDtypeStruct((bsz, (N_GROUPS - 1) * n_q, seq_len, LANES), F32),
                   jax.ShapeDtypeStruct((bsz, KV_WIDTH, seq_len), F32),
                   jax.ShapeDtypeStruct((bsz, DEINT, n_q, seq_len // DEINT, LANES), F32),
                   jax.ShapeDtypeStruct((bsz, DEINT, n_kv, seq_len // DEINT, LANES), F32)],
        scratch_shapes=[pltpu.VMEM(((wq_g + KV_WIDTH) // LANES, tm, LANES), F32)],
        compiler_params=_params("parallel"),
        name="qkv_seq",
    )(h2, gkv, ng, wkv, wq)


def _attn_prompt_body(q00, q01, q10, q11, q20, q21, k0, k1, k2, v0, v1, v2, bias_rest_ref, bias_first_ref,
                      o_ref, o_scr, lse_scr, k_scr, v_scr):
    t_len = k0.shape[1]
    par = pl.program_id(1) % 2
    lane = lax.broadcasted_iota(jnp.int32, (1, LANES), 1)
    low = lane < HEAD_DIM
    same = (lane >= HEAD_DIM).astype(jnp.int32) == par

    def strided_rows(start, d):
        if d > 1:
            return pl.ds(start, QB, stride=d)
        return pl.ds(start if isinstance(start, int) else pl.multiple_of(start, QB), QB)

    def block_rows(blk, n=1):
        return pl.ds(pl.multiple_of(blk * QB, QB), n * QB)

    def load_block(g, ref, d, nblk, blk):
        rho = blk // nblk
        i = blk - rho * nblk
        if g == N_GROUPS - 1:
            inner = d // DEINT
            return ref[0, rho % DEINT, pl.ds(rho // DEINT + inner * QB * i, QB, stride=inner), :]
        return ref[0, strided_rows(rho + d * QB * i, d), :]

    def prepare(g, k_ref, v_ref, d, nblk, blk):
        dst = block_rows(blk)
        for src_ref, dst_scr in ((k_ref, k_scr), (v_ref, v_scr)):
            own = jnp.where(same, load_block(g, src_ref, d, nblk, blk), 0.0)
            dst_scr[par, dst, :] = own.astype(BF)
            dst_scr[1 - par, dst, :] = pltpu.roll(own, HEAD_DIM, axis=1).astype(BF)

    def attend(g, q_refs, d, nblk, blk, with_prev):
        rho = blk // nblk
        rows_q = strided_rows(rho + d * QB * (blk - rho * nblk), d)
        keys = block_rows(blk - 1, 2) if with_prev else block_rows(blk)
        nk = 2 * QB if with_prev else QB
        q = jnp.concatenate([load_block(g, q_refs[p], d, nblk, blk) for p in range(N_PAIR)],
                            axis=0).astype(BF)
        k_cat = jnp.concatenate([k_scr[0, keys, :], k_scr[1, keys, :]], axis=0)
        bias = bias_rest_ref[g] if with_prev else bias_first_ref[g]
        s = _dot_nt(q, k_cat) + bias
        mx, ps = [], []
        for odd in range(2):
            sh = s[:, odd * nk:(odd + 1) * nk]
            mx.append(jnp.max(sh, axis=-1, keepdims=True))
            ps.append(jnp.exp2(sh - mx[odd]).astype(BF))
        ones_lo = jnp.broadcast_to(jnp.where(low, 1.0, 0.0), (nk, LANES)).astype(BF)
        ones_hi = jnp.broadcast_to(jnp.where(low, 0.0, 1.0), (nk, LANES)).astype(BF)
        rhs = jnp.concatenate([jnp.concatenate([v_scr[0, keys, :], ones_lo], axis=1),
                               jnp.concatenate([v_scr[1, keys, :], ones_hi], axis=1)], axis=0)
        res = _dot(jnp.concatenate(ps, axis=1), rhs)
        m = jnp.where(low, mx[0], mx[1])
        for p in range(N_PAIR):
            part = slice(p * QB, (p + 1) * QB)
            o_g, l_g, m_g = res[part, :LANES], res[part, LANES:], m[part]
            if g > 0:
                slot = N_PAIR * (g - 1) + p
                o_scr[slot, rows_q, :] = o_g / l_g
                lse_scr[slot, rows_q, :] = m_g + jnp.log2(l_g)
            else:
                slots = [N_PAIR * other + p for other in range(N_GROUPS - 1)]
                lses = [lse_scr[s_, rows_q, :] for s_ in slots]
                top = jnp.maximum(jnp.maximum(m_g, lses[0]), lses[1])
                e_own = jnp.exp2(m_g - top)
                es = [jnp.exp2(ll - top) for ll in lses]
                num = e_own * o_g + es[0] * o_scr[slots[0], rows_q, :] + es[1] * o_scr[slots[1], rows_q, :]
                den = e_own * l_g + es[0] + es[1]
                o_ref[0, rows_q, p * LANES:(p + 1) * LANES] = num / den

    groups = (((q00, q01), k0, v0), ((q10, q11), k1, v1), ((q20, q21), k2, v2))
    for g in reversed(range(N_GROUPS)):
        q_refs, k_ref, v_ref = groups[g]
        d = DILATIONS[g]
        nblk = t_len // d // QB

        def prep_step(blk, carry, g=g, k_ref=k_ref, v_ref=v_ref, d=d, nblk=nblk):
            prepare(g, k_ref, v_ref, d, nblk, blk)
            return carry

        def first_step(rho, carry, g=g, q_refs=q_refs, d=d, nblk=nblk):
            attend(g, q_refs, d, nblk, rho * nblk, False)
            return carry

        def rest_step(n, carry, g=g, q_refs=q_refs, d=d, nblk=nblk):
            rho = n // (nblk - 1)
            attend(g, q_refs, d, nblk, n + rho + 1, True)
            return carry

        lax.fori_loop(0, d * nblk, prep_step, 0, unroll=8)
        lax.fori_loop(0, d, first_step, 0, unroll=min(d, 8))
        if nblk > 1:
            n_rest = d * (nblk - 1)
            lax.fori_loop(0, n_rest, rest_step, 0, unroll=n_rest // 2 if n_rest % 2 == 0 else n_rest)


def _attn_prompt(q, kvs, qm, kvm, bias_rest, bias_first):
    bsz, _, t, _ = q.shape
    assert t % (QB * DILATIONS[-1]) == 0
    wq = Q_PER_KV * HEAD_DIM
    heads_per_block = LANES // HEAD_DIM
    k_col = lambda h: h // heads_per_block
    v_col = lambda h: KV_HEADS // heads_per_block + h // heads_per_block
    nat = lambda col: pl.BlockSpec((1, None, t, LANES), lambda b, h: (b, col(h), 0, 0))
    split = lambda col: pl.BlockSpec((1, DEINT, None, t // DEINT, LANES), lambda b, h: (b, 0, col(h), 0, 0))
    q_specs = [nat(lambda h, g=g, p=p: (g * KV_HEADS + h) * N_PAIR + p)
               for g in range(N_GROUPS - 1) for p in range(N_PAIR)]
    q_specs += [split(lambda h, p=p: h * N_PAIR + p) for p in range(N_PAIR)]
    k_specs = [nat(k_col)] * (N_GROUPS - 1) + [split(k_col)]
    v_specs = [nat(v_col)] * (N_GROUPS - 1) + [split(v_col)]
    bias_specs = [pl.BlockSpec((N_GROUPS, N_PAIR * QB, tab.shape[-1]), lambda b, h: (0, h, 0))
                  for tab in (bias_rest, bias_first)]
    slots = (N_GROUPS - 1) * N_PAIR
    q_args = [q] * (N_PAIR * (N_GROUPS - 1)) + [qm] * N_PAIR
    return pl.pallas_call(
        _attn_prompt_body,
        grid=(bsz, KV_HEADS),
        in_specs=q_specs + k_specs + v_specs + bias_specs,
        out_specs=pl.BlockSpec((1, t, wq), lambda b, h: (b, 0, h)),
        out_shape=jax.ShapeDtypeStruct((bsz, t, KV_HEADS * wq), F32),
        scratch_shapes=[pltpu.VMEM((slots, t, LANES), F32)] * 2 + [pltpu.VMEM((2, t, LANES), BF)] * 2,
        compiler_params=_params("parallel", "arbitrary"),
        name="attn_prompt",
    )(*q_args, *kvs, kvm, *kvs, kvm, bias_rest, bias_first)


def _merge_groups(o_parts, lse_parts):
    mx = jnp.maximum(jnp.maximum(lse_parts[0], lse_parts[1]), lse_parts[2])
    es = [jnp.exp2(l - mx) for l in lse_parts]
    num = es[0] * o_parts[0] + es[1] * o_parts[1] + es[2] * o_parts[2]
    return num / (es[0] + es[1] + es[2])


def _attn_sample_body(qp_ref, n0, n1, n2, c0, c1, c2, bc0, bc1, bc2, bn_ref, o_ref, w0, w1, w2):
    n_new = n0.shape[1]
    half = KV_HEADS * HEAD_DIM
    lane = lax.broadcasted_iota(jnp.int32, (1, half), 1)
    tail_lane = lax.broadcasted_iota(jnp.int32, (1, LANES), 1)
    o_parts, lse_parts = [], []
    for g, (new_ref, c_ref, bc_ref, w_ref) in enumerate(((n0, c0, bc0, w0), (n1, c1, bc1, w1), (n2, c2, bc2, w2))):
        w = c_ref.shape[2]
        new = new_ref[0]
        newt = jnp.concatenate([new, jnp.zeros((LANES - n_new, 2 * half), F32)], axis=0).T
        ct = c_ref[0]
        shifted = pltpu.roll(ct, w - n_new, axis=1)
        tail = jnp.where(tail_lane >= LANES - n_new, pltpu.roll(newt, LANES - n_new, axis=1), shifted[:, w - LANES:])
        w_ref[0] = tail if w == LANES else jnp.concatenate([shifted[:, :w - LANES], tail], axis=1)
        kc, vc = ct[:half].astype(BF), ct[half:].astype(BF)
        kn, vn = newt[:half].astype(BF), newt[half:].astype(BF)
        o_g = jnp.zeros((Q_PER_KV * n_new, half), F32)
        l_g = jnp.zeros((Q_PER_KV * n_new, half), F32)
        for h in range(KV_HEADS):
            qp = qp_ref[0, g, h]
            s_c = _dot(qp, kc) + bc_ref[h]
            s_n = _dot(qp, kn) + bn_ref[g, h]
            m = jnp.maximum(jnp.max(s_c, axis=-1, keepdims=True), jnp.max(s_n, axis=-1, keepdims=True))
            p_c = jnp.exp2(s_c - m)
            p_n = jnp.exp2(s_n - m)
            l = jnp.sum(p_c, axis=-1, keepdims=True) + jnp.sum(p_n, axis=-1, keepdims=True)
            o = (_dot_nt(p_c.astype(BF), vc) + _dot_nt(p_n.astype(BF), vn)) / l
            mine = (lane >= h * HEAD_DIM) & (lane < (h + 1) * HEAD_DIM)
            o_g = jnp.where(mine, o, o_g)
            l_g = jnp.where(mine, m + jnp.log2(l), l_g)
        o_parts.append(o_g)
        lse_parts.append(l_g)
    o_ref[0] = _merge_groups(o_parts, lse_parts)


def _attn_sample(qpad, new_kvs, caches, bias_c, bias_n):
    bsz = qpad.shape[0]
    n_new = new_kvs[0].shape[1]
    rows = Q_PER_KV * n_new
    half = KV_HEADS * HEAD_DIM
    per_b = lambda shape: pl.BlockSpec((1,) + shape[1:], lambda b: (b,) + (0,) * (len(shape) - 1))
    cache_specs = [per_b(c.shape) for c in caches]
    return pl.pallas_call(
        _attn_sample_body,
        grid=(bsz,),
        in_specs=[per_b(qpad.shape)] + [per_b(a.shape) for a in new_kvs] + cache_specs
                 + [_const_spec(b.shape) for b in bias_c] + [_const_spec(bias_n.shape)],
        out_specs=[per_b((bsz, rows, half))] + cache_specs,
        out_shape=[jax.ShapeDtypeStruct((bsz, rows, half), F32)]
                  + [jax.ShapeDtypeStruct(c.shape, F32) for c in caches],
        compiler_params=_params("parallel"),
        name="attn_sample",
    )(qpad, *new_kvs, *caches, *bias_c, bias_n)


def _t5_buckets(dist):
    d = np.asarray(dist)
    large = MAX_EXACT + (np.log(np.maximum(d, 1) / MAX_EXACT) / np.log(MAX_DISTANCE / MAX_EXACT)
                         * (N_BUCKETS - MAX_EXACT)).astype(np.int64)
    large = np.minimum(large, N_BUCKETS - 1)
    return np.where(d < MAX_EXACT, d, large).astype(np.int32)


def _group_bias(rel_bias, g):
    bk = _t5_buckets(DILATIONS[g] * np.arange(N_KEYS))
    return rel_bias[bk][:, g * HEADS_PER_GROUP:(g + 1) * HEADS_PER_GROUP].T.astype(F32) * LOG2E


def _prompt_bias_tables(rel_bias):
    period = 2 * QB
    rest, first = [], []
    for g in range(N_GROUPS):
        bv = _group_bias(rel_bias, g)
        row0 = jnp.concatenate([bv[:, ::-1], jnp.full((HEADS_PER_GROUP, period - N_KEYS), NEG, F32)], axis=1)
        wrap = jnp.concatenate([row0, row0, row0[:, :1]], axis=1)
        skew = jnp.broadcast_to(wrap[:, None, :], (HEADS_PER_GROUP, QB, 2 * period + 1))
        skew = skew.reshape(HEADS_PER_GROUP, -1)[:, :QB * 2 * period].reshape(HEADS_PER_GROUP, QB, 2 * period)
        tab = skew[:, :, period:]
        for out, part in ((rest, tab), (first, tab[:, :, QB:])):
            nk = part.shape[-1]
            pairs = part.reshape(HEADS_PER_GROUP // 2, 2, QB, nk).transpose(0, 2, 1, 3)
            out.append(pairs.reshape(HEADS_PER_GROUP // 2 * QB, 2 * nk))
    return jnp.stack(rest, axis=0), jnp.stack(first, axis=0)


def _sample_bias_tables(rel_bias, n_new):
    tabs_c, tabs_n = [], []
    for g in range(N_GROUPS):
        w, d = WINDOWS[g], DILATIONS[g]
        bv = _group_bias(rel_bias, g)
        fill = jnp.full(bv.shape, NEG, F32)
        dil = jnp.stack([bv] + [fill] * (d - 1), axis=-1).reshape(HEADS_PER_GROUP, N_KEYS * d)[:, :w + 1]
        ext = jnp.concatenate([jnp.full((HEADS_PER_GROUP, LANES - 1), NEG, F32), dil,
                               jnp.full((HEADS_PER_GROUP, n_new - 1), NEG, F32)], axis=1)
        rev = ext[:, ::-1]
        rows = [rev[:, n_new - 1 - r:n_new - 1 - r + w + LANES] for r in range(n_new)]
        tab = jnp.stack(rows, axis=1).reshape(KV_HEADS, Q_PER_KV * n_new, w + LANES)
        tabs_c.append(tab[:, :, :w])
        tabs_n.append(tab[:, :, w:])
    return tabs_c, jnp.stack(tabs_n, axis=0)


def _prep_weights(w_in_a, w_a2, b_a, w_o_a, w_kv, w_q_b, w_o_b, w_gate_up, w_down):
    n_main = 2 * GLA_DK + 2 * GLA_DV
    w_in = w_in_a[0]
    wa = jnp.pad(w_in[:, n_main:], ((0, 0), (0, LANES - GATE_RANK)))
    wa2 = jnp.pad(w_a2[0], ((0, LANES - GATE_RANK), (0, 0)))
    return dict(
        wm=w_in.astype(BF), wa=wa.astype(BF), wa2=wa2.astype(BF), ba=b_a[0][None, :],
        wo_a=w_o_a[0].astype(BF), wkv=w_kv.astype(BF),
        wq=(w_q_b[0] * (HEAD_DIM ** -0.5 * LOG2E)).astype(BF),
        wo_b=w_o_b[0].astype(BF),
        wgu=[w_gate_up[l].astype(BF) for l in range(2)], wd=[w_down[l].astype(BF) for l in range(2)])


def _from_positions_last(a):
    return jnp.transpose(a.reshape(a.shape[0], 2, KV_HEADS, HEAD_DIM, a.shape[-1]), (0, 4, 1, 2, 3))


def _layer0(x, s0, norm_g, g_onorm, wts):
    bsz, t, _ = x.shape
    x2 = x.reshape(bsz * t, D_MODEL)
    v_dtype = BF if t % GLA_CHUNK == 0 else F32
    q, k, v, r, g = _gla_in(x2, norm_g[0], wts["wm"], wts["wa"], wts["wa2"], wts["ba"], v_dtype)
    sh = lambda a: a.reshape(bsz, t, a.shape[-1])
    o, st = _gla(sh(q), sh(k), sh(v), sh(g), s0)
    h = _mix_ffn([o.reshape(bsz * t, GLA_DV), r], x2, g_onorm, wts["wo_a"], norm_g[0], wts["wgu"][0],
                 wts["wd"][0])
    return h, st


def kernel(x_prompt, x_sample, state_gla, cache_win1, cache_win2, cache_win3, norm_g, w_in_a, w_a2, b_a,
           g_onorm, w_o_a, g_kv, w_kv, w_q_b, w_o_b, rel_bias, w_gate_up, w_down):
    wts = _prep_weights(w_in_a, w_a2, b_a, w_o_a, w_kv, w_q_b, w_o_b, w_gate_up, w_down)
    gkv = g_kv[None, :]

    bp, tp, _ = x_prompt.shape
    s0p = jnp.zeros((bp, GLA_HEADS, DKH, DVH), F32)
    h_p, gla_p = _layer0(x_prompt, s0p, norm_g, g_onorm, wts)
    *kv_p, q_p, kvt_last, qm_p, kvm_p = _qkv_seq(h_p, gkv, norm_g[1], wts["wkv"], wts["wq"], tp)
    o_p = _attn_prompt(q_p, kv_p, qm_p, kvm_p, *_prompt_bias_tables(rel_bias))
    y_p = _mix_ffn([o_p.reshape(bp * tp, D_MODEL)], h_p, None, wts["wo_b"], norm_g[1], wts["wgu"][1],
                   wts["wd"][1])
    win_p = [kv_p[g][:, :, tp - min(WINDOWS[g], tp):].transpose(0, 2, 1, 3).reshape(bp, -1, 2, KV_HEADS, HEAD_DIM)
             for g in range(N_GROUPS - 1)]
    assert WINDOWS[-1] >= tp
    win_p.append(_from_positions_last(kvt_last))

    bs, ts, _ = x_sample.shape
    h_s, gla_s = _layer0(x_sample, state_gla[0], norm_g, g_onorm, wts)
    *kv_s, q_s = _qkv(h_s, gkv, norm_g[1], wts["wkv"], wts["wq"])
    q6 = q_s.astype(BF).reshape(bs, ts, N_GROUPS, KV_HEADS, Q_PER_KV, HEAD_DIM).transpose(0, 2, 3, 4, 1, 5)
    q6 = q6.reshape(bs, N_GROUPS, KV_HEADS, Q_PER_KV * ts, HEAD_DIM)
    qpad = jnp.stack([jnp.pad(q6[:, :, h], ((0, 0), (0, 0), (0, 0), (h * HEAD_DIM, (KV_HEADS - 1 - h) * HEAD_DIM)))
                      for h in range(KV_HEADS)], axis=2)
    caches = [jnp.transpose(c, (0, 2, 3, 4, 1)).reshape(bs, KV_WIDTH, c.shape[1])
              for c in (cache_win1, cache_win2, cache_win3)]
    bias_c, bias_n = _sample_bias_tables(rel_bias, ts)
    o_s, *win_s = _attn_sample(qpad, [a.reshape(bs, ts, KV_WIDTH) for a in kv_s], caches, bias_c, bias_n)
    o_s = o_s.reshape(bs, Q_PER_KV, ts, KV_HEADS, HEAD_DIM).transpose(0, 2, 3, 1, 4).reshape(bs * ts, D_MODEL)
    y_s = _mix_ffn([o_s], h_s, None, wts["wo_b"], norm_g[1], wts["wgu"][1], wts["wd"][1])
    win_s = [_from_positions_last(w) for w in win_s]

    return (y_p.reshape(bp, tp, D_MODEL), y_s.reshape(bs, ts, D_MODEL), gla_p[None], win_p[0], win_p[1],
            win_p[2], gla_s[None], win_s[0], win_s[1], win_s[2])
```

```python
import functools

import numpy as np
import jax
import jax.numpy as jnp
from jax import lax
from jax.experimental import pallas as pl
from jax.experimental.pallas import tpu as pltpu

BF = jnp.bfloat16
F32 = jnp.float32

D_MODEL = 1024
GLA_HEADS = 4
GLA_DK = 512
GLA_DV = 1024
DKH = GLA_DK // GLA_HEADS
DVH = GLA_DV // GLA_HEADS
GATE_RANK = 16
GATE_TAU = 16.0
GLA_CHUNK = 64
GLA_SUB = 16
GLA_UNROLL = 4
GLA_TIME_BLOCK = 1024
WINDOWS = (128, 512, 2048)
DILATIONS = (1, 4, 16)
N_GROUPS = 3
HEAD_DIM = 64
HEADS_PER_GROUP = 16
KV_HEADS = 4
Q_PER_KV = 4
N_PAIR = Q_PER_KV // 2
N_KEYS = 129
N_BUCKETS = 32
MAX_EXACT = 16
MAX_DISTANCE = 2048
D_FF = 2816
FF_SPLITS = (0, 1536, D_FF)
EPS = 1e-6
NEG = -1e30
LOG2E = 1.4426950408889634
QB = 128
LANES = 128
ROW_ALIGN = 16
DEINT = 4
KV_WIDTH = 2 * KV_HEADS * HEAD_DIM
VMEM_LIMIT_BYTES = 56 * 1024 * 1024


def _dot(a, b):
    return jnp.dot(a, b, preferred_element_type=F32)


def _dot_nt(a, b):
    return lax.dot_general(a, b, (((1,), (1,)), ((), ())), preferred_element_type=F32)


def _rms(x, g):
    return x * lax.rsqrt(jnp.mean(x * x, axis=-1, keepdims=True) + EPS) * g


def _sigmoid(x):
    return 1.0 / (1.0 + jnp.exp(-x))


def _const_spec(shape):
    nd = len(shape)
    return pl.BlockSpec(shape, lambda *_: (0,) * nd, pipeline_mode=pl.Buffered(1))


def _params(*sem):
    return pltpu.CompilerParams(dimension_semantics=sem, vmem_limit_bytes=VMEM_LIMIT_BYTES)


def _row_tile(n, want):
    tm = min(n, want)
    assert n % tm == 0
    return tm


def _row_halves(tm):
    n_sub = 2 if tm % (2 * ROW_ALIGN) == 0 else 1
    return [slice(i * tm // n_sub, (i + 1) * tm // n_sub) for i in range(n_sub)]


def _gla_in_body(x_ref, ng_ref, wm_ref, wa_ref, wa2_ref, ba_ref, q_ref, k_ref, v_ref, r_ref, g_ref):
    subs = _row_halves(x_ref.shape[0])
    xn = [_rms(x_ref[rows, :], ng_ref[0:1, :]).astype(BF) for rows in subs]
    a = [_dot(xi, wa_ref[...]).astype(BF) for xi in xn]
    z = [_dot(ai, wa2_ref[...]) + ba_ref[...] for ai in a]
    for rows, zi in zip(subs, z):
        g_ref[rows, :] = (jnp.minimum(zi, 0.0) - jnp.log(1.0 + jnp.exp(-jnp.abs(zi)))) * (1.0 / GATE_TAU)
    for rows, xi in zip(subs, xn):
        q_ref[rows, :] = _dot(xi, wm_ref[:, 0:GLA_DK]) * (DKH ** -0.5)
        k_ref[rows, :] = _dot(xi, wm_ref[:, GLA_DK:2 * GLA_DK])
        v_ref[rows, :] = _dot(xi, wm_ref[:, 2 * GLA_DK:2 * GLA_DK + GLA_DV]).astype(v_ref.dtype)
        r_ref[rows, :] = _dot(xi, wm_ref[:, 2 * GLA_DK + GLA_DV:2 * GLA_DK + 2 * GLA_DV])


def _gla_in(x2, ng, wm, wa, wa2, ba, v_dtype):
    n = x2.shape[0]
    tm = _row_tile(n, 512)
    row = lambda w: pl.BlockSpec((tm, w), lambda i: (i, 0))
    return pl.pallas_call(
        _gla_in_body,
        grid=(n // tm,),
        in_specs=[row(D_MODEL), _const_spec(ng.shape), _const_spec(wm.shape), _const_spec(wa.shape),
                  _const_spec(wa2.shape), _const_spec(ba.shape)],
        out_specs=[row(GLA_DK), row(GLA_DK), row(GLA_DV), row(GLA_DV), row(GLA_DK)],
        out_shape=[jax.ShapeDtypeStruct((n, w), dt) for w, dt in
                   ((GLA_DK, F32), (GLA_DK, F32), (GLA_DV, v_dtype), (GLA_DV, F32), (GLA_DK, F32))],
        compiler_params=_params("parallel"),
        name="gla_in",
    )(x2, ng, wm, wa, wa2, ba)


def _gla_body(c_real, q_ref, k_ref, v_ref, g_ref, s0_ref, o_ref, st_ref, s_scr):
    t = pl.program_id(1)
    c = min(GLA_CHUNK, max(c_real, GLA_SUB))
    nsb = c // GLA_SUB
    n_chunks = q_ref.shape[1] // c_real

    @pl.when(t == 0)
    def _():
        s_scr[...] = s0_ref[0]

    gc = min(n_chunks, GLA_UNROLL)
    assert n_chunks % gc == 0
    ri = lax.broadcasted_iota(jnp.int32, (gc * c, gc * c), 0)
    ci = lax.broadcasted_iota(jnp.int32, (gc * c, gc * c), 1)
    tril_bf = jnp.where((ri >= ci) & (ri // c == ci // c), 1.0, 0.0).astype(BF)
    ri2 = lax.broadcasted_iota(jnp.int32, (c, LANES), 0)
    ci2 = lax.broadcasted_iota(jnp.int32, (c, LANES), 1)
    causal = ri2 >= ci2

    def pad_rows(a, rows):
        if a.shape[0] == rows:
            return a
        return jnp.concatenate([a, jnp.zeros((rows - a.shape[0], a.shape[1]), a.dtype)], axis=0)

    heads = range(GLA_HEADS)
    ks = [slice(h * DKH, (h + 1) * DKH) for h in heads]
    vs = [slice(h * DVH, (h + 1) * DVH) for h in heads]
    units = [(j, h) for j in range(gc) for h in heads]

    def group(idx, carry):
        span = gc * c_real
        grows = pl.ds(pl.multiple_of(idx * span, span), span)
        rows = [pl.ds(pl.multiple_of(idx * span + j * c_real, c_real), c_real) for j in range(gc)]
        g_all = pad_rows(g_ref[0, grows, :], gc * c)
        g_hi = g_all.astype(BF)
        g_lo = (g_all - g_hi.astype(F32)).astype(BF)
        b_all = _dot(tril_bf, g_hi) + _dot(tril_bf, g_lo)
        b = {(j, h): b_all[j * c:(j + 1) * c, ks[h]] for j, h in units}
        qh = {(j, h): pad_rows(q_ref[0, rows[j], ks[h]], c) for j, h in units}
        kh = {(j, h): pad_rows(k_ref[0, rows[j], ks[h]], c) for j, h in units}
        vh = {(j, h): pad_rows(v_ref[0, rows[j], vs[h]], LANES) for j, h in units}
        b_last = {u: b[u][c - 1:c, :] for u in units}
        scores = {}
        for u in units:
            qparts, kparts = [], []
            for sbi in range(nsb):
                lo, hi = sbi * GLA_SUB, (sbi + 1) * GLA_SUB
                ref_row = b[u][lo:lo + 1, :]
                qj = (qh[u][lo:] * jnp.exp(b[u][lo:] - ref_row)).astype(BF)
                qparts.append(jnp.concatenate([jnp.zeros((lo, DKH), BF), qj], axis=0) if lo else qj)
                kj = (kh[u][lo:hi] * jnp.exp(ref_row - b[u][lo:hi])).astype(BF)
                pieces = []
                if lo > 0:
                    pieces.append(jnp.zeros((lo, DKH), BF))
                pieces.append(kj)
                pieces.append(jnp.zeros((LANES - hi, DKH), BF))
                kparts.append(jnp.concatenate(pieces, axis=0))
            qcat = jnp.concatenate(qparts, axis=1)
            kcat = jnp.concatenate(kparts, axis=1)
            scores[u] = _dot_nt(qcat, kcat)
        upd, decay = {}, {}
        for u in units:
            k2 = pad_rows(kh[u] * jnp.exp(b_last[u] - b[u]), LANES)
            upd[u] = _dot(k2.T.astype(BF), vh[u].astype(BF))
            col = jnp.broadcast_to(jnp.exp(b_last[u]), (LANES, DKH)).T
            decay[u] = jnp.concatenate([col] * (DVH // LANES), axis=1)
        lhs = {u: jnp.concatenate([(qh[u] * jnp.exp(b[u])).astype(BF),
                                   jnp.where(causal, scores[u], 0.0).astype(BF)], axis=1) for u in units}
        v_bf = {u: vh[u].astype(BF) for u in units}
        st = [s_scr[h] for h in heads]
        for j in range(gc):
            for h in heads:
                o = _dot(lhs[j, h], jnp.concatenate([st[h].astype(BF), v_bf[j, h]], axis=0))
                o_ref[0, rows[j], vs[h]] = o[0:c_real]
            st = [st[h] * decay[j, h] + upd[j, h] for h in heads]
        for h in heads:
            s_scr[h] = st[h]
        return carry

    lax.fori_loop(0, n_chunks // gc, group, 0)

    @pl.when(t == pl.num_programs(1) - 1)
    def _():
        st_ref[0] = s_scr[...]


def _gla(q, k, v, g, s0):
    bsz, t, _ = q.shape
    c_real = min(t, GLA_CHUNK)
    tc = min(t, GLA_TIME_BLOCK)
    assert t % tc == 0 and tc % c_real == 0
    blk = lambda w: pl.BlockSpec((1, tc, w), lambda b, i: (b, i, 0))
    st_spec = pl.BlockSpec((1, GLA_HEADS, DKH, DVH), lambda b, i: (b, 0, 0, 0))
    return pl.pallas_call(
        functools.partial(_gla_body, c_real),
        grid=(bsz, t // tc),
        in_specs=[blk(GLA_DK), blk(GLA_DK), blk(GLA_DV), blk(GLA_DK), st_spec],
        out_specs=[blk(GLA_DV), st_spec],
        out_shape=[jax.ShapeDtypeStruct((bsz, t, GLA_DV), F32),
                   jax.ShapeDtypeStruct((bsz, GLA_HEADS, DKH, DVH), F32)],
        scratch_shapes=[pltpu.VMEM((GLA_HEADS, DKH, DVH), F32)],
        compiler_params=_params("parallel", "arbitrary"),
        name="gla",
    )(q, k, v, g, s0)


def _mix_ffn_body(gated, *refs):
    if gated:
        o_ref, r_ref, x_ref, gon_ref, wo_ref, ng_ref, wgu_ref, wd_ref, out_ref = refs
    else:
        m_ref, x_ref, wo_ref, ng_ref, wgu_ref, wd_ref, out_ref = refs
    subs = _row_halves(x_ref.shape[0])
    n_sub = len(subs)
    if gated:
        def mixed(rows):
            o = o_ref[rows, :]
            r = r_ref[rows, :]
            on = jnp.concatenate(
                [_rms(o[:, h * DVH:(h + 1) * DVH], gon_ref[...]) for h in range(GLA_HEADS)], axis=1)
            return on * (r * _sigmoid(r))
        m = [mixed(rows) for rows in subs]
    else:
        m = [m_ref[rows, :] for rows in subs]
    y = [_dot(mi.astype(BF), wo_ref[...]) for mi in m]
    h1 = [x_ref[rows, :] + _rms(yi, ng_ref[1:2, :]) for rows, yi in zip(subs, y)]
    u = [_rms(hi, ng_ref[2:3, :]).astype(BF) for hi in h1]
    f = [None] * n_sub
    for lo, hi in zip(FF_SPLITS[:-1], FF_SPLITS[1:]):
        gate = [_dot(ui, wgu_ref[:, lo:hi]) for ui in u]
        up = [_dot(ui, wgu_ref[:, D_FF + lo:D_FF + hi]) for ui in u]
        part = [_dot((gi * _sigmoid(gi) * pi).astype(BF), wd_ref[lo:hi, :]) for gi, pi in zip(gate, up)]
        f = [pi if fi is None else fi + pi for fi, pi in zip(f, part)]
    for rows, hi, fi in zip(subs, h1, f):
        out_ref[rows, :] = hi + _rms(fi, ng_ref[3:4, :])


def _mix_ffn(mix_inputs, x2, gon, wo, ng, wgu, wd):
    n = x2.shape[0]
    tm = _row_tile(n, 512)
    row = pl.BlockSpec((tm, D_MODEL), lambda i: (i, 0))
    gated = gon is not None
    args = list(mix_inputs) + [x2] + ([gon] if gated else []) + [wo, ng, wgu, wd]
    in_specs = [row] * (len(mix_inputs) + 1) + [_const_spec(a.shape) for a in args[len(mix_inputs) + 1:]]
    return pl.pallas_call(
        functools.partial(_mix_ffn_body, gated),
        grid=(n // tm,),
        in_specs=in_specs,
        out_specs=row,
        out_shape=jax.ShapeDtypeStruct((n, D_MODEL), F32),
        compiler_params=_params("parallel"),
        name="mix_ffn_gated" if gated else "mix_ffn",
    )(*args)


def _normed_halves(h_ref, gkv_ref, ng_ref):
    subs = _row_halves(h_ref.shape[0])
    hn = []
    for rows in subs:
        h = h_ref[rows, :]
        hn.append(h * lax.rsqrt(jnp.mean(h * h, axis=-1, keepdims=True) + EPS))
    hkv = [(hi * gkv_ref[...]).astype(BF) for hi in hn]
    hq = [(hi * ng_ref[0:1, :]).astype(BF) for hi in hn]
    return subs, hkv, hq


def _qkv_body(h_ref, gkv_ref, ng_ref, wkv_ref, wq_ref, kv0_ref, kv1_ref, kv2_ref, q_ref):
    subs, hkv, hq = _normed_halves(h_ref, gkv_ref, ng_ref)
    for rows, hi in zip(subs, hkv):
        for g, kv_ref in enumerate((kv0_ref, kv1_ref, kv2_ref)):
            kv_ref[rows, :] = _dot(hi, wkv_ref[:, g * KV_WIDTH:(g + 1) * KV_WIDTH])
    for rows, hi in zip(subs, hq):
        q_ref[rows, :] = _dot(hi, wq_ref[...])


def _qkv_seq_body(h_ref, gkv_ref, ng_ref, wkv_ref, wq_ref, kv0_ref, kv1_ref, q_ref, kvt_ref, qm_ref, kvm_ref,
                  stage_scr):
    last = N_GROUPS - 1
    wq_g = HEADS_PER_GROUP * HEAD_DIM
    subs, hkv, hq = _normed_halves(h_ref, gkv_ref, ng_ref)
    for rows, hi in zip(subs, hkv):
        for g, kv_ref in enumerate((kv0_ref, kv1_ref)):
            kv_ref[rows, :] = _dot(hi, wkv_ref[:, g * KV_WIDTH:(g + 1) * KV_WIDTH])
        kv = _dot(hi, wkv_ref[:, last * KV_WIDTH:])
        kvt_ref[0, :, rows] = kv.T
        for c in range(KV_WIDTH // LANES):
            stage_scr[wq_g // LANES + c, rows, :] = kv[:, c * LANES:(c + 1) * LANES]
    for rows, hi in zip(subs, hq):
        q_ref[rows, :] = _dot(hi, wq_ref[:, :last * wq_g])
        q_last = _dot(hi, wq_ref[:, last * wq_g:])
        for c in range(wq_g // LANES):
            stage_scr[c, rows, :] = q_last[:, c * LANES:(c + 1) * LANES]
    per_class = h_ref.shape[0] // DEINT
    for a in range(DEINT):
        picked = pl.ds(a, per_class, stride=DEINT)
        for c in range(wq_g // LANES):
            qm_ref[0, a, :, c * LANES:(c + 1) * LANES] = stage_scr[c, picked, :]
        for c in range(KV_WIDTH // LANES):
            kvm_ref[0, a, :, c * LANES:(c + 1) * LANES] = stage_scr[wq_g // LANES + c, picked, :]


def _qkv(h2, gkv, ng, wkv, wq):
    n = h2.shape[0]
    tm = _row_tile(n, 512)
    row = lambda w: pl.BlockSpec((tm, w), lambda i: (i, 0))
    nq = wq.shape[1]
    return pl.pallas_call(
        _qkv_body,
        grid=(n // tm,),
        in_specs=[row(D_MODEL), _const_spec(gkv.shape), _const_spec(ng.shape), _const_spec(wkv.shape),
                  _const_spec(wq.shape)],
        out_specs=[row(KV_WIDTH)] * N_GROUPS + [row(nq)],
        out_shape=[jax.ShapeDtypeStruct((n, KV_WIDTH), F32)] * N_GROUPS + [jax.ShapeDtypeStruct((n, nq), F32)],
        compiler_params=_params("parallel"),
        name="qkv",
    )(h2, gkv, ng, wkv, wq)


def _qkv_seq(h2, gkv, ng, wkv, wq, seq_len):
    n = h2.shape[0]
    tm = _row_tile(n, 512)
    assert seq_len % tm == 0 and tm % (DEINT * 8) == 0
    bsz, per_seq = n // seq_len, seq_len // tm
    wq_g = HEADS_PER_GROUP * HEAD_DIM
    row = lambda w: pl.BlockSpec((tm, w), lambda i: (i, 0))
    split = lambda w: pl.BlockSpec((1, DEINT, tm // DEINT, w), lambda i: (i // per_seq, 0, i % per_seq, 0))
    return pl.pallas_call(
        _qkv_seq_body,
        grid=(n // tm,),
        in_specs=[row(D_MODEL), _const_spec(gkv.shape), _const_spec(ng.shape), _const_spec(wkv.shape),
                  _const_spec(wq.shape)],
        out_specs=[row(KV_WIDTH), row(KV_WIDTH), row((N_GROUPS - 1) * wq_g),
                   pl.BlockSpec((1, KV_WIDTH, tm), lambda i: (i // per_seq, 0, i % per_seq)),
                   split(wq_g), split(KV_WIDTH)],
        out_shape=[jax.ShapeDtypeStruct((n, KV_WIDTH), F32), jax.ShapeDtypeStruct((n, KV_WIDTH), F32),
                   jax.ShapeDtypeStruct((n, (N_GROUPS - 1) * wq_g), F32),
                   jax.ShapeDtypeStruct((bsz, KV_WIDTH, seq_len), F32),
                   jax.ShapeDtypeStruct((bsz, DEINT, seq_len // DEINT, wq_g), F32),
                   jax.ShapeDtypeStruct((bsz, DEINT, seq_len // DEINT, KV_WIDTH), F32)],
        scratch_shapes=[pltpu.VMEM(((wq_g + KV_WIDTH) // LANES, tm, LANES), F32)],
        compiler_params=_params("parallel"),
        name="qkv_seq",
    )(h2, gkv, ng, wkv, wq)


def _attn_prompt_body(q00, q01, q10, q11, q20, q21, k0, k1, k2, v0, v1, v2, bias_rest_ref, bias_first_ref,
                      o_ref, o_scr, lse_scr, kt_scr, v_scr):
    t_len = k0.shape[1]
    par = pl.program_id(1) % 2
    lane = lax.broadcasted_iota(jnp.int32, (1, LANES), 1)
    low = lane < HEAD_DIM
    same = (lane >= HEAD_DIM).astype(jnp.int32) == par

    def strided_rows(start, d):
        if d > 1:
            return pl.ds(start, QB, stride=d)
        return pl.ds(start if isinstance(start, int) else pl.multiple_of(start, QB), QB)

    def block_rows(blk, n=1):
        return pl.ds(pl.multiple_of(blk * QB, QB), n * QB)

    def load_block(g, ref, d, nblk, blk):
        rho = blk // nblk
        i = blk - rho * nblk
        if g == N_GROUPS - 1:
            inner = d // DEINT
            return ref[0, rho % DEINT, pl.ds(rho // DEINT + inner * QB * i, QB, stride=inner), :]
        return ref[0, strided_rows(rho + d * QB * i, d), :]

    def prepare(g, k_ref, v_ref, d, nblk, blk):
        dst = block_rows(blk)
        own = jnp.where(same, load_block(g, v_ref, d, nblk, blk), 0.0)
        v_scr[par, dst, :] = own.astype(BF)
        v_scr[1 - par, dst, :] = pltpu.roll(own, HEAD_DIM, axis=1).astype(BF)
        own_t = jnp.where(same, load_block(g, k_ref, d, nblk, blk), 0.0).T
        kt_scr[par, blk] = own_t.astype(BF)
        kt_scr[1 - par, blk] = jnp.concatenate([own_t[HEAD_DIM:], own_t[:HEAD_DIM]], axis=0).astype(BF)

    def attend(g, q_refs, d, nblk, blk, with_prev):
        rho = blk // nblk
        rows_q = strided_rows(rho + d * QB * (blk - rho * nblk), d)
        keys = block_rows(blk - 1, 2) if with_prev else block_rows(blk)
        nk = 2 * QB if with_prev else QB
        q = jnp.concatenate([load_block(g, q_refs[p], d, nblk, blk) for p in range(N_PAIR)],
                            axis=0).astype(BF)
        key_blocks = (blk - 1, blk) if with_prev else (blk,)
        k_cat = jnp.concatenate([kt_scr[slot, kb] for slot in range(2) for kb in key_blocks], axis=1)
        bias = bias_rest_ref[g] if with_prev else bias_first_ref[g]
        s = _dot(q, k_cat) + bias
        mx, ps = [], []
        for odd in range(2):
            sh = s[:, odd * nk:(odd + 1) * nk]
            mx.append(jnp.max(sh, axis=-1, keepdims=True))
            ps.append(jnp.exp2(sh - mx[odd]).astype(BF))
        ones_lo = jnp.broadcast_to(jnp.where(low, 1.0, 0.0), (nk, LANES)).astype(BF)
        ones_hi = jnp.broadcast_to(jnp.where(low, 0.0, 1.0), (nk, LANES)).astype(BF)
        rhs = jnp.concatenate([jnp.concatenate([v_scr[0, keys, :], ones_lo], axis=1),
                               jnp.concatenate([v_scr[1, keys, :], ones_hi], axis=1)], axis=0)
        res = _dot(jnp.concatenate(ps, axis=1), rhs)
        m = jnp.where(low, mx[0], mx[1])
        for p in range(N_PAIR):
            part = slice(p * QB, (p + 1) * QB)
            o_g, l_g, m_g = res[part, :LANES], res[part, LANES:], m[part]
            if g > 0:
                slot = N_PAIR * (g - 1) + p
                o_scr[slot, rows_q, :] = o_g / l_g
                lse_scr[slot, rows_q, :] = m_g + jnp.log2(l_g)
            else:
                slots = [N_PAIR * other + p for other in range(N_GROUPS - 1)]
                lses = [lse_scr[s_, rows_q, :] for s_ in slots]
                top = jnp.maximum(jnp.maximum(m_g, lses[0]), lses[1])
                e_own = jnp.exp2(m_g - top)
                es = [jnp.exp2(ll - top) for ll in lses]
                num = e_own * o_g + es[0] * o_scr[slots[0], rows_q, :] + es[1] * o_scr[slots[1], rows_q, :]
                den = e_own * l_g + es[0] + es[1]
                o_ref[0, rows_q, p * LANES:(p + 1) * LANES] = num / den

    groups = (((q00, q01), k0, v0), ((q10, q11), k1, v1), ((q20, q21), k2, v2))
    for g in reversed(range(N_GROUPS)):
        q_refs, k_ref, v_ref = groups[g]
        d = DILATIONS[g]
        nblk = t_len // d // QB

        def prep_step(blk, carry, g=g, k_ref=k_ref, v_ref=v_ref, d=d, nblk=nblk):
            prepare(g, k_ref, v_ref, d, nblk, blk)
            return carry

        def first_step(rho, carry, g=g, q_refs=q_refs, d=d, nblk=nblk):
            attend(g, q_refs, d, nblk, rho * nblk, False)
            return carry

        def rest_step(n, carry, g=g, q_refs=q_refs, d=d, nblk=nblk):
            rho = n // (nblk - 1)
            attend(g, q_refs, d, nblk, n + rho + 1, True)
            return carry

        lax.fori_loop(0, d * nblk, prep_step, 0, unroll=8)
        lax.fori_loop(0, d, first_step, 0, unroll=min(d, 8))
        if nblk > 1:
            n_rest = d * (nblk - 1)
            lax.fori_loop(0, n_rest, rest_step, 0, unroll=n_rest // 2 if n_rest % 2 == 0 else n_rest)


def _attn_prompt(q, kvs, qm, kvm, bias_rest, bias_first):
    bsz, t, _ = q.shape
    assert t % (QB * DILATIONS[-1]) == 0
    wq = Q_PER_KV * HEAD_DIM
    heads_per_block = LANES // HEAD_DIM
    k_col = lambda h: h // heads_per_block
    v_col = lambda h: KV_HEADS // heads_per_block + h // heads_per_block
    q_specs = [pl.BlockSpec((1, t, LANES), lambda b, h, g=g, p=p: (b, 0, (g * KV_HEADS + h) * N_PAIR + p))
               for g in range(N_GROUPS - 1) for p in range(N_PAIR)]
    q_specs += [pl.BlockSpec((1, DEINT, t // DEINT, LANES), lambda b, h, p=p: (b, 0, 0, h * N_PAIR + p))
                for p in range(N_PAIR)]
    k_specs = [pl.BlockSpec((1, t, LANES), lambda b, h: (b, 0, k_col(h)))] * (N_GROUPS - 1)
    k_specs.append(pl.BlockSpec((1, DEINT, t // DEINT, LANES), lambda b, h: (b, 0, 0, k_col(h))))
    v_specs = [pl.BlockSpec((1, t, LANES), lambda b, h: (b, 0, v_col(h)))] * (N_GROUPS - 1)
    v_specs.append(pl.BlockSpec((1, DEINT, t // DEINT, LANES), lambda b, h: (b, 0, 0, v_col(h))))
    bias_specs = [pl.BlockSpec((N_GROUPS, N_PAIR * QB, tab.shape[-1]), lambda b, h: (0, h, 0))
                  for tab in (bias_rest, bias_first)]
    slots = (N_GROUPS - 1) * N_PAIR
    q_args = [q] * (N_PAIR * (N_GROUPS - 1)) + [qm] * N_PAIR
    return pl.pallas_call(
        _attn_prompt_body,
        grid=(bsz, KV_HEADS),
        in_specs=q_specs + k_specs + v_specs + bias_specs,
        out_specs=pl.BlockSpec((1, t, wq), lambda b, h: (b, 0, h)),
        out_shape=jax.ShapeDtypeStruct((bsz, t, KV_HEADS * wq), F32),
        scratch_shapes=[pltpu.VMEM((slots, t, LANES), F32)] * 2
                       + [pltpu.VMEM((2, t // QB, LANES, QB), BF), pltpu.VMEM((2, t, LANES), BF)],
        compiler_params=_params("parallel", "arbitrary"),
        name="attn_prompt",
    )(*q_args, *kvs, kvm, *kvs, kvm, bias_rest, bias_first)


def _merge_groups(o_parts, lse_parts):
    mx = jnp.maximum(jnp.maximum(lse_parts[0], lse_parts[1]), lse_parts[2])
    es = [jnp.exp2(l - mx) for l in lse_parts]
    num = es[0] * o_parts[0] + es[1] * o_parts[1] + es[2] * o_parts[2]
    return num / (es[0] + es[1] + es[2])


def _attn_sample_body(qp_ref, n0, n1, n2, c0, c1, c2, bc0, bc1, bc2, bn_ref, o_ref, w0, w1, w2):
    n_new = n0.shape[1]
    half = KV_HEADS * HEAD_DIM
    lane = lax.broadcasted_iota(jnp.int32, (1, half), 1)
    tail_lane = lax.broadcasted_iota(jnp.int32, (1, LANES), 1)
    o_parts, lse_parts = [], []
    for g, (new_ref, c_ref, bc_ref, w_ref) in enumerate(((n0, c0, bc0, w0), (n1, c1, bc1, w1), (n2, c2, bc2, w2))):
        w = c_ref.shape[2]
        new = new_ref[0]
        newt = jnp.concatenate([new, jnp.zeros((LANES - n_new, 2 * half), F32)], axis=0).T
        ct = c_ref[0]
        shifted = pltpu.roll(ct, w - n_new, axis=1)
        tail = jnp.where(tail_lane >= LANES - n_new, pltpu.roll(newt, LANES - n_new, axis=1), shifted[:, w - LANES:])
        w_ref[0] = tail if w == LANES else jnp.concatenate([shifted[:, :w - LANES], tail], axis=1)
        kc, vc = ct[:half].astype(BF), ct[half:].astype(BF)
        kn, vn = newt[:half].astype(BF), newt[half:].astype(BF)
        o_g = jnp.zeros((Q_PER_KV * n_new, half), F32)
        l_g = jnp.zeros((Q_PER_KV * n_new, half), F32)
        for h in range(KV_HEADS):
            qp = qp_ref[0, g, h]
            s_c = _dot(qp, kc) + bc_ref[h]
            s_n = _dot(qp, kn) + bn_ref[g, h]
            m = jnp.maximum(jnp.max(s_c, axis=-1, keepdims=True), jnp.max(s_n, axis=-1, keepdims=True))
            p_c = jnp.exp2(s_c - m)
            p_n = jnp.exp2(s_n - m)
            l = jnp.sum(p_c, axis=-1, keepdims=True) + jnp.sum(p_n, axis=-1, keepdims=True)
            o = (_dot_nt(p_c.astype(BF), vc) + _dot_nt(p_n.astype(BF), vn)) / l
            mine = (lane >= h * HEAD_DIM) & (lane < (h + 1) * HEAD_DIM)
            o_g = jnp.where(mine, o, o_g)
            l_g = jnp.where(mine, m + jnp.log2(l), l_g)
        o_parts.append(o_g)
        lse_parts.append(l_g)
    o_ref[0] = _merge_groups(o_parts, lse_parts)


def _attn_sample(qpad, new_kvs, caches, bias_c, bias_n):
    bsz = qpad.shape[0]
    n_new = new_kvs[0].shape[1]
    rows = Q_PER_KV * n_new
    half = KV_HEADS * HEAD_DIM
    per_b = lambda shape: pl.BlockSpec((1,) + shape[1:], lambda b: (b,) + (0,) * (len(shape) - 1))
    cache_specs = [per_b(c.shape) for c in caches]
    return pl.pallas_call(
        _attn_sample_body,
        grid=(bsz,),
        in_specs=[per_b(qpad.shape)] + [per_b(a.shape) for a in new_kvs] + cache_specs
                 + [_const_spec(b.shape) for b in bias_c] + [_const_spec(bias_n.shape)],
        out_specs=[per_b((bsz, rows, half))] + cache_specs,
        out_shape=[jax.ShapeDtypeStruct((bsz, rows, half), F32)]
                  + [jax.ShapeDtypeStruct(c.shape, F32) for c in caches],
        compiler_params=_params("parallel"),
        name="attn_sample",
    )(qpad, *new_kvs, *caches, *bias_c, bias_n)


def _t5_buckets(dist):
    d = np.asarray(dist)
    large = MAX_EXACT + (np.log(np.maximum(d, 1) / MAX_EXACT) / np.log(MAX_DISTANCE / MAX_EXACT)
                         * (N_BUCKETS - MAX_EXACT)).astype(np.int64)
    large = np.minimum(large, N_BUCKETS - 1)
    return np.where(d < MAX_EXACT, d, large).astype(np.int32)


def _group_bias(rel_bias, g):
    bk = _t5_buckets(DILATIONS[g] * np.arange(N_KEYS))
    return rel_bias[bk][:, g * HEADS_PER_GROUP:(g + 1) * HEADS_PER_GROUP].T.astype(F32) * LOG2E


def _prompt_bias_tables(rel_bias):
    period = 2 * QB
    rest, first = [], []
    for g in range(N_GROUPS):
        bv = _group_bias(rel_bias, g)
        row0 = jnp.concatenate([bv[:, ::-1], jnp.full((HEADS_PER_GROUP, period - N_KEYS), NEG, F32)], axis=1)
        wrap = jnp.concatenate([row0, row0, row0[:, :1]], axis=1)
        skew = jnp.broadcast_to(wrap[:, None, :], (HEADS_PER_GROUP, QB, 2 * period + 1))
        skew = skew.reshape(HEADS_PER_GROUP, -1)[:, :QB * 2 * period].reshape(HEADS_PER_GROUP, QB, 2 * period)
        tab = skew[:, :, period:]
        for out, part in ((rest, tab), (first, tab[:, :, QB:])):
            nk = part.shape[-1]
            pairs = part.reshape(HEADS_PER_GROUP // 2, 2, QB, nk).transpose(0, 2, 1, 3)
            out.append(pairs.reshape(HEADS_PER_GROUP // 2 * QB, 2 * nk))
    return jnp.stack(rest, axis=0), jnp.stack(first, axis=0)


def _sample_bias_tables(rel_bias, n_new):
    tabs_c, tabs_n = [], []
    for g in range(N_GROUPS):
        w, d = WINDOWS[g], DILATIONS[g]
        bv = _group_bias(rel_bias, g)
        fill = jnp.full(bv.shape, NEG, F32)
        dil = jnp.stack([bv] + [fill] * (d - 1), axis=-1).reshape(HEADS_PER_GROUP, N_KEYS * d)[:, :w + 1]
        ext = jnp.concatenate([jnp.full((HEADS_PER_GROUP, LANES - 1), NEG, F32), dil,
                               jnp.full((HEADS_PER_GROUP, n_new - 1), NEG, F32)], axis=1)
        rev = ext[:, ::-1]
        rows = [rev[:, n_new - 1 - r:n_new - 1 - r + w + LANES] for r in range(n_new)]
        tab = jnp.stack(rows, axis=1).reshape(KV_HEADS, Q_PER_KV * n_new, w + LANES)
        tabs_c.append(tab[:, :, :w])
        tabs_n.append(tab[:, :, w:])
    return tabs_c, jnp.stack(tabs_n, axis=0)


def _prep_weights(w_in_a, w_a2, b_a, w_o_a, w_kv, w_q_b, w_o_b, w_gate_up, w_down):
    n_main = 2 * GLA_DK + 2 * GLA_DV
    w_in = w_in_a[0]
    wa = jnp.pad(w_in[:, n_main:], ((0, 0), (0, LANES - GATE_RANK)))
    wa2 = jnp.pad(w_a2[0], ((0, LANES - GATE_RANK), (0, 0)))
    return dict(
        wm=w_in.astype(BF), wa=wa.astype(BF), wa2=wa2.astype(BF), ba=b_a[0][None, :],
        wo_a=w_o_a[0].astype(BF), wkv=w_kv.astype(BF),
        wq=(w_q_b[0] * (HEAD_DIM ** -0.5 * LOG2E)).astype(BF),
        wo_b=w_o_b[0].astype(BF),
        wgu=[w_gate_up[l].astype(BF) for l in range(2)], wd=[w_down[l].astype(BF) for l in range(2)])


def _from_positions_last(a):
    return jnp.transpose(a.reshape(a.shape[0], 2, KV_HEADS, HEAD_DIM, a.shape[-1]), (0, 4, 1, 2, 3))


def _layer0(x, s0, norm_g, g_onorm, wts):
    bsz, t, _ = x.shape
    x2 = x.reshape(bsz * t, D_MODEL)
    v_dtype = BF if t % GLA_CHUNK == 0 else F32
    q, k, v, r, g = _gla_in(x2, norm_g[0], wts["wm"], wts["wa"], wts["wa2"], wts["ba"], v_dtype)
    sh = lambda a: a.reshape(bsz, t, a.shape[-1])
    o, st = _gla(sh(q), sh(k), sh(v), sh(g), s0)
    h = _mix_ffn([o.reshape(bsz * t, GLA_DV), r], x2, g_onorm, wts["wo_a"], norm_g[0], wts["wgu"][0],
                 wts["wd"][0])
    return h, st


def kernel(x_prompt, x_sample, state_gla, cache_win1, cache_win2, cache_win3, norm_g, w_in_a, w_a2, b_a,
           g_onorm, w_o_a, g_kv, w_kv, w_q_b, w_o_b, rel_bias, w_gate_up, w_down):
    wts = _prep_weights(w_in_a, w_a2, b_a, w_o_a, w_kv, w_q_b, w_o_b, w_gate_up, w_down)
    gkv = g_kv[None, :]

    bp, tp, _ = x_prompt.shape
    s0p = jnp.zeros((bp, GLA_HEADS, DKH, DVH), F32)
    h_p, gla_p = _layer0(x_prompt, s0p, norm_g, g_onorm, wts)
    *kv_p, q_p, kvt_last, qm_p, kvm_p = _qkv_seq(h_p, gkv, norm_g[1], wts["wkv"], wts["wq"], tp)
    kv_p = [a.reshape(bp, tp, KV_WIDTH) for a in kv_p]
    o_p = _attn_prompt(q_p.reshape(bp, tp, -1), kv_p, qm_p, kvm_p, *_prompt_bias_tables(rel_bias))
    y_p = _mix_ffn([o_p.reshape(bp * tp, D_MODEL)], h_p, None, wts["wo_b"], norm_g[1], wts["wgu"][1],
                   wts["wd"][1])
    win_p = [kv_p[g][:, tp - min(WINDOWS[g], tp):].reshape(bp, -1, 2, KV_HEADS, HEAD_DIM)
             for g in range(N_GROUPS - 1)]
    assert WINDOWS[-1] >= tp
    win_p.append(_from_positions_last(kvt_last))

    bs, ts, _ = x_sample.shape
    h_s, gla_s = _layer0(x_sample, state_gla[0], norm_g, g_onorm, wts)
    *kv_s, q_s = _qkv(h_s, gkv, norm_g[1], wts["wkv"], wts["wq"])
    q6 = q_s.astype(BF).reshape(bs, ts, N_GROUPS, KV_HEADS, Q_PER_KV, HEAD_DIM).transpose(0, 2, 3, 4, 1, 5)
    q6 = q6.reshape(bs, N_GROUPS, KV_HEADS, Q_PER_KV * ts, HEAD_DIM)
    qpad = jnp.stack([jnp.pad(q6[:, :, h], ((0, 0), (0, 0), (0, 0), (h * HEAD_DIM, (KV_HEADS - 1 - h) * HEAD_DIM)))
                      for h in range(KV_HEADS)], axis=2)
    caches = [jnp.transpose(c, (0, 2, 3, 4, 1)).reshape(bs, KV_WIDTH, c.shape[1])
              for c in (cache_win1, cache_win2, cache_win3)]
    bias_c, bias_n = _sample_bias_tables(rel_bias, ts)
    o_s, *win_s = _attn_sample(qpad, [a.reshape(bs, ts, KV_WIDTH) for a in kv_s], caches, bias_c, bias_n)
    o_s = o_s.reshape(bs, Q_PER_KV, ts, KV_HEADS, HEAD_DIM).transpose(0, 2, 3, 1, 4).reshape(bs * ts, D_MODEL)
    y_s = _mix_ffn([o_s], h_s, None, wts["wo_b"], norm_g[1], wts["wgu"][1], wts["wd"][1])
    win_s = [_from_positions_last(w) for w in win_s]

    return (y_p.reshape(bp, tp, D_MODEL), y_s.reshape(bs, ts, D_MODEL), gla_p[None], win_p[0], win_p[1],
            win_p[2], gla_s[None], win_s[0], win_s[1], win_s[2])
```

```python
import functools

import numpy as np
import jax
import jax.numpy as jnp
from jax import lax
from jax.experimental import pallas as pl
from jax.experimental.pallas import tpu as pltpu

BF = jnp.bfloat16
F32 = jnp.float32

D_MODEL = 1024
GLA_HEADS = 4
GLA_DK = 512
GLA_DV = 1024
DKH = GLA_DK // GLA_HEADS
DVH = GLA_DV // GLA_HEADS
GATE_RANK = 16
GATE_TAU = 16.0
GLA_CHUNK = 64
GLA_SUB = 16
GLA_UNROLL = 4
GLA_TIME_BLOCK = 1024
WINDOWS = (128, 512, 2048)
DILATIONS = (1, 4, 16)
N_GROUPS = 3
HEAD_DIM = 64
HEADS_PER_GROUP = 16
KV_HEADS = 4
Q_PER_KV = 4
N_PAIR = Q_PER_KV // 2
N_KEYS = 129
N_BUCKETS = 32
MAX_EXACT = 16
MAX_DISTANCE = 2048
D_FF = 2816
FF_SPLITS = (0, 1536, D_FF)
EPS = 1e-6
NEG = -1e30
LOG2E = 1.4426950408889634
QB = 128
LANES = 128
ROW_ALIGN = 16
DEINT = 4
KV_WIDTH = 2 * KV_HEADS * HEAD_DIM
VMEM_LIMIT_BYTES = 56 * 1024 * 1024
ROW_TILE = 512


def _dot(a, b):
    return jnp.dot(a, b, preferred_element_type=F32)


def _dot_nt(a, b):
    return lax.dot_general(a, b, (((1,), (1,)), ((), ())), preferred_element_type=F32)


def _rms(x, g):
    return x * lax.rsqrt(jnp.mean(x * x, axis=-1, keepdims=True) + EPS) * g


def _sigmoid(x):
    return 1.0 / (1.0 + jnp.exp(-x))


def _const_spec(shape):
    nd = len(shape)
    return pl.BlockSpec(shape, lambda *_: (0,) * nd, pipeline_mode=pl.Buffered(1))


def _params(*sem):
    return pltpu.CompilerParams(dimension_semantics=sem, vmem_limit_bytes=VMEM_LIMIT_BYTES)


def _row_tile(n, want):
    tm = min(n, want)
    assert n % tm == 0
    return tm


def _row_halves(tm):
    n_sub = 2 if tm % (2 * ROW_ALIGN) == 0 else 1
    return [slice(i * tm // n_sub, (i + 1) * tm // n_sub) for i in range(n_sub)]


def _gla_in_body(x_ref, ng_ref, wm_ref, wa_ref, wa2_ref, ba_ref, q_ref, k_ref, v_ref, r_ref, g_ref):
    subs = _row_halves(x_ref.shape[0])
    xn = [_rms(x_ref[rows, :], ng_ref[0:1, :]).astype(BF) for rows in subs]
    a = [_dot(xi, wa_ref[...]).astype(BF) for xi in xn]
    z = [_dot(ai, wa2_ref[...]) + ba_ref[...] for ai in a]
    for rows, zi in zip(subs, z):
        g_ref[rows, :] = (jnp.minimum(zi, 0.0) - jnp.log(1.0 + jnp.exp(-jnp.abs(zi)))) * (1.0 / GATE_TAU)
    for rows, xi in zip(subs, xn):
        q_ref[rows, :] = _dot(xi, wm_ref[:, 0:GLA_DK]) * (DKH ** -0.5)
        k_ref[rows, :] = _dot(xi, wm_ref[:, GLA_DK:2 * GLA_DK])
        v_ref[rows, :] = _dot(xi, wm_ref[:, 2 * GLA_DK:2 * GLA_DK + GLA_DV]).astype(v_ref.dtype)
        r_ref[rows, :] = _dot(xi, wm_ref[:, 2 * GLA_DK + GLA_DV:2 * GLA_DK + 2 * GLA_DV])


def _gla_in(x2, ng, wm, wa, wa2, ba, v_dtype):
    n = x2.shape[0]
    tm = _row_tile(n, ROW_TILE)
    row = lambda w: pl.BlockSpec((tm, w), lambda i: (i, 0))
    return pl.pallas_call(
        _gla_in_body,
        grid=(n // tm,),
        in_specs=[row(D_MODEL), _const_spec(ng.shape), _const_spec(wm.shape), _const_spec(wa.shape),
                  _const_spec(wa2.shape), _const_spec(ba.shape)],
        out_specs=[row(GLA_DK), row(GLA_DK), row(GLA_DV), row(GLA_DV), row(GLA_DK)],
        out_shape=[jax.ShapeDtypeStruct((n, w), dt) for w, dt in
                   ((GLA_DK, F32), (GLA_DK, F32), (GLA_DV, v_dtype), (GLA_DV, F32), (GLA_DK, F32))],
        compiler_params=_params("parallel"),
        name="gla_in",
    )(x2, ng, wm, wa, wa2, ba)


def _gla_body(c_real, q_ref, k_ref, v_ref, g_ref, s0_ref, o_ref, st_ref, s_scr):
    t = pl.program_id(1)
    c = min(GLA_CHUNK, max(c_real, GLA_SUB))
    nsb = c // GLA_SUB
    n_chunks = q_ref.shape[1] // c_real

    @pl.when(t == 0)
    def _():
        s_scr[...] = s0_ref[0]

    gc = min(n_chunks, GLA_UNROLL)
    assert n_chunks % gc == 0
    ri = lax.broadcasted_iota(jnp.int32, (gc * c, gc * c), 0)
    ci = lax.broadcasted_iota(jnp.int32, (gc * c, gc * c), 1)
    tril_bf = jnp.where((ri >= ci) & (ri // c == ci // c), 1.0, 0.0).astype(BF)
    ri2 = lax.broadcasted_iota(jnp.int32, (c, LANES), 0)
    ci2 = lax.broadcasted_iota(jnp.int32, (c, LANES), 1)
    causal = ri2 >= ci2

    def pad_rows(a, rows):
        if a.shape[0] == rows:
            return a
        return jnp.concatenate([a, jnp.zeros((rows - a.shape[0], a.shape[1]), a.dtype)], axis=0)

    heads = range(GLA_HEADS)
    ks = [slice(h * DKH, (h + 1) * DKH) for h in heads]
    vs = [slice(h * DVH, (h + 1) * DVH) for h in heads]
    units = [(j, h) for j in range(gc) for h in heads]

    def group(idx, carry):
        span = gc * c_real
        grows = pl.ds(pl.multiple_of(idx * span, span), span)
        rows = [pl.ds(pl.multiple_of(idx * span + j * c_real, c_real), c_real) for j in range(gc)]
        g_all = pad_rows(g_ref[0, grows, :], gc * c)
        g_hi = g_all.astype(BF)
        g_lo = (g_all - g_hi.astype(F32)).astype(BF)
        b_all = _dot(tril_bf, g_hi) + _dot(tril_bf, g_lo)
        b = {(j, h): b_all[j * c:(j + 1) * c, ks[h]] for j, h in units}
        qh = {(j, h): pad_rows(q_ref[0, rows[j], ks[h]], c) for j, h in units}
        kh = {(j, h): pad_rows(k_ref[0, rows[j], ks[h]], c) for j, h in units}
        vh = {(j, h): pad_rows(v_ref[0, rows[j], vs[h]], LANES) for j, h in units}
        b_last = {u: b[u][c - 1:c, :] for u in units}
        scores = {}
        for u in units:
            qparts, kparts = [], []
            for sbi in range(nsb):
                lo, hi = sbi * GLA_SUB, (sbi + 1) * GLA_SUB
                ref_row = b[u][lo:lo + 1, :]
                qj = (qh[u][lo:] * jnp.exp(b[u][lo:] - ref_row)).astype(BF)
                qparts.append(jnp.concatenate([jnp.zeros((lo, DKH), BF), qj], axis=0) if lo else qj)
                kj = (kh[u][lo:hi] * jnp.exp(ref_row - b[u][lo:hi])).astype(BF)
                pieces = []
                if lo > 0:
                    pieces.append(jnp.zeros((lo, DKH), BF))
                pieces.append(kj)
                pieces.append(jnp.zeros((LANES - hi, DKH), BF))
                kparts.append(jnp.concatenate(pieces, axis=0))
            qcat = jnp.concatenate(qparts, axis=1)
            kcat = jnp.concatenate(kparts, axis=1)
            scores[u] = _dot_nt(qcat, kcat)
        upd, decay = {}, {}
        for u in units:
            k2 = pad_rows(kh[u] * jnp.exp(b_last[u] - b[u]), LANES)
            upd[u] = _dot(k2.T.astype(BF), vh[u].astype(BF))
            col = jnp.broadcast_to(jnp.exp(b_last[u]), (LANES, DKH)).T
            decay[u] = jnp.concatenate([col] * (DVH // LANES), axis=1)
        lhs = {u: jnp.concatenate([(qh[u] * jnp.exp(b[u])).astype(BF),
                                   jnp.where(causal, scores[u], 0.0).astype(BF)], axis=1) for u in units}
        v_bf = {u: vh[u].astype(BF) for u in units}
        st = [s_scr[h] for h in heads]
        for j in range(gc):
            for h in heads:
                o = _dot(lhs[j, h], jnp.concatenate([st[h].astype(BF), v_bf[j, h]], axis=0))
                o_ref[0, rows[j], vs[h]] = o[0:c_real]
            st = [st[h] * decay[j, h] + upd[j, h] for h in heads]
        for h in heads:
            s_scr[h] = st[h]
        return carry

    lax.fori_loop(0, n_chunks // gc, group, 0)

    @pl.when(t == pl.num_programs(1) - 1)
    def _():
        st_ref[0] = s_scr[...]


def _gla(q, k, v, g, s0):
    bsz, t, _ = q.shape
    c_real = min(t, GLA_CHUNK)
    tc = min(t, GLA_TIME_BLOCK)
    assert t % tc == 0 and tc % c_real == 0
    blk = lambda w: pl.BlockSpec((1, tc, w), lambda b, i: (b, i, 0))
    st_spec = pl.BlockSpec((1, GLA_HEADS, DKH, DVH), lambda b, i: (b, 0, 0, 0))
    return pl.pallas_call(
        functools.partial(_gla_body, c_real),
        grid=(bsz, t // tc),
        in_specs=[blk(GLA_DK), blk(GLA_DK), blk(GLA_DV), blk(GLA_DK), st_spec],
        out_specs=[blk(GLA_DV), st_spec],
        out_shape=[jax.ShapeDtypeStruct((bsz, t, GLA_DV), F32),
                   jax.ShapeDtypeStruct((bsz, GLA_HEADS, DKH, DVH), F32)],
        scratch_shapes=[pltpu.VMEM((GLA_HEADS, DKH, DVH), F32)],
        compiler_params=_params("parallel", "arbitrary"),
        name="gla",
    )(q, k, v, g, s0)


def _mix_ffn_body(gated, *refs):
    if gated:
        o_ref, r_ref, x_ref, gon_ref, wo_ref, ng_ref, wgu_ref, wd_ref, out_ref = refs
    else:
        m_ref, x_ref, wo_ref, ng_ref, wgu_ref, wd_ref, out_ref = refs
    subs = _row_halves(x_ref.shape[0])
    n_sub = len(subs)
    if gated:
        def mixed(rows):
            o = o_ref[rows, :]
            r = r_ref[rows, :]
            on = jnp.concatenate(
                [_rms(o[:, h * DVH:(h + 1) * DVH], gon_ref[...]) for h in range(GLA_HEADS)], axis=1)
            return on * (r * _sigmoid(r))
        m = [mixed(rows) for rows in subs]
    else:
        m = [m_ref[rows, :] for rows in subs]
    y = [_dot(mi.astype(BF), wo_ref[...]) for mi in m]
    h1 = [x_ref[rows, :] + _rms(yi, ng_ref[1:2, :]) for rows, yi in zip(subs, y)]
    u = [_rms(hi, ng_ref[2:3, :]).astype(BF) for hi in h1]
    f = [None] * n_sub
    for lo, hi in zip(FF_SPLITS[:-1], FF_SPLITS[1:]):
        gate = [_dot(ui, wgu_ref[:, lo:hi]) for ui in u]
        up = [_dot(ui, wgu_ref[:, D_FF + lo:D_FF + hi]) for ui in u]
        part = [_dot((gi * _sigmoid(gi) * pi).astype(BF), wd_ref[lo:hi, :]) for gi, pi in zip(gate, up)]
        f = [pi if fi is None else fi + pi for fi, pi in zip(f, part)]
    for rows, hi, fi in zip(subs, h1, f):
        out_ref[rows, :] = hi + _rms(fi, ng_ref[3:4, :])


def _mix_ffn(mix_inputs, x2, gon, wo, ng, wgu, wd):
    n = x2.shape[0]
    tm = _row_tile(n, ROW_TILE)
    row = pl.BlockSpec((tm, D_MODEL), lambda i: (i, 0))
    gated = gon is not None
    args = list(mix_inputs) + [x2] + ([gon] if gated else []) + [wo, ng, wgu, wd]
    in_specs = [row] * (len(mix_inputs) + 1) + [_const_spec(a.shape) for a in args[len(mix_inputs) + 1:]]
    return pl.pallas_call(
        functools.partial(_mix_ffn_body, gated),
        grid=(n // tm,),
        in_specs=in_specs,
        out_specs=row,
        out_shape=jax.ShapeDtypeStruct((n, D_MODEL), F32),
        compiler_params=_params("parallel"),
        name="mix_ffn_gated" if gated else "mix_ffn",
    )(*args)


def _normed_halves(h_ref, gkv_ref, ng_ref):
    subs = _row_halves(h_ref.shape[0])
    hn = []
    for rows in subs:
        h = h_ref[rows, :]
        hn.append(h * lax.rsqrt(jnp.mean(h * h, axis=-1, keepdims=True) + EPS))
    hkv = [(hi * gkv_ref[...]).astype(BF) for hi in hn]
    hq = [(hi * ng_ref[0:1, :]).astype(BF) for hi in hn]
    return subs, hkv, hq


def _qkv_body(h_ref, gkv_ref, ng_ref, wkv_ref, wq_ref, kv0_ref, kv1_ref, kv2_ref, q_ref):
    subs, hkv, hq = _normed_halves(h_ref, gkv_ref, ng_ref)
    for rows, hi in zip(subs, hkv):
        for g, kv_ref in enumerate((kv0_ref, kv1_ref, kv2_ref)):
            kv_ref[rows, :] = _dot(hi, wkv_ref[:, g * KV_WIDTH:(g + 1) * KV_WIDTH])
    for rows, hi in zip(subs, hq):
        q_ref[rows, :] = _dot(hi, wq_ref[...])


def _qkv_seq_body(h_ref, gkv_ref, ng_ref, wkv_ref, wq_ref, kv0_ref, kv1_ref, q_ref, kvt_ref, qm_ref, kvm_ref,
                  stage_scr):
    last = N_GROUPS - 1
    wq_g = HEADS_PER_GROUP * HEAD_DIM
    subs, hkv, hq = _normed_halves(h_ref, gkv_ref, ng_ref)
    for rows, hi in zip(subs, hkv):
        for g, kv_ref in enumerate((kv0_ref, kv1_ref)):
            kv_ref[rows, :] = _dot(hi, wkv_ref[:, g * KV_WIDTH:(g + 1) * KV_WIDTH])
        kv = _dot(hi, wkv_ref[:, last * KV_WIDTH:])
        kvt_ref[0, :, rows] = kv.T
        for c in range(KV_WIDTH // LANES):
            stage_scr[wq_g // LANES + c, rows, :] = kv[:, c * LANES:(c + 1) * LANES]
    for rows, hi in zip(subs, hq):
        q_ref[rows, :] = _dot(hi, wq_ref[:, :last * wq_g])
        q_last = _dot(hi, wq_ref[:, last * wq_g:])
        for c in range(wq_g // LANES):
            stage_scr[c, rows, :] = q_last[:, c * LANES:(c + 1) * LANES]
    per_class = h_ref.shape[0] // DEINT
    for a in range(DEINT):
        picked = pl.ds(a, per_class, stride=DEINT)
        for c in range(wq_g // LANES):
            qm_ref[0, a, :, c * LANES:(c + 1) * LANES] = stage_scr[c, picked, :]
        for c in range(KV_WIDTH // LANES):
            kvm_ref[0, a, :, c * LANES:(c + 1) * LANES] = stage_scr[wq_g // LANES + c, picked, :]


def _qkv(h2, gkv, ng, wkv, wq):
    n = h2.shape[0]
    tm = _row_tile(n, ROW_TILE)
    row = lambda w: pl.BlockSpec((tm, w), lambda i: (i, 0))
    nq = wq.shape[1]
    return pl.pallas_call(
        _qkv_body,
        grid=(n // tm,),
        in_specs=[row(D_MODEL), _const_spec(gkv.shape), _const_spec(ng.shape), _const_spec(wkv.shape),
                  _const_spec(wq.shape)],
        out_specs=[row(KV_WIDTH)] * N_GROUPS + [row(nq)],
        out_shape=[jax.ShapeDtypeStruct((n, KV_WIDTH), F32)] * N_GROUPS + [jax.ShapeDtypeStruct((n, nq), F32)],
        compiler_params=_params("parallel"),
        name="qkv",
    )(h2, gkv, ng, wkv, wq)


def _qkv_seq(h2, gkv, ng, wkv, wq, seq_len):
    n = h2.shape[0]
    tm = _row_tile(n, ROW_TILE)
    assert seq_len % tm == 0 and tm % (DEINT * 8) == 0
    bsz, per_seq = n // seq_len, seq_len // tm
    wq_g = HEADS_PER_GROUP * HEAD_DIM
    row = lambda w: pl.BlockSpec((tm, w), lambda i: (i, 0))
    split = lambda w: pl.BlockSpec((1, DEINT, tm // DEINT, w), lambda i: (i // per_seq, 0, i % per_seq, 0))
    return pl.pallas_call(
        _qkv_seq_body,
        grid=(n // tm,),
        in_specs=[row(D_MODEL), _const_spec(gkv.shape), _const_spec(ng.shape), _const_spec(wkv.shape),
                  _const_spec(wq.shape)],
        out_specs=[row(KV_WIDTH), row(KV_WIDTH), row((N_GROUPS - 1) * wq_g),
                   pl.BlockSpec((1, KV_WIDTH, tm), lambda i: (i // per_seq, 0, i % per_seq)),
                   split(wq_g), split(KV_WIDTH)],
        out_shape=[jax.ShapeDtypeStruct((n, KV_WIDTH), F32), jax.ShapeDtypeStruct((n, KV_WIDTH), F32),
                   jax.ShapeDtypeStruct((n, (N_GROUPS - 1) * wq_g), F32),
                   jax.ShapeDtypeStruct((bsz, KV_WIDTH, seq_len), F32),
                   jax.ShapeDtypeStruct((bsz, DEINT, seq_len // DEINT, wq_g), F32),
                   jax.ShapeDtypeStruct((bsz, DEINT, seq_len // DEINT, KV_WIDTH), F32)],
        scratch_shapes=[pltpu.VMEM(((wq_g + KV_WIDTH) // LANES, tm, LANES), F32)],
        compiler_params=_params("parallel"),
        name="qkv_seq",
    )(h2, gkv, ng, wkv, wq)


def _attn_prompt_body(q00, q01, q10, q11, q20, q21, k0, k1, k2, v0, v1, v2, bias_rest_ref, bias_first_ref,
                      o_ref, o_scr, lse_scr, k_scr, v_scr):
    t_len = k0.shape[1]
    par = pl.program_id(1) % 2
    lane = lax.broadcasted_iota(jnp.int32, (1, LANES), 1)
    low = lane < HEAD_DIM
    same = (lane >= HEAD_DIM).astype(jnp.int32) == par

    def strided_rows(start, d):
        if d > 1:
            return pl.ds(start, QB, stride=d)
        return pl.ds(start if isinstance(start, int) else pl.multiple_of(start, QB), QB)

    def block_rows(blk, n=1):
        return pl.ds(pl.multiple_of(blk * QB, QB), n * QB)

    def load_block(g, ref, d, nblk, blk):
        rho = blk // nblk
        i = blk - rho * nblk
        if g == N_GROUPS - 1:
            inner = d // DEINT
            return ref[0, rho % DEINT, pl.ds(rho // DEINT + inner * QB * i, QB, stride=inner), :]
        return ref[0, strided_rows(rho + d * QB * i, d), :]

    def prepare(g, k_ref, v_ref, d, nblk, blk):
        dst = block_rows(blk)
        for src_ref, dst_scr in ((k_ref, k_scr), (v_ref, v_scr)):
            own = jnp.where(same, load_block(g, src_ref, d, nblk, blk), 0.0)
            dst_scr[par, dst, :] = own.astype(BF)
            dst_scr[1 - par, dst, :] = pltpu.roll(own, HEAD_DIM, axis=1).astype(BF)

    def attend(g, q_refs, d, nblk, blk, with_prev):
        rho = blk // nblk
        rows_q = strided_rows(rho + d * QB * (blk - rho * nblk), d)
        keys = block_rows(blk - 1, 2) if with_prev else block_rows(blk)
        nk = 2 * QB if with_prev else QB
        q = jnp.concatenate([load_block(g, q_refs[p], d, nblk, blk) for p in range(N_PAIR)],
                            axis=0).astype(BF)
        k_cat = jnp.concatenate([k_scr[0, keys, :], k_scr[1, keys, :]], axis=0)
        bias = bias_rest_ref[g] if with_prev else bias_first_ref[g]
        s = _dot_nt(q, k_cat) + bias
        mx, ps = [], []
        for odd in range(2):
            sh = s[:, odd * nk:(odd + 1) * nk]
            mx.append(jnp.max(sh, axis=-1, keepdims=True))
            ps.append(jnp.exp2(sh - mx[odd]).astype(BF))
        ones_lo = jnp.broadcast_to(jnp.where(low, 1.0, 0.0), (nk, LANES)).astype(BF)
        ones_hi = jnp.broadcast_to(jnp.where(low, 0.0, 1.0), (nk, LANES)).astype(BF)
        rhs = jnp.concatenate([jnp.concatenate([v_scr[0, keys, :], ones_lo], axis=1),
                               jnp.concatenate([v_scr[1, keys, :], ones_hi], axis=1)], axis=0)
        res = _dot(jnp.concatenate(ps, axis=1), rhs)
        m = jnp.where(low, mx[0], mx[1])
        for p in range(N_PAIR):
            part = slice(p * QB, (p + 1) * QB)
            o_g, l_g, m_g = res[part, :LANES], res[part, LANES:], m[part]
            if g > 0:
                slot = N_PAIR * (g - 1) + p
                o_scr[slot, rows_q, :] = o_g / l_g
                lse_scr[slot, rows_q, :] = m_g + jnp.log2(l_g)
            else:
                slots = [N_PAIR * other + p for other in range(N_GROUPS - 1)]
                lses = [lse_scr[s_, rows_q, :] for s_ in slots]
                top = jnp.maximum(jnp.maximum(m_g, lses[0]), lses[1])
                e_own = jnp.exp2(m_g - top)
                es = [jnp.exp2(ll - top) for ll in lses]
                num = e_own * o_g + es[0] * o_scr[slots[0], rows_q, :] + es[1] * o_scr[slots[1], rows_q, :]
                den = e_own * l_g + es[0] + es[1]
                o_ref[0, rows_q, p * LANES:(p + 1) * LANES] = num / den

    groups = (((q00, q01), k0, v0), ((q10, q11), k1, v1), ((q20, q21), k2, v2))
    for g in reversed(range(N_GROUPS)):
        q_refs, k_ref, v_ref = groups[g]
        d = DILATIONS[g]
        nblk = t_len // d // QB

        def prep_step(blk, carry, g=g, k_ref=k_ref, v_ref=v_ref, d=d, nblk=nblk):
            prepare(g, k_ref, v_ref, d, nblk, blk)
            return carry

        def first_step(rho, carry, g=g, q_refs=q_refs, d=d, nblk=nblk):
            attend(g, q_refs, d, nblk, rho * nblk, False)
            return carry

        def rest_step(n, carry, g=g, q_refs=q_refs, d=d, nblk=nblk):
            rho = n // (nblk - 1)
            attend(g, q_refs, d, nblk, n + rho + 1, True)
            return carry

        lax.fori_loop(0, d * nblk, prep_step, 0, unroll=8)
        lax.fori_loop(0, d, first_step, 0, unroll=min(d, 8))
        if nblk > 1:
            n_rest = d * (nblk - 1)
            lax.fori_loop(0, n_rest, rest_step, 0, unroll=n_rest // 2 if n_rest % 2 == 0 else n_rest)


def _attn_prompt(q, kvs, qm, kvm, bias_rest, bias_first):
    bsz, t, _ = q.shape
    assert t % (QB * DILATIONS[-1]) == 0
    wq = Q_PER_KV * HEAD_DIM
    heads_per_block = LANES // HEAD_DIM
    k_col = lambda h: h // heads_per_block
    v_col = lambda h: KV_HEADS // heads_per_block + h // heads_per_block
    q_specs = [pl.BlockSpec((1, t, LANES), lambda b, h, g=g, p=p: (b, 0, (g * KV_HEADS + h) * N_PAIR + p))
               for g in range(N_GROUPS - 1) for p in range(N_PAIR)]
    q_specs += [pl.BlockSpec((1, DEINT, t // DEINT, LANES), lambda b, h, p=p: (b, 0, 0, h * N_PAIR + p))
                for p in range(N_PAIR)]
    k_specs = [pl.BlockSpec((1, t, LANES), lambda b, h: (b, 0, k_col(h)))] * (N_GROUPS - 1)
    k_specs.append(pl.BlockSpec((1, DEINT, t // DEINT, LANES), lambda b, h: (b, 0, 0, k_col(h))))
    v_specs = [pl.BlockSpec((1, t, LANES), lambda b, h: (b, 0, v_col(h)))] * (N_GROUPS - 1)
    v_specs.append(pl.BlockSpec((1, DEINT, t // DEINT, LANES), lambda b, h: (b, 0, 0, v_col(h))))
    bias_specs = [pl.BlockSpec((N_GROUPS, N_PAIR * QB, tab.shape[-1]), lambda b, h: (0, h, 0))
                  for tab in (bias_rest, bias_first)]
    slots = (N_GROUPS - 1) * N_PAIR
    q_args = [q] * (N_PAIR * (N_GROUPS - 1)) + [qm] * N_PAIR
    return pl.pallas_call(
        _attn_prompt_body,
        grid=(bsz, KV_HEADS),
        in_specs=q_specs + k_specs + v_specs + bias_specs,
        out_specs=pl.BlockSpec((1, t, wq), lambda b, h: (b, 0, h)),
        out_shape=jax.ShapeDtypeStruct((bsz, t, KV_HEADS * wq), F32),
        scratch_shapes=[pltpu.VMEM((slots, t, LANES), F32)] * 2 + [pltpu.VMEM((2, t, LANES), BF)] * 2,
        compiler_params=_params("parallel", "arbitrary"),
        name="attn_prompt",
    )(*q_args, *kvs, kvm, *kvs, kvm, bias_rest, bias_first)


def _merge_groups(o_parts, lse_parts):
    mx = jnp.maximum(jnp.maximum(lse_parts[0], lse_parts[1]), lse_parts[2])
    es = [jnp.exp2(l - mx) for l in lse_parts]
    num = es[0] * o_parts[0] + es[1] * o_parts[1] + es[2] * o_parts[2]
    return num / (es[0] + es[1] + es[2])


def _attn_sample_body(qp_ref, n0, n1, n2, c0, c1, c2, bc0, bc1, bc2, bn_ref, o_ref, w0, w1, w2):
    n_new = n0.shape[1]
    half = KV_HEADS * HEAD_DIM
    lane = lax.broadcasted_iota(jnp.int32, (1, half), 1)
    tail_lane = lax.broadcasted_iota(jnp.int32, (1, LANES), 1)
    o_parts, lse_parts = [], []
    for g, (new_ref, c_ref, bc_ref, w_ref) in enumerate(((n0, c0, bc0, w0), (n1, c1, bc1, w1), (n2, c2, bc2, w2))):
        w = c_ref.shape[2]
        new = new_ref[0]
        newt = jnp.concatenate([new, jnp.zeros((LANES - n_new, 2 * half), F32)], axis=0).T
        ct = c_ref[0]
        shifted = pltpu.roll(ct, w - n_new, axis=1)
        tail = jnp.where(tail_lane >= LANES - n_new, pltpu.roll(newt, LANES - n_new, axis=1), shifted[:, w - LANES:])
        w_ref[0] = tail if w == LANES else jnp.concatenate([shifted[:, :w - LANES], tail], axis=1)
        kc, vc = ct[:half].astype(BF), ct[half:].astype(BF)
        kn, vn = newt[:half].astype(BF), newt[half:].astype(BF)
        o_g = jnp.zeros((Q_PER_KV * n_new, half), F32)
        l_g = jnp.zeros((Q_PER_KV * n_new, half), F32)
        for h in range(KV_HEADS):
            qp = qp_ref[0, g, h]
            s_c = _dot(qp, kc) + bc_ref[h]
            s_n = _dot(qp, kn) + bn_ref[g, h]
            m = jnp.maximum(jnp.max(s_c, axis=-1, keepdims=True), jnp.max(s_n, axis=-1, keepdims=True))
            p_c = jnp.exp2(s_c - m)
            p_n = jnp.exp2(s_n - m)
            l = jnp.sum(p_c, axis=-1, keepdims=True) + jnp.sum(p_n, axis=-1, keepdims=True)
            o = (_dot_nt(p_c.astype(BF), vc) + _dot_nt(p_n.astype(BF), vn)) / l
            mine = (lane >= h * HEAD_DIM) & (lane < (h + 1) * HEAD_DIM)
            o_g = jnp.where(mine, o, o_g)
            l_g = jnp.where(mine, m + jnp.log2(l), l_g)
        o_parts.append(o_g)
        lse_parts.append(l_g)
    o_ref[0] = _merge_groups(o_parts, lse_parts)


def _attn_sample(qpad, new_kvs, caches, bias_c, bias_n):
    bsz = qpad.shape[0]
    n_new = new_kvs[0].shape[1]
    rows = Q_PER_KV * n_new
    half = KV_HEADS * HEAD_DIM
    per_b = lambda shape: pl.BlockSpec((1,) + shape[1:], lambda b: (b,) + (0,) * (len(shape) - 1))
    cache_specs = [per_b(c.shape) for c in caches]
    return pl.pallas_call(
        _attn_sample_body,
        grid=(bsz,),
        in_specs=[per_b(qpad.shape)] + [per_b(a.shape) for a in new_kvs] + cache_specs
                 + [_const_spec(b.shape) for b in bias_c] + [_const_spec(bias_n.shape)],
        out_specs=[per_b((bsz, rows, half))] + cache_specs,
        out_shape=[jax.ShapeDtypeStruct((bsz, rows, half), F32)]
                  + [jax.ShapeDtypeStruct(c.shape, F32) for c in caches],
        compiler_params=_params("parallel"),
        name="attn_sample",
    )(qpad, *new_kvs, *caches, *bias_c, bias_n)


def _t5_buckets(dist):
    d = np.asarray(dist)
    large = MAX_EXACT + (np.log(np.maximum(d, 1) / MAX_EXACT) / np.log(MAX_DISTANCE / MAX_EXACT)
                         * (N_BUCKETS - MAX_EXACT)).astype(np.int64)
    large = np.minimum(large, N_BUCKETS - 1)
    return np.where(d < MAX_EXACT, d, large).astype(np.int32)


def _group_bias(rel_bias, g):
    bk = _t5_buckets(DILATIONS[g] * np.arange(N_KEYS))
    return rel_bias[bk][:, g * HEADS_PER_GROUP:(g + 1) * HEADS_PER_GROUP].T.astype(F32) * LOG2E


def _prompt_bias_tables(rel_bias):
    period = 2 * QB
    rest, first = [], []
    for g in range(N_GROUPS):
        bv = _group_bias(rel_bias, g)
        row0 = jnp.concatenate([bv[:, ::-1], jnp.full((HEADS_PER_GROUP, period - N_KEYS), NEG, F32)], axis=1)
        wrap = jnp.concatenate([row0, row0, row0[:, :1]], axis=1)
        skew = jnp.broadcast_to(wrap[:, None, :], (HEADS_PER_GROUP, QB, 2 * period + 1))
        skew = skew.reshape(HEADS_PER_GROUP, -1)[:, :QB * 2 * period].reshape(HEADS_PER_GROUP, QB, 2 * period)
        tab = skew[:, :, period:]
        for out, part in ((rest, tab), (first, tab[:, :, QB:])):
            nk = part.shape[-1]
            pairs = part.reshape(HEADS_PER_GROUP // 2, 2, QB, nk).transpose(0, 2, 1, 3)
            out.append(pairs.reshape(HEADS_PER_GROUP // 2 * QB, 2 * nk))
    return jnp.stack(rest, axis=0), jnp.stack(first, axis=0)


def _sample_bias_tables(rel_bias, n_new):
    tabs_c, tabs_n = [], []
    for g in range(N_GROUPS):
        w, d = WINDOWS[g], DILATIONS[g]
        bv = _group_bias(rel_bias, g)
        fill = jnp.full(bv.shape, NEG, F32)
        dil = jnp.stack([bv] + [fill] * (d - 1), axis=-1).reshape(HEADS_PER_GROUP, N_KEYS * d)[:, :w + 1]
        ext = jnp.concatenate([jnp.full((HEADS_PER_GROUP, LANES - 1), NEG, F32), dil,
                               jnp.full((HEADS_PER_GROUP, n_new - 1), NEG, F32)], axis=1)
        rev = ext[:, ::-1]
        rows = [rev[:, n_new - 1 - r:n_new - 1 - r + w + LANES] for r in range(n_new)]
        tab = jnp.stack(rows, axis=1).reshape(KV_HEADS, Q_PER_KV * n_new, w + LANES)
        tabs_c.append(tab[:, :, :w])
        tabs_n.append(tab[:, :, w:])
    return tabs_c, jnp.stack(tabs_n, axis=0)


def _prep_weights(w_in_a, w_a2, b_a, w_o_a, w_kv, w_q_b, w_o_b, w_gate_up, w_down):
    n_main = 2 * GLA_DK + 2 * GLA_DV
    w_in = w_in_a[0]
    wa = jnp.pad(w_in[:, n_main:], ((0, 0), (0, LANES - GATE_RANK)))
    wa2 = jnp.pad(w_a2[0], ((0, LANES - GATE_RANK), (0, 0)))
    return dict(
        wm=w_in.astype(BF), wa=wa.astype(BF), wa2=wa2.astype(BF), ba=b_a[0][None, :],
        wo_a=w_o_a[0].astype(BF), wkv=w_kv.astype(BF),
        wq=(w_q_b[0] * (HEAD_DIM ** -0.5 * LOG2E)).astype(BF),
        wo_b=w_o_b[0].astype(BF),
        wgu=[w_gate_up[l].astype(BF) for l in range(2)], wd=[w_down[l].astype(BF) for l in range(2)])


def _from_positions_last(a):
    return jnp.transpose(a.reshape(a.shape[0], 2, KV_HEADS, HEAD_DIM, a.shape[-1]), (0, 4, 1, 2, 3))


def _layer0(x, s0, norm_g, g_onorm, wts):
    bsz, t, _ = x.shape
    x2 = x.reshape(bsz * t, D_MODEL)
    v_dtype = BF if t % GLA_CHUNK == 0 else F32
    q, k, v, r, g = _gla_in(x2, norm_g[0], wts["wm"], wts["wa"], wts["wa2"], wts["ba"], v_dtype)
    sh = lambda a: a.reshape(bsz, t, a.shape[-1])
    o, st = _gla(sh(q), sh(k), sh(v), sh(g), s0)
    h = _mix_ffn([o.reshape(bsz * t, GLA_DV), r], x2, g_onorm, wts["wo_a"], norm_g[0], wts["wgu"][0],
                 wts["wd"][0])
    return h, st


def kernel(x_prompt, x_sample, state_gla, cache_win1, cache_win2, cache_win3, norm_g, w_in_a, w_a2, b_a,
           g_onorm, w_o_a, g_kv, w_kv, w_q_b, w_o_b, rel_bias, w_gate_up, w_down):
    wts = _prep_weights(w_in_a, w_a2, b_a, w_o_a, w_kv, w_q_b, w_o_b, w_gate_up, w_down)
    gkv = g_kv[None, :]

    bp, tp, _ = x_prompt.shape
    s0p = jnp.zeros((bp, GLA_HEADS, DKH, DVH), F32)
    h_p, gla_p = _layer0(x_prompt, s0p, norm_g, g_onorm, wts)
    *kv_p, q_p, kvt_last, qm_p, kvm_p = _qkv_seq(h_p, gkv, norm_g[1], wts["wkv"], wts["wq"], tp)
    kv_p = [a.reshape(bp, tp, KV_WIDTH) for a in kv_p]
    o_p = _attn_prompt(q_p.reshape(bp, tp, -1), kv_p, qm_p, kvm_p, *_prompt_bias_tables(rel_bias))
    y_p = _mix_ffn([o_p.reshape(bp * tp, D_MODEL)], h_p, None, wts["wo_b"], norm_g[1], wts["wgu"][1],
                   wts["wd"][1])
    win_p = [kv_p[g][:, tp - min(WINDOWS[g], tp):].reshape(bp, -1, 2, KV_HEADS, HEAD_DIM)
             for g in range(N_GROUPS - 1)]
    assert WINDOWS[-1] >= tp
    win_p.append(_from_positions_last(kvt_last))

    bs, ts, _ = x_sample.shape
    h_s, gla_s = _layer0(x_sample, state_gla[0], norm_g, g_onorm, wts)
    *kv_s, q_s = _qkv(h_s, gkv, norm_g[1], wts["wkv"], wts["wq"])
    q6 = q_s.astype(BF).reshape(bs, ts, N_GROUPS, KV_HEADS, Q_PER_KV, HEAD_DIM).transpose(0, 2, 3, 4, 1, 5)
    q6 = q6.reshape(bs, N_GROUPS, KV_HEADS, Q_PER_KV * ts, HEAD_DIM)
    qpad = jnp.stack([jnp.pad(q6[:, :, h], ((0, 0), (0, 0), (0, 0), (h * HEAD_DIM, (KV_HEADS - 1 - h) * HEAD_DIM)))
                      for h in range(KV_HEADS)], axis=2)
    caches = [jnp.transpose(c, (0, 2, 3, 4, 1)).reshape(bs, KV_WIDTH, c.shape[1])
              for c in (cache_win1, cache_win2, cache_win3)]
    bias_c, bias_n = _sample_bias_tables(rel_bias, ts)
    o_s, *win_s = _attn_sample(qpad, [a.reshape(bs, ts, KV_WIDTH) for a in kv_s], caches, bias_c, bias_n)
    o_s = o_s.reshape(bs, Q_PER_KV, ts, KV_HEADS, HEAD_DIM).transpose(0, 2, 3, 1, 4).reshape(bs * ts, D_MODEL)
    y_s = _mix_ffn([o_s], h_s, None, wts["wo_b"], norm_g[1], wts["wgu"][1], wts["wd"][1])
    win_s = [_from_positions_last(w) for w in win_s]

    return (y_p.reshape(bp, tp, D_MODEL), y_s.reshape(bs, ts, D_MODEL), gla_p[None], win_p[0], win_p[1],
            win_p[2], gla_s[None], win_s[0], win_s[1], win_s[2])
```

```python
import functools

import numpy as np
import jax
import jax.numpy as jnp
from jax import lax
from jax.experimental import pallas as pl
from jax.experimental.pallas import tpu as pltpu

BF = jnp.bfloat16
F32 = jnp.float32

D_MODEL = 1024
GLA_HEADS = 4
GLA_DK = 512
GLA_DV = 1024
DKH = GLA_DK // GLA_HEADS
DVH = GLA_DV // GLA_HEADS
GATE_RANK = 16
GATE_TAU = 16.0
GLA_CHUNK = 64
GLA_SUB = 16
GLA_UNROLL = 4
GLA_TIME_BLOCK = 1024
WINDOWS = (128, 512, 2048)
DILATIONS = (1, 4, 16)
N_GROUPS = 3
HEAD_DIM = 64
HEADS_PER_GROUP = 16
KV_HEADS = 4
Q_PER_KV = 4
N_PAIR = Q_PER_KV // 2
N_KEYS = 129
N_BUCKETS = 32
MAX_EXACT = 16
MAX_DISTANCE = 2048
D_FF = 2816
FF_SPLITS = (0, 1536, D_FF)
EPS = 1e-6
NEG = -1e30
LOG2E = 1.4426950408889634
QB = 128
LANES = 128
ROW_ALIGN = 16
DEINT = 4
KV_WIDTH = 2 * KV_HEADS * HEAD_DIM
VMEM_LIMIT_BYTES = 56 * 1024 * 1024
ROW_TILE = 512


def _dot(a, b):
    return jnp.dot(a, b, preferred_element_type=F32)


def _dot_nt(a, b):
    return lax.dot_general(a, b, (((1,), (1,)), ((), ())), preferred_element_type=F32)


def _rms(x, g):
    return x * lax.rsqrt(jnp.mean(x * x, axis=-1, keepdims=True) + EPS) * g


def _sigmoid(x):
    return 1.0 / (1.0 + jnp.exp(-x))


def _const_spec(shape):
    nd = len(shape)
    return pl.BlockSpec(shape, lambda *_: (0,) * nd, pipeline_mode=pl.Buffered(1))


def _params(*sem):
    return pltpu.CompilerParams(dimension_semantics=sem, vmem_limit_bytes=VMEM_LIMIT_BYTES)


def _row_tile(n, want):
    tm = min(n, want)
    assert n % tm == 0
    return tm


def _row_halves(tm):
    n_sub = 2 if tm % (2 * ROW_ALIGN) == 0 else 1
    return [slice(i * tm // n_sub, (i + 1) * tm // n_sub) for i in range(n_sub)]


def _gla_in_body(x_ref, ng_ref, wm_ref, wa_ref, wa2_ref, ba_ref, q_ref, k_ref, v_ref, r_ref, g_ref):
    subs = _row_halves(x_ref.shape[0])
    xn = [_rms(x_ref[rows, :], ng_ref[0:1, :]).astype(BF) for rows in subs]
    a = [_dot(xi, wa_ref[...]).astype(BF) for xi in xn]
    z = [_dot(ai, wa2_ref[...]) + ba_ref[...] for ai in a]
    for rows, zi in zip(subs, z):
        g_ref[rows, :] = (jnp.minimum(zi, 0.0) - jnp.log(1.0 + jnp.exp(-jnp.abs(zi)))) * (1.0 / GATE_TAU)
    for rows, xi in zip(subs, xn):
        q_ref[rows, :] = _dot(xi, wm_ref[:, 0:GLA_DK]) * (DKH ** -0.5)
        k_ref[rows, :] = _dot(xi, wm_ref[:, GLA_DK:2 * GLA_DK])
        v_ref[rows, :] = _dot(xi, wm_ref[:, 2 * GLA_DK:2 * GLA_DK + GLA_DV]).astype(v_ref.dtype)
        r_ref[rows, :] = _dot(xi, wm_ref[:, 2 * GLA_DK + GLA_DV:2 * GLA_DK + 2 * GLA_DV])


def _gla_in(x2, ng, wm, wa, wa2, ba, v_dtype):
    n = x2.shape[0]
    tm = _row_tile(n, ROW_TILE)
    row = lambda w: pl.BlockSpec((tm, w), lambda i: (i, 0))
    return pl.pallas_call(
        _gla_in_body,
        grid=(n // tm,),
        in_specs=[row(D_MODEL), _const_spec(ng.shape), _const_spec(wm.shape), _const_spec(wa.shape),
                  _const_spec(wa2.shape), _const_spec(ba.shape)],
        out_specs=[row(GLA_DK), row(GLA_DK), row(GLA_DV), row(GLA_DV), row(GLA_DK)],
        out_shape=[jax.ShapeDtypeStruct((n, w), dt) for w, dt in
                   ((GLA_DK, F32), (GLA_DK, F32), (GLA_DV, v_dtype), (GLA_DV, F32), (GLA_DK, F32))],
        compiler_params=_params("parallel"),
        name="gla_in",
    )(x2, ng, wm, wa, wa2, ba)


def _gla_body(c_real, q_ref, k_ref, v_ref, g_ref, s0_ref, o_ref, st_ref, s_scr):
    t = pl.program_id(1)
    c = min(GLA_CHUNK, max(c_real, GLA_SUB))
    nsb = c // GLA_SUB
    n_chunks = q_ref.shape[1] // c_real

    @pl.when(t == 0)
    def _():
        s_scr[...] = s0_ref[0]

    gc = min(n_chunks, GLA_UNROLL)
    assert n_chunks % gc == 0
    ri = lax.broadcasted_iota(jnp.int32, (gc * c, gc * c), 0)
    ci = lax.broadcasted_iota(jnp.int32, (gc * c, gc * c), 1)
    tril_bf = jnp.where((ri >= ci) & (ri // c == ci // c), 1.0, 0.0).astype(BF)
    ri2 = lax.broadcasted_iota(jnp.int32, (c, LANES), 0)
    ci2 = lax.broadcasted_iota(jnp.int32, (c, LANES), 1)
    causal = ri2 >= ci2

    def pad_rows(a, rows):
        if a.shape[0] == rows:
            return a
        return jnp.concatenate([a, jnp.zeros((rows - a.shape[0], a.shape[1]), a.dtype)], axis=0)

    heads = range(GLA_HEADS)
    ks = [slice(h * DKH, (h + 1) * DKH) for h in heads]
    vs = [slice(h * DVH, (h + 1) * DVH) for h in heads]
    units = [(j, h) for j in range(gc) for h in heads]

    def group(idx, carry):
        span = gc * c_real
        grows = pl.ds(pl.multiple_of(idx * span, span), span)
        rows = [pl.ds(pl.multiple_of(idx * span + j * c_real, c_real), c_real) for j in range(gc)]
        g_all = pad_rows(g_ref[0, grows, :], gc * c)
        g_hi = g_all.astype(BF)
        g_lo = (g_all - g_hi.astype(F32)).astype(BF)
        b_all = _dot(tril_bf, g_hi) + _dot(tril_bf, g_lo)
        b = {(j, h): b_all[j * c:(j + 1) * c, ks[h]] for j, h in units}
        qh = {(j, h): pad_rows(q_ref[0, rows[j], ks[h]], c) for j, h in units}
        kh = {(j, h): pad_rows(k_ref[0, rows[j], ks[h]], c) for j, h in units}
        vh = {(j, h): pad_rows(v_ref[0, rows[j], vs[h]], LANES) for j, h in units}
        b_last = {u: b[u][c - 1:c, :] for u in units}
        scores = {}
        for u in units:
            qparts, kparts = [], []
            for sbi in range(nsb):
                lo, hi = sbi * GLA_SUB, (sbi + 1) * GLA_SUB
                ref_row = b[u][lo:lo + 1, :]
                qj = (qh[u][lo:] * jnp.exp(b[u][lo:] - ref_row)).astype(BF)
                qparts.append(jnp.concatenate([jnp.zeros((lo, DKH), BF), qj], axis=0) if lo else qj)
                kj = (kh[u][lo:hi] * jnp.exp(ref_row - b[u][lo:hi])).astype(BF)
                pieces = []
                if lo > 0:
                    pieces.append(jnp.zeros((lo, DKH), BF))
                pieces.append(kj)
                pieces.append(jnp.zeros((LANES - hi, DKH), BF))
                kparts.append(jnp.concatenate(pieces, axis=0))
            qcat = jnp.concatenate(qparts, axis=1)
            kcat = jnp.concatenate(kparts, axis=1)
            scores[u] = _dot_nt(qcat, kcat)
        upd, decay = {}, {}
        for u in units:
            k2 = pad_rows(kh[u] * jnp.exp(b_last[u] - b[u]), LANES)
            upd[u] = _dot(k2.T.astype(BF), vh[u].astype(BF))
            col = jnp.broadcast_to(jnp.exp(b_last[u]), (LANES, DKH)).T
            decay[u] = jnp.concatenate([col] * (DVH // LANES), axis=1)
        lhs = {u: jnp.concatenate([(qh[u] * jnp.exp(b[u])).astype(BF),
                                   jnp.where(causal, scores[u], 0.0).astype(BF)], axis=1) for u in units}
        v_bf = {u: vh[u].astype(BF) for u in units}
        st = [s_scr[h] for h in heads]
        for j in range(gc):
            for h in heads:
                o = _dot(lhs[j, h], jnp.concatenate([st[h].astype(BF), v_bf[j, h]], axis=0))
                o_ref[0, rows[j], vs[h]] = o[0:c_real]
            st = [st[h] * decay[j, h] + upd[j, h] for h in heads]
        for h in heads:
            s_scr[h] = st[h]
        return carry

    lax.fori_loop(0, n_chunks // gc, group, 0)

    @pl.when(t == pl.num_programs(1) - 1)
    def _():
        st_ref[0] = s_scr[...]


def _gla(q, k, v, g, s0):
    bsz, t, _ = q.shape
    c_real = min(t, GLA_CHUNK)
    tc = min(t, GLA_TIME_BLOCK)
    assert t % tc == 0 and tc % c_real == 0
    blk = lambda w: pl.BlockSpec((1, tc, w), lambda b, i: (b, i, 0))
    st_spec = pl.BlockSpec((1, GLA_HEADS, DKH, DVH), lambda b, i: (b, 0, 0, 0))
    return pl.pallas_call(
        functools.partial(_gla_body, c_real),
        grid=(bsz, t // tc),
        in_specs=[blk(GLA_DK), blk(GLA_DK), blk(GLA_DV), blk(GLA_DK), st_spec],
        out_specs=[blk(GLA_DV), st_spec],
        out_shape=[jax.ShapeDtypeStruct((bsz, t, GLA_DV), F32),
                   jax.ShapeDtypeStruct((bsz, GLA_HEADS, DKH, DVH), F32)],
        scratch_shapes=[pltpu.VMEM((GLA_HEADS, DKH, DVH), F32)],
        compiler_params=_params("parallel", "arbitrary"),
        name="gla",
    )(q, k, v, g, s0)


def _mix_ffn_body(gated, *refs):
    if gated:
        o_ref, r_ref, x_ref, gon_ref, wo_ref, ng_ref, wgu_ref, wd_ref, out_ref = refs
    else:
        m_ref, x_ref, wo_ref, ng_ref, wgu_ref, wd_ref, out_ref = refs
    subs = _row_halves(x_ref.shape[0])
    n_sub = len(subs)
    if gated:
        def mixed(rows):
            o = o_ref[rows, :]
            r = r_ref[rows, :]
            on = jnp.concatenate(
                [_rms(o[:, h * DVH:(h + 1) * DVH], gon_ref[...]) for h in range(GLA_HEADS)], axis=1)
            return on * (r * _sigmoid(r))
        m = [mixed(rows) for rows in subs]
    else:
        m = [m_ref[rows, :] for rows in subs]
    y = [_dot(mi.astype(BF), wo_ref[...]) for mi in m]
    h1 = [x_ref[rows, :] + _rms(yi, ng_ref[1:2, :]) for rows, yi in zip(subs, y)]
    u = [_rms(hi, ng_ref[2:3, :]).astype(BF) for hi in h1]
    f = [None] * n_sub
    for lo, hi in zip(FF_SPLITS[:-1], FF_SPLITS[1:]):
        gate = [_dot(ui, wgu_ref[:, lo:hi]) for ui in u]
        up = [_dot(ui, wgu_ref[:, D_FF + lo:D_FF + hi]) for ui in u]
        part = [_dot((gi * _sigmoid(gi) * pi).astype(BF), wd_ref[lo:hi, :]) for gi, pi in zip(gate, up)]
        f = [pi if fi is None else fi + pi for fi, pi in zip(f, part)]
    for rows, hi, fi in zip(subs, h1, f):
        out_ref[rows, :] = hi + _rms(fi, ng_ref[3:4, :])


def _mix_ffn(mix_inputs, x2, gon, wo, ng, wgu, wd):
    n = x2.shape[0]
    tm = _row_tile(n, ROW_TILE)
    row = pl.BlockSpec((tm, D_MODEL), lambda i: (i, 0))
    gated = gon is not None
    args = list(mix_inputs) + [x2] + ([gon] if gated else []) + [wo, ng, wgu, wd]
    in_specs = [row] * (len(mix_inputs) + 1) + [_const_spec(a.shape) for a in args[len(mix_inputs) + 1:]]
    return pl.pallas_call(
        functools.partial(_mix_ffn_body, gated),
        grid=(n // tm,),
        in_specs=in_specs,
        out_specs=row,
        out_shape=jax.ShapeDtypeStruct((n, D_MODEL), F32),
        compiler_params=_params("parallel"),
        name="mix_ffn_gated" if gated else "mix_ffn",
    )(*args)


def _normed_halves(h_ref, gkv_ref, ng_ref):
    subs = _row_halves(h_ref.shape[0])
    hn = []
    for rows in subs:
        h = h_ref[rows, :]
        hn.append(h * lax.rsqrt(jnp.mean(h * h, axis=-1, keepdims=True) + EPS))
    hkv = [(hi * gkv_ref[...]).astype(BF) for hi in hn]
    hq = [(hi * ng_ref[0:1, :]).astype(BF) for hi in hn]
    return subs, hkv, hq


def _qkv_body(h_ref, gkv_ref, ng_ref, wkv_ref, wq_ref, kv0_ref, kv1_ref, kv2_ref, q_ref):
    subs, hkv, hq = _normed_halves(h_ref, gkv_ref, ng_ref)
    for rows, hi in zip(subs, hkv):
        for g, kv_ref in enumerate((kv0_ref, kv1_ref, kv2_ref)):
            kv_ref[rows, :] = _dot(hi, wkv_ref[:, g * KV_WIDTH:(g + 1) * KV_WIDTH])
    for rows, hi in zip(subs, hq):
        q_ref[rows, :] = _dot(hi, wq_ref[...])


def _qkv_seq_body(h_ref, gkv_ref, ng_ref, wkv_ref, wq_ref, kv0_ref, kv1_ref, q_ref, kvt_ref, qm_ref, kvm_ref,
                  stage_scr):
    last = N_GROUPS - 1
    wq_g = HEADS_PER_GROUP * HEAD_DIM
    subs, hkv, hq = _normed_halves(h_ref, gkv_ref, ng_ref)
    for rows, hi in zip(subs, hkv):
        for g, kv_ref in enumerate((kv0_ref, kv1_ref)):
            kv_ref[rows, :] = _dot(hi, wkv_ref[:, g * KV_WIDTH:(g + 1) * KV_WIDTH])
        kv = _dot(hi, wkv_ref[:, last * KV_WIDTH:])
        kvt_ref[0, :, rows] = kv.T
        for c in range(KV_WIDTH // LANES):
            stage_scr[wq_g // LANES + c, rows, :] = kv[:, c * LANES:(c + 1) * LANES]
    for rows, hi in zip(subs, hq):
        q_ref[rows, :] = _dot(hi, wq_ref[:, :last * wq_g])
        q_last = _dot(hi, wq_ref[:, last * wq_g:])
        for c in range(wq_g // LANES):
            stage_scr[c, rows, :] = q_last[:, c * LANES:(c + 1) * LANES]
    per_class = h_ref.shape[0] // DEINT
    for a in range(DEINT):
        picked = pl.ds(a, per_class, stride=DEINT)
        for c in range(wq_g // LANES):
            qm_ref[0, a, :, c * LANES:(c + 1) * LANES] = stage_scr[c, picked, :]
        for c in range(KV_WIDTH // LANES):
            kvm_ref[0, a, :, c * LANES:(c + 1) * LANES] = stage_scr[wq_g // LANES + c, picked, :]


def _qkv(h2, gkv, ng, wkv, wq):
    n = h2.shape[0]
    tm = _row_tile(n, ROW_TILE)
    row = lambda w: pl.BlockSpec((tm, w), lambda i: (i, 0))
    nq = wq.shape[1]
    return pl.pallas_call(
        _qkv_body,
        grid=(n // tm,),
        in_specs=[row(D_MODEL), _const_spec(gkv.shape), _const_spec(ng.shape), _const_spec(wkv.shape),
                  _const_spec(wq.shape)],
        out_specs=[row(KV_WIDTH)] * N_GROUPS + [row(nq)],
        out_shape=[jax.ShapeDtypeStruct((n, KV_WIDTH), F32)] * N_GROUPS + [jax.ShapeDtypeStruct((n, nq), F32)],
        compiler_params=_params("parallel"),
        name="qkv",
    )(h2, gkv, ng, wkv, wq)


def _qkv_seq(h2, gkv, ng, wkv, wq, seq_len):
    n = h2.shape[0]
    tm = _row_tile(n, ROW_TILE)
    assert seq_len % tm == 0 and tm % (DEINT * 8) == 0
    bsz, per_seq = n // seq_len, seq_len // tm
    wq_g = HEADS_PER_GROUP * HEAD_DIM
    row = lambda w: pl.BlockSpec((tm, w), lambda i: (i, 0))
    split = lambda w: pl.BlockSpec((1, DEINT, tm // DEINT, w), lambda i: (i // per_seq, 0, i % per_seq, 0))
    return pl.pallas_call(
        _qkv_seq_body,
        grid=(n // tm,),
        in_specs=[row(D_MODEL), _const_spec(gkv.shape), _const_spec(ng.shape), _const_spec(wkv.shape),
                  _const_spec(wq.shape)],
        out_specs=[row(KV_WIDTH), row(KV_WIDTH), row((N_GROUPS - 1) * wq_g),
                   pl.BlockSpec((1, KV_WIDTH, tm), lambda i: (i // per_seq, 0, i % per_seq)),
                   split(wq_g), split(KV_WIDTH)],
        out_shape=[jax.ShapeDtypeStruct((n, KV_WIDTH), F32), jax.ShapeDtypeStruct((n, KV_WIDTH), F32),
                   jax.ShapeDtypeStruct((n, (N_GROUPS - 1) * wq_g), F32),
                   jax.ShapeDtypeStruct((bsz, KV_WIDTH, seq_len), F32),
                   jax.ShapeDtypeStruct((bsz, DEINT, seq_len // DEINT, wq_g), F32),
                   jax.ShapeDtypeStruct((bsz, DEINT, seq_len // DEINT, KV_WIDTH), F32)],
        scratch_shapes=[pltpu.VMEM(((wq_g + KV_WIDTH) // LANES, tm, LANES), F32)],
        compiler_params=_params("parallel"),
        name="qkv_seq",
    )(h2, gkv, ng, wkv, wq)


def _attn_prompt_body(q00, q01, q10, q11, q20, q21, k0, k1, k2, v0, v1, v2, bias_rest_ref, bias_first_ref,
                      o_ref, o_scr, lse_scr, k_scr, v_scr):
    t_len = k0.shape[1]
    par = pl.program_id(1) % 2
    lane = lax.broadcasted_iota(jnp.int32, (1, LANES), 1)
    low = lane < HEAD_DIM
    same = (lane >= HEAD_DIM).astype(jnp.int32) == par

    def strided_rows(start, d):
        if d > 1:
            return pl.ds(start, QB, stride=d)
        return pl.ds(start if isinstance(start, int) else pl.multiple_of(start, QB), QB)

    def block_rows(blk, n=1):
        return pl.ds(pl.multiple_of(blk * QB, QB), n * QB)

    def load_block(g, ref, d, nblk, blk):
        rho = blk // nblk
        i = blk - rho * nblk
        if g == N_GROUPS - 1:
            inner = d // DEINT
            return ref[0, rho % DEINT, pl.ds(rho // DEINT + inner * QB * i, QB, stride=inner), :]
        return ref[0, strided_rows(rho + d * QB * i, d), :]

    def prepare(g, k_ref, v_ref, d, nblk, blk):
        dst = block_rows(blk)
        for src_ref, dst_scr in ((k_ref, k_scr), (v_ref, v_scr)):
            own = jnp.where(same, load_block(g, src_ref, d, nblk, blk), 0.0)
            dst_scr[par, dst, :] = own.astype(BF)
            dst_scr[1 - par, dst, :] = pltpu.roll(own, HEAD_DIM, axis=1).astype(BF)

    def attend(g, q_refs, d, nblk, blk, with_prev):
        rho = blk // nblk
        rows_q = strided_rows(rho + d * QB * (blk - rho * nblk), d)
        keys = block_rows(blk - 1, 2) if with_prev else block_rows(blk)
        nk = 2 * QB if with_prev else QB
        q = jnp.concatenate([load_block(g, q_refs[p], d, nblk, blk) for p in range(N_PAIR)],
                            axis=0).astype(BF)
        k_cat = jnp.concatenate([k_scr[0, keys, :], k_scr[1, keys, :]], axis=0)
        bias = bias_rest_ref[g] if with_prev else bias_first_ref[g]
        s = _dot_nt(q, k_cat) + bias
        mx, ps = [], []
        for odd in range(2):
            sh = s[:, odd * nk:(odd + 1) * nk]
            mx.append(jnp.max(sh, axis=-1, keepdims=True))
            ps.append(jnp.exp2(sh - mx[odd]).astype(BF))
        ones_lo = jnp.broadcast_to(jnp.where(low, 1.0, 0.0), (nk, LANES)).astype(BF)
        ones_hi = jnp.broadcast_to(jnp.where(low, 0.0, 1.0), (nk, LANES)).astype(BF)
        rhs = jnp.concatenate([jnp.concatenate([v_scr[0, keys, :], ones_lo], axis=1),
                               jnp.concatenate([v_scr[1, keys, :], ones_hi], axis=1)], axis=0)
        res = _dot(jnp.concatenate(ps, axis=1), rhs)
        m = jnp.where(low, mx[0], mx[1])
        for p in range(N_PAIR):
            part = slice(p * QB, (p + 1) * QB)
            o_g, l_g, m_g = res[part, :LANES], res[part, LANES:], m[part]
            if g > 0:
                slot = N_PAIR * (g - 1) + p
                o_scr[slot, rows_q, :] = o_g / l_g
                lse_scr[slot, rows_q, :] = m_g + jnp.log2(l_g)
            else:
                slots = [N_PAIR * other + p for other in range(N_GROUPS - 1)]
                lses = [lse_scr[s_, rows_q, :] for s_ in slots]
                top = jnp.maximum(jnp.maximum(m_g, lses[0]), lses[1])
                e_own = jnp.exp2(m_g - top)
                es = [jnp.exp2(ll - top) for ll in lses]
                num = e_own * o_g + es[0] * o_scr[slots[0], rows_q, :] + es[1] * o_scr[slots[1], rows_q, :]
                den = e_own * l_g + es[0] + es[1]
                o_ref[0, rows_q, p * LANES:(p + 1) * LANES] = num / den

    groups = (((q00, q01), k0, v0), ((q10, q11), k1, v1), ((q20, q21), k2, v2))
    for g in reversed(range(N_GROUPS)):
        q_refs, k_ref, v_ref = groups[g]
        d = DILATIONS[g]
        nblk = t_len // d // QB

        def prep_step(blk, carry, g=g, k_ref=k_ref, v_ref=v_ref, d=d, nblk=nblk):
            prepare(g, k_ref, v_ref, d, nblk, blk)
            return carry

        def first_step(rho, carry, g=g, q_refs=q_refs, d=d, nblk=nblk):
            attend(g, q_refs, d, nblk, rho * nblk, False)
            return carry

        def rest_step(n, carry, g=g, q_refs=q_refs, d=d, nblk=nblk):
            rho = n // (nblk - 1)
            attend(g, q_refs, d, nblk, n + rho + 1, True)
            return carry

        lax.fori_loop(0, d * nblk, prep_step, 0, unroll=8)
        lax.fori_loop(0, d, first_step, 0, unroll=min(d, 8))
        if nblk > 1:
            n_rest = d * (nblk - 1)
            lax.fori_loop(0, n_rest, rest_step, 0, unroll=n_rest)


def _attn_prompt(q, kvs, qm, kvm, bias_rest, bias_first):
    bsz, t, _ = q.shape
    assert t % (QB * DILATIONS[-1]) == 0
    wq = Q_PER_KV * HEAD_DIM
    heads_per_block = LANES // HEAD_DIM
    k_col = lambda h: h // heads_per_block
    v_col = lambda h: KV_HEADS // heads_per_block + h // heads_per_block
    q_specs = [pl.BlockSpec((1, t, LANES), lambda b, h, g=g, p=p: (b, 0, (g * KV_HEADS + h) * N_PAIR + p))
               for g in range(N_GROUPS - 1) for p in range(N_PAIR)]
    q_specs += [pl.BlockSpec((1, DEINT, t // DEINT, LANES), lambda b, h, p=p: (b, 0, 0, h * N_PAIR + p))
                for p in range(N_PAIR)]
    k_specs = [pl.BlockSpec((1, t, LANES), lambda b, h: (b, 0, k_col(h)))] * (N_GROUPS - 1)
    k_specs.append(pl.BlockSpec((1, DEINT, t // DEINT, LANES), lambda b, h: (b, 0, 0, k_col(h))))
    v_specs = [pl.BlockSpec((1, t, LANES), lambda b, h: (b, 0, v_col(h)))] * (N_GROUPS - 1)
    v_specs.append(pl.BlockSpec((1, DEINT, t // DEINT, LANES), lambda b, h: (b, 0, 0, v_col(h))))
    bias_specs = [pl.BlockSpec((N_GROUPS, N_PAIR * QB, tab.shape[-1]), lambda b, h: (0, h, 0))
                  for tab in (bias_rest, bias_first)]
    slots = (N_GROUPS - 1) * N_PAIR
    q_args = [q] * (N_PAIR * (N_GROUPS - 1)) + [qm] * N_PAIR
    return pl.pallas_call(
        _attn_prompt_body,
        grid=(bsz, KV_HEADS),
        in_specs=q_specs + k_specs + v_specs + bias_specs,
        out_specs=pl.BlockSpec((1, t, wq), lambda b, h: (b, 0, h)),
        out_shape=jax.ShapeDtypeStruct((bsz, t, KV_HEADS * wq), F32),
        scratch_shapes=[pltpu.VMEM((slots, t, LANES), F32)] * 2 + [pltpu.VMEM((2, t, LANES), BF)] * 2,
        compiler_params=_params("parallel", "arbitrary"),
        name="attn_prompt",
    )(*q_args, *kvs, kvm, *kvs, kvm, bias_rest, bias_first)


def _merge_groups(o_parts, lse_parts):
    mx = jnp.maximum(jnp.maximum(lse_parts[0], lse_parts[1]), lse_parts[2])
    es = [jnp.exp2(l - mx) for l in lse_parts]
    num = es[0] * o_parts[0] + es[1] * o_parts[1] + es[2] * o_parts[2]
    return num / (es[0] + es[1] + es[2])


def _attn_sample_body(qp_ref, n0, n1, n2, c0, c1, c2, bc0, bc1, bc2, bn_ref, o_ref, w0, w1, w2):
    n_new = n0.shape[1]
    half = KV_HEADS * HEAD_DIM
    lane = lax.broadcasted_iota(jnp.int32, (1, half), 1)
    tail_lane = lax.broadcasted_iota(jnp.int32, (1, LANES), 1)
    o_parts, lse_parts = [], []
    for g, (new_ref, c_ref, bc_ref, w_ref) in enumerate(((n0, c0, bc0, w0), (n1, c1, bc1, w1), (n2, c2, bc2, w2))):
        w = c_ref.shape[2]
        new = new_ref[0]
        newt = jnp.concatenate([new, jnp.zeros((LANES - n_new, 2 * half), F32)], axis=0).T
        ct = c_ref[0]
        shifted = pltpu.roll(ct, w - n_new, axis=1)
        tail = jnp.where(tail_lane >= LANES - n_new, pltpu.roll(newt, LANES - n_new, axis=1), shifted[:, w - LANES:])
        w_ref[0] = tail if w == LANES else jnp.concatenate([shifted[:, :w - LANES], tail], axis=1)
        kc, vc = ct[:half].astype(BF), ct[half:].astype(BF)
        kn, vn = newt[:half].astype(BF), newt[half:].astype(BF)
        o_g = jnp.zeros((Q_PER_KV * n_new, half), F32)
        l_g = jnp.zeros((Q_PER_KV * n_new, half), F32)
        for h in range(KV_HEADS):
            qp = qp_ref[0, g, h]
            s_c = _dot(qp, kc) + bc_ref[h]
            s_n = _dot(qp, kn) + bn_ref[g, h]
            m = jnp.maximum(jnp.max(s_c, axis=-1, keepdims=True), jnp.max(s_n, axis=-1, keepdims=True))
            p_c = jnp.exp2(s_c - m)
            p_n = jnp.exp2(s_n - m)
            l = jnp.sum(p_c, axis=-1, keepdims=True) + jnp.sum(p_n, axis=-1, keepdims=True)
            o = (_dot_nt(p_c.astype(BF), vc) + _dot_nt(p_n.astype(BF), vn)) / l
            mine = (lane >= h * HEAD_DIM) & (lane < (h + 1) * HEAD_DIM)
            o_g = jnp.where(mine, o, o_g)
            l_g = jnp.where(mine, m + jnp.log2(l), l_g)
        o_parts.append(o_g)
        lse_parts.append(l_g)
    o_ref[0] = _merge_groups(o_parts, lse_parts)


def _attn_sample(qpad, new_kvs, caches, bias_c, bias_n):
    bsz = qpad.shape[0]
    n_new = new_kvs[0].shape[1]
    rows = Q_PER_KV * n_new
    half = KV_HEADS * HEAD_DIM
    per_b = lambda shape: pl.BlockSpec((1,) + shape[1:], lambda b: (b,) + (0,) * (len(shape) - 1))
    cache_specs = [per_b(c.shape) for c in caches]
    return pl.pallas_call(
        _attn_sample_body,
        grid=(bsz,),
        in_specs=[per_b(qpad.shape)] + [per_b(a.shape) for a in new_kvs] + cache_specs
                 + [_const_spec(b.shape) for b in bias_c] + [_const_spec(bias_n.shape)],
        out_specs=[per_b((bsz, rows, half))] + cache_specs,
        out_shape=[jax.ShapeDtypeStruct((bsz, rows, half), F32)]
                  + [jax.ShapeDtypeStruct(c.shape, F32) for c in caches],
        compiler_params=_params("parallel"),
        name="attn_sample",
    )(qpad, *new_kvs, *caches, *bias_c, bias_n)


def _t5_buckets(dist):
    d = np.asarray(dist)
    large = MAX_EXACT + (np.log(np.maximum(d, 1) / MAX_EXACT) / np.log(MAX_DISTANCE / MAX_EXACT)
                         * (N_BUCKETS - MAX_EXACT)).astype(np.int64)
    large = np.minimum(large, N_BUCKETS - 1)
    return np.where(d < MAX_EXACT, d, large).astype(np.int32)


def _group_bias(rel_bias, g):
    bk = _t5_buckets(DILATIONS[g] * np.arange(N_KEYS))
    return rel_bias[bk][:, g * HEADS_PER_GROUP:(g + 1) * HEADS_PER_GROUP].T.astype(F32) * LOG2E


def _prompt_bias_tables(rel_bias):
    period = 2 * QB
    rest, first = [], []
    for g in range(N_GROUPS):
        bv = _group_bias(rel_bias, g)
        row0 = jnp.concatenate([bv[:, ::-1], jnp.full((HEADS_PER_GROUP, period - N_KEYS), NEG, F32)], axis=1)
        wrap = jnp.concatenate([row0, row0, row0[:, :1]], axis=1)
        skew = jnp.broadcast_to(wrap[:, None, :], (HEADS_PER_GROUP, QB, 2 * period + 1))
        skew = skew.reshape(HEADS_PER_GROUP, -1)[:, :QB * 2 * period].reshape(HEADS_PER_GROUP, QB, 2 * period)
        tab = skew[:, :, period:]
        for out, part in ((rest, tab), (first, tab[:, :, QB:])):
            nk = part.shape[-1]
            pairs = part.reshape(HEADS_PER_GROUP // 2, 2, QB, nk).transpose(0, 2, 1, 3)
            out.append(pairs.reshape(HEADS_PER_GROUP // 2 * QB, 2 * nk))
    return jnp.stack(rest, axis=0), jnp.stack(first, axis=0)


def _sample_bias_tables(rel_bias, n_new):
    tabs_c, tabs_n = [], []
    for g in range(N_GROUPS):
        w, d = WINDOWS[g], DILATIONS[g]
        bv = _group_bias(rel_bias, g)
        fill = jnp.full(bv.shape, NEG, F32)
        dil = jnp.stack([bv] + [fill] * (d - 1), axis=-1).reshape(HEADS_PER_GROUP, N_KEYS * d)[:, :w + 1]
        ext = jnp.concatenate([jnp.full((HEADS_PER_GROUP, LANES - 1), NEG, F32), dil,
                               jnp.full((HEADS_PER_GROUP, n_new - 1), NEG, F32)], axis=1)
        rev = ext[:, ::-1]
        rows = [rev[:, n_new - 1 - r:n_new - 1 - r + w + LANES] for r in range(n_new)]
        tab = jnp.stack(rows, axis=1).reshape(KV_HEADS, Q_PER_KV * n_new, w + LANES)
        tabs_c.append(tab[:, :, :w])
        tabs_n.append(tab[:, :, w:])
    return tabs_c, jnp.stack(tabs_n, axis=0)


def _prep_weights(w_in_a, w_a2, b_a, w_o_a, w_kv, w_q_b, w_o_b, w_gate_up, w_down):
    n_main = 2 * GLA_DK + 2 * GLA_DV
    w_in = w_in_a[0]
    wa = jnp.pad(w_in[:, n_main:], ((0, 0), (0, LANES - GATE_RANK)))
    wa2 = jnp.pad(w_a2[0], ((0, LANES - GATE_RANK), (0, 0)))
    return dict(
        wm=w_in.astype(BF), wa=wa.astype(BF), wa2=wa2.astype(BF), ba=b_a[0][None, :],
        wo_a=w_o_a[0].astype(BF), wkv=w_kv.astype(BF),
        wq=(w_q_b[0] * (HEAD_DIM ** -0.5 * LOG2E)).astype(BF),
        wo_b=w_o_b[0].astype(BF),
        wgu=[w_gate_up[l].astype(BF) for l in range(2)], wd=[w_down[l].astype(BF) for l in range(2)])


def _from_positions_last(a):
    return jnp.transpose(a.reshape(a.shape[0], 2, KV_HEADS, HEAD_DIM, a.shape[-1]), (0, 4, 1, 2, 3))


def _layer0(x, s0, norm_g, g_onorm, wts):
    bsz, t, _ = x.shape
    x2 = x.reshape(bsz * t, D_MODEL)
    v_dtype = BF if t % GLA_CHUNK == 0 else F32
    q, k, v, r, g = _gla_in(x2, norm_g[0], wts["wm"], wts["wa"], wts["wa2"], wts["ba"], v_dtype)
    sh = lambda a: a.reshape(bsz, t, a.shape[-1])
    o, st = _gla(sh(q), sh(k), sh(v), sh(g), s0)
    h = _mix_ffn([o.reshape(bsz * t, GLA_DV), r], x2, g_onorm, wts["wo_a"], norm_g[0], wts["wgu"][0],
                 wts["wd"][0])
    return h, st


def kernel(x_prompt, x_sample, state_gla, cache_win1, cache_win2, cache_win3, norm_g, w_in_a, w_a2, b_a,
           g_onorm, w_o_a, g_kv, w_kv, w_q_b, w_o_b, rel_bias, w_gate_up, w_down):
    wts = _prep_weights(w_in_a, w_a2, b_a, w_o_a, w_kv, w_q_b, w_o_b, w_gate_up, w_down)
    gkv = g_kv[None, :]

    bp, tp, _ = x_prompt.shape
    s0p = jnp.zeros((bp, GLA_HEADS, DKH, DVH), F32)
    h_p, gla_p = _layer0(x_prompt, s0p, norm_g, g_onorm, wts)
    *kv_p, q_p, kvt_last, qm_p, kvm_p = _qkv_seq(h_p, gkv, norm_g[1], wts["wkv"], wts["wq"], tp)
    kv_p = [a.reshape(bp, tp, KV_WIDTH) for a in kv_p]
    o_p = _attn_prompt(q_p.reshape(bp, tp, -1), kv_p, qm_p, kvm_p, *_prompt_bias_tables(rel_bias))
    y_p = _mix_ffn([o_p.reshape(bp * tp, D_MODEL)], h_p, None, wts["wo_b"], norm_g[1], wts["wgu"][1],
                   wts["wd"][1])
    win_p = [kv_p[g][:, tp - min(WINDOWS[g], tp):].reshape(bp, -1, 2, KV_HEADS, HEAD_DIM)
             for g in range(N_GROUPS - 1)]
    assert WINDOWS[-1] >= tp
    win_p.append(_from_positions_last(kvt_last))

    bs, ts, _ = x_sample.shape
    h_s, gla_s = _layer0(x_sample, state_gla[0], norm_g, g_onorm, wts)
    *kv_s, q_s = _qkv(h_s, gkv, norm_g[1], wts["wkv"], wts["wq"])
    q6 = q_s.astype(BF).reshape(bs, ts, N_GROUPS, KV_HEADS, Q_PER_KV, HEAD_DIM).transpose(0, 2, 3, 4, 1, 5)
    q6 = q6.reshape(bs, N_GROUPS, KV_HEADS, Q_PER_KV * ts, HEAD_DIM)
    qpad = jnp.stack([jnp.pad(q6[:, :, h], ((0, 0), (0, 0), (0, 0), (h * HEAD_DIM, (KV_HEADS - 1 - h) * HEAD_DIM)))
                      for h in range(KV_HEADS)], axis=2)
    caches = [jnp.transpose(c, (0, 2, 3, 4, 1)).reshape(bs, KV_WIDTH, c.shape[1])
              for c in (cache_win1, cache_win2, cache_win3)]
    bias_c, bias_n = _sample_bias_tables(rel_bias, ts)
    o_s, *win_s = _attn_sample(qpad, [a.reshape(bs, ts, KV_WIDTH) for a in kv_s], caches, bias_c, bias_n)
    o_s = o_s.reshape(bs, Q_PER_KV, ts, KV_HEADS, HEAD_DIM).transpose(0, 2, 3, 1, 4).reshape(bs * ts, D_MODEL)
    y_s = _mix_ffn([o_s], h_s, None, wts["wo_b"], norm_g[1], wts["wgu"][1], wts["wd"][1])
    win_s = [_from_positions_last(w) for w in win_s]

    return (y_p.reshape(bp, tp, D_MODEL), y_s.reshape(bs, ts, D_MODEL), gla_p[None], win_p[0], win_p[1],
            win_p[2], gla_s[None], win_s[0], win_s[1], win_s[2])
```
